```python
import math
import jax
import jax.numpy as jnp
from jax import lax
import numpy as np

D_MODEL = 4096
BATCH = 4
SEQ = 2048
DEPTH = 2

GRID_W = 64
CTX_LEN = 256
EPS = 1e-6
MIX_W = D_MODEL
DN_DK = 128
DN_DV = 128
DN_HEADS = (MIX_W // 2) // DN_DV
QK_W = DN_HEADS * DN_DK
DN_W = DN_HEADS * DN_DV
QKV_W = 2 * QK_W + DN_W
SHORT_CONV = 5
CHUNK = 64
CF_W = MIX_W - DN_W
CF_GROUP_W = 128
CF_GROUPS = CF_W // CF_GROUP_W
CF_KERNEL = 31
N_EXPERTS = 16
N_GROUPS = 4
EXPERTS_PER_GROUP = N_EXPERTS // N_GROUPS
TOP_K = 2
D_FF = D_MODEL // 4
MOE_BLOCK = 128
Z_OFF = QKV_W
BETA_OFF = Z_OFF + DN_W
ALPHA_OFF = BETA_OFF + 2 * DN_HEADS
CF_OFF = ALPHA_OFF + 2 * DN_HEADS
IN_COLS = CF_OFF + 2 * CF_W

kernel_name = 'hybrid_deltanet_conformer_moe_dit'


def rmsnorm(x, g):
    xf = x.astype(jnp.float32)
    y = xf * lax.rsqrt(jnp.mean(xf * xf, axis=-1, keepdims=True) + EPS)
    return (y * g).astype(x.dtype)


def modulate(h, shift, scale):
    return h * (1 + scale) + shift


def l2norm(x):
    xf = x.astype(jnp.float32)
    return xf * lax.rsqrt(jnp.sum(xf * xf, axis=-1, keepdims=True) + EPS)


def depthwise_conv1d(x, w):
    pad = (w.shape[0] - 1) // 2
    return lax.conv_general_dilated(
        x, w[:, None, :].astype(x.dtype), window_strides=(1,), padding=((pad, pad),),
        dimension_numbers=('NWC', 'WIO', 'NWC'), feature_group_count=x.shape[-1])


def gated_delta_chunked(q, k, v, g, beta, s0):
    b, l, h, dk = q.shape
    dv = v.shape[-1]
    n = l // CHUNK

    def to_chunks(t):
        t = t.astype(jnp.float32).reshape((b, n, CHUNK, h) + t.shape[3:])
        return jnp.moveaxis(t, (1, 3), (0, 2))

    qc = to_chunks(q) * (dk ** -0.5)
    kc, vc, bc = to_chunks(k), to_chunks(v), to_chunks(beta)
    gcum = jnp.cumsum(to_chunks(g), axis=-1)
    lower = jnp.tril(jnp.ones((CHUNK, CHUNK), bool))
    strict = jnp.tril(jnp.ones((CHUNK, CHUNK), bool), -1)
    decay = jnp.exp(jnp.where(lower, gcum[..., :, None] - gcum[..., None, :], -jnp.inf))
    kb = kc * bc[..., None]
    a_mat = jnp.where(strict, jnp.einsum('nbhid,nbhjd->nbhij', kb, kc) * decay, 0.0)
    rhs = jnp.concatenate([vc * bc[..., None], kb * jnp.exp(gcum)[..., None]], axis=-1)
    sol = lax.linalg.triangular_solve(a_mat + jnp.eye(CHUNK, dtype=jnp.float32), rhs,
                                      left_side=True, lower=True, unit_diagonal=True)
    u, w = sol[..., :dv], sol[..., dv:]
    qk = jnp.where(lower, jnp.einsum('nbhid,nbhjd->nbhij', qc, kc) * decay, 0.0)
    q_dec = qc * jnp.exp(gcum)[..., None]
    k_dec = kc * jnp.exp(gcum[..., -1:] - gcum)[..., None]
    chunk_decay = jnp.exp(gcum[..., -1])

    def step(s, inp):
        u_i, w_i, qk_i, qd_i, kd_i, cd_i = inp
        v_new = u_i - jnp.einsum('bhck,bhkv->bhcv', w_i, s)
        o_i = jnp.einsum('bhck,bhkv->bhcv', qd_i, s) + jnp.einsum('bhij,bhjv->bhiv', qk_i, v_new)
        s = s * cd_i[..., None, None] + jnp.einsum('bhck,bhcv->bhkv', kd_i, v_new)
        return s, o_i

    s_fin, o = lax.scan(step, s0.astype(jnp.float32), (u, w, qk, q_dec, k_dec, chunk_decay))
    o = jnp.moveaxis(o, (0, 2), (1, 3)).reshape(b, l, h, dv)
    return o, s_fin


def deltanet_inputs(p, conv_w, a_log, dt_bias):
    b, l, _ = p.shape
    qkv = jax.nn.silu(depthwise_conv1d(p[..., :QKV_W], conv_w))
    q = l2norm(qkv[..., :QK_W].reshape(b, l, DN_HEADS, DN_DK))
    k = l2norm(qkv[..., QK_W:2 * QK_W].reshape(b, l, DN_HEADS, DN_DK))
    v = qkv[..., 2 * QK_W:].reshape(b, l, DN_HEADS, DN_DV)
    beta = jax.nn.sigmoid(p[..., BETA_OFF:ALPHA_OFF].astype(jnp.float32)).reshape(b, l, 2, DN_HEADS)
    alpha = p[..., ALPHA_OFF:CF_OFF].astype(jnp.float32).reshape(b, l, 2, DN_HEADS)
    g = -jnp.exp(a_log.astype(jnp.float32)) * jax.nn.softplus(alpha + dt_bias.astype(jnp.float32))
    return q, k, v, g, beta


def flip(t):
    return jnp.flip(t, axis=1)


def bidirectional_delta(q, k, v, g, beta, s0_fwd, s0_bwd):
    o_f, s_f = gated_delta_chunked(q, k, v, g[:, :, 0], beta[:, :, 0], s0_fwd)
    o_b, s_b = gated_delta_chunked(flip(q), flip(k), flip(v), flip(g[:, :, 1]),
                                   flip(beta[:, :, 1]), s0_bwd)
    return o_f + flip(o_b), s_f, s_b


def gated_head_norm(o, z, g):
    of = o.astype(jnp.float32)
    of = of * lax.rsqrt(jnp.mean(of * of, axis=-1, keepdims=True) + EPS) * g.astype(jnp.float32)
    zf = z.astype(jnp.float32).reshape(o.shape)
    return (of * jax.nn.silu(zf)).reshape(z.shape).astype(z.dtype)


def deltanet_heads(p_ctx, p_lat, conv_w, a_log, dt_bias, norm_g, need_ctx):
    b = p_ctx.shape[0]
    zero = jnp.zeros((b, DN_HEADS, DN_DK, DN_DV), jnp.float32)
    o_ctx, s_f, s_b = bidirectional_delta(*deltanet_inputs(p_ctx, conv_w, a_log, dt_bias), zero, zero)
    o_lat, _, _ = bidirectional_delta(*deltanet_inputs(p_lat, conv_w, a_log, dt_bias), s_f, s_b)
    y_lat = gated_head_norm(o_lat, p_lat[..., Z_OFF:BETA_OFF], norm_g)
    y_ctx = gated_head_norm(o_ctx, p_ctx[..., Z_OFF:BETA_OFF], norm_g) if need_ctx else None
    return y_ctx, y_lat


def grid_depthwise_conv(u, w):
    b, l, ch = u.shape
    rows = l // GRID_W
    half = ch // 2
    uh = u[..., :half].reshape(b * rows, GRID_W, half)
    yh = depthwise_conv1d(uh, w[:, :half]).reshape(b, l, half)
    uv = u[..., half:].reshape(b, rows, GRID_W, half).transpose(0, 2, 1, 3).reshape(b * GRID_W, rows, half)
    yv = depthwise_conv1d(uv, w[:, half:]).reshape(b, GRID_W, rows, half).transpose(0, 2, 1, 3).reshape(b, l, half)
    return jnp.concatenate([yh, yv], axis=-1)


def group_layernorm(x, g, bias):
    xf = x.astype(jnp.float32).reshape(x.shape[:-1] + (CF_GROUPS, CF_GROUP_W))
    xc = xf - jnp.mean(xf, axis=-1, keepdims=True)
    y = (xc * lax.rsqrt(jnp.mean(xc * xc, axis=-1, keepdims=True) + EPS)).reshape(x.shape)
    return (y * g + bias).astype(x.dtype)


def conformer_heads(p, dw_w, ln_g, ln_b, on_grid):
    val, gate = jnp.split(p, 2, axis=-1)
    u = val * jax.nn.sigmoid(gate)
    u = grid_depthwise_conv(u, dw_w) if on_grid else depthwise_conv1d(u, dw_w)
    return jax.nn.silu(group_layernorm(u, ln_g, ln_b))


def moe_ffn(h, router_w, router_bias, w_gate, w_up, w_down):
    t, d = h.shape
    scores = jax.nn.sigmoid((h @ router_w).astype(jnp.float32))
    grouped = (scores + router_bias.astype(jnp.float32)).reshape(t, N_GROUPS, EXPERTS_PER_GROUP)
    group_score = jnp.sum(lax.top_k(grouped, TOP_K)[0], axis=-1)
    best = jnp.argmax(group_score, axis=-1)
    in_group = jnp.arange(N_GROUPS)[None, :] == best[:, None]
    masked = jnp.where(in_group[..., None], grouped, -jnp.inf).reshape(t, N_EXPERTS)
    _, idx = lax.top_k(masked, TOP_K)
    wts = jnp.take_along_axis(scores, idx, axis=-1)
    wts = wts / jnp.sum(wts, axis=-1, keepdims=True)
    n = t * TOP_K
    flat_e = idx.reshape(-1)
    order = jnp.argsort(flat_e)
    e_sorted = flat_e[order]
    tok_sorted = (order // TOP_K).astype(jnp.int32)
    counts = jnp.bincount(flat_e, length=N_EXPERTS)
    padded = (counts + MOE_BLOCK - 1) // MOE_BLOCK * MOE_BLOCK
    pad_end = jnp.cumsum(padded)
    pad_start = pad_end - padded
    start = jnp.cumsum(counts) - counts
    dest = pad_start[e_sorted] + jnp.arange(n) - start[e_sorted]
    n_blocks = -(-n // MOE_BLOCK) + N_EXPERTS
    cap = n_blocks * MOE_BLOCK
    slot_tok = jnp.full((cap,), t, jnp.int32).at[dest].set(tok_sorted)
    slot_w = jnp.zeros((cap,), jnp.float32).at[dest].set(wts.reshape(-1)[order])
    block_expert = jnp.minimum(
        jnp.searchsorted(pad_end, jnp.arange(n_blocks) * MOE_BLOCK, side='right'), N_EXPERTS - 1)
    h_pad = jnp.concatenate([h, jnp.zeros((1, d), h.dtype)], axis=0)

    def run_block(args):
        toks, e = args
        xb = h_pad[toks]
        return (jax.nn.silu(xb @ w_gate[e]) * (xb @ w_up[e])) @ w_down[e]

    y = lax.map(run_block, (slot_tok.reshape(n_blocks, MOE_BLOCK), block_expert))
    y = y.reshape(cap, d) * slot_w[:, None].astype(y.dtype)
    return jnp.zeros((t + 1, d), y.dtype).at[slot_tok].add(y)[:t]


def hybrid_layer(x, ctx, mod_lat, mod_ctx, norm1_g, norm2_g, w_in, conv_qkv_w, a_log, dt_bias,
                 dn_norm_g, cf_dw_w, cf_ln_g, cf_ln_b, w_out, router_w, router_bias,
                 w_gate, w_up, w_down, update_ctx):
    ctx_len = ctx.shape[1]
    d = x.shape[-1]
    sh1_l, sc1_l, g1_l, sh2_l, sc2_l, g2_l = jnp.split(mod_lat[:, None, :], 6, axis=-1)
    sh1_c, sc1_c, g1_c, sh2_c, sc2_c, g2_c = jnp.split(mod_ctx, 6, axis=-1)

    h = jnp.concatenate([modulate(rmsnorm(ctx, norm1_g), sh1_c, sc1_c),
                         modulate(rmsnorm(x, norm1_g), sh1_l, sc1_l)], axis=1)
    p = h @ w_in
    p_ctx, p_lat = p[:, :ctx_len], p[:, ctx_len:]
    dn_ctx, dn_lat = deltanet_heads(p_ctx, p_lat, conv_qkv_w, a_log, dt_bias, dn_norm_g, update_ctx)
    cf_lat = conformer_heads(p_lat[..., CF_OFF:], cf_dw_w, cf_ln_g, cf_ln_b, True)
    mix_lat = jnp.concatenate([dn_lat, cf_lat], axis=-1)

    if update_ctx:
        cf_ctx = conformer_heads(p_ctx[..., CF_OFF:], cf_dw_w, cf_ln_g, cf_ln_b, False)
        mix_ctx = jnp.concatenate([dn_ctx, cf_ctx], axis=-1)
        out = jnp.concatenate([mix_ctx, mix_lat], axis=1) @ w_out
        ctx = ctx + g1_c * out[:, :ctx_len]
        x = x + g1_l * out[:, ctx_len:]
        h2 = jnp.concatenate([modulate(rmsnorm(ctx, norm2_g), sh2_c, sc2_c),
                              modulate(rmsnorm(x, norm2_g), sh2_l, sc2_l)], axis=1)
        y = moe_ffn(h2.reshape(-1, d), router_w, router_bias, w_gate, w_up, w_down).reshape(h2.shape)
        ctx = ctx + g2_c * y[:, :ctx_len]
        x = x + g2_l * y[:, ctx_len:]
    else:
        x = x + g1_l * (mix_lat @ w_out)
        h2 = modulate(rmsnorm(x, norm2_g), sh2_l, sc2_l)
        x = x + g2_l * moe_ffn(h2.reshape(-1, d), router_w, router_bias,
                               w_gate, w_up, w_down).reshape(h2.shape)
    return x, ctx


def setup_inputs(seed: int = 0) -> dict:
    key = jax.random.key(seed)
    ks = jax.random.split(key, 24)
    f32 = jnp.float32
    nl, d = DEPTH, D_MODEL

    def nrm(k, shape, scale):
        return jax.random.normal(k, shape, f32) * scale

    dt = jnp.exp(jax.random.uniform(ks[11], (nl, 2, DN_HEADS), f32, math.log(1e-3), math.log(1e-1)))
    return {
        'x': nrm(ks[0], (BATCH, SEQ, d), 1.0),
        'c': nrm(ks[1], (BATCH, d), 1.0),
        'ctx': nrm(ks[2], (BATCH, CTX_LEN, d), 1.0),
        'c_ctx': nrm(ks[3], (d,), 1.0),
        'w_ada': nrm(ks[4], (nl, d, 6 * d), 0.5 * d ** -0.5),
        'b_ada': nrm(ks[5], (nl, 6 * d), 0.01),
        'norm1_g': 1.0 + nrm(ks[6], (nl, d), 0.02),
        'norm2_g': 1.0 + nrm(ks[7], (nl, d), 0.02),
        'w_in': nrm(ks[8], (nl, d, IN_COLS), d ** -0.5),
        'conv_qkv_w': nrm(ks[9], (nl, SHORT_CONV, QKV_W), SHORT_CONV ** -0.5),
        'a_log': jnp.log(jax.random.uniform(ks[10], (nl, 2, DN_HEADS), f32, 1.0, 16.0)),
        'dt_bias': dt + jnp.log(-jnp.expm1(-dt)),
        'dn_norm_g': 1.0 + nrm(ks[12], (nl, DN_DV), 0.02),
        'cf_dw_w': nrm(ks[13], (nl, CF_KERNEL, CF_W), CF_KERNEL ** -0.5),
        'cf_ln_g': 1.0 + nrm(ks[14], (nl, CF_W), 0.02),
        'cf_ln_b': nrm(ks[15], (nl, CF_W), 0.01),
        'w_out': nrm(ks[16], (nl, MIX_W, d), MIX_W ** -0.5),
        'router_w': nrm(ks[17], (d, N_EXPERTS), d ** -0.5),
        'router_bias': nrm(ks[18], (N_EXPERTS,), 0.01),
        'w_gate': nrm(ks[19], (nl, N_EXPERTS, d, D_FF), d ** -0.5),
        'w_up': nrm(ks[20], (nl, N_EXPERTS, d, D_FF), d ** -0.5),
        'w_down': nrm(ks[21], (nl, N_EXPERTS, D_FF, d), D_FF ** -0.5),
        'final_g': 1.0 + nrm(ks[22], (d,), 0.02),
    }


def reference(x, c, ctx, c_ctx, w_ada, b_ada, norm1_g, norm2_g, w_in, conv_qkv_w, a_log, dt_bias,
              dn_norm_g, cf_dw_w, cf_ln_g, cf_ln_b, w_out, router_w, router_bias,
              w_gate, w_up, w_down, final_g):
    c_act = jax.nn.silu(c)
    cc_act = jax.nn.silu(c_ctx)
    for i in range(DEPTH):
        mod_lat = c_act @ w_ada[i] + b_ada[i]
        mod_ctx = cc_act @ w_ada[i] + b_ada[i]
        x, ctx = hybrid_layer(x, ctx, mod_lat, mod_ctx, norm1_g[i], norm2_g[i], w_in[i],
                              conv_qkv_w[i], a_log[i], dt_bias[i], dn_norm_g[i], cf_dw_w[i],
                              cf_ln_g[i], cf_ln_b[i], w_out[i], router_w, router_bias,
                              w_gate[i], w_up[i], w_down[i], i < DEPTH - 1)
    return rmsnorm(x, final_g)
```

```python
import functools
import math

import jax
import jax.numpy as jnp
from jax import lax
from jax.experimental import pallas as pl
from jax.experimental.pallas import tpu as pltpu

F32 = jnp.float32
BF16 = jnp.bfloat16

D_MODEL = 4096
CTX_LEN = 256
GRID_W = 64
EPS = 1e-6
DN_HEADS = 16
DN_D = 128
DN_W = DN_HEADS * DN_D
SHORT_CONV = 5
CHUNK = 64
CF_W = 2048
CF_KERNEL = 31
CF_PAD = (CF_KERNEL - 1) // 2
N_EXPERTS = 16
N_GROUPS = 4
EXPERTS_PER_GROUP = 4
D_FF = 1024
MOE_BLOCK = 128
LANES = 128
MAIN_COLS = 4 * DN_W + 2 * CF_W
ROW_TILE = 256
MIB = 1024 * 1024


def _cparams(sem, vmem_mib):
    return pltpu.CompilerParams(dimension_semantics=sem, vmem_limit_bytes=vmem_mib * MIB)


def _sigmoid(x):
    return 1.0 / (1.0 + jnp.exp(-x))


def _ada_kernel(c_ref, w_ref, b_ref, o_ref):
    c = c_ref[...]
    act = (c * _sigmoid(c)).astype(BF16)
    o_ref[0] = jnp.dot(act, w_ref[0].astype(BF16), preferred_element_type=F32) + b_ref[0]


def ada_mods(c_all, w_ada, b_ada):
    depth, d, n = w_ada.shape
    tn = 512
    return pl.pallas_call(
        _ada_kernel,
        out_shape=jax.ShapeDtypeStruct((depth, 8, n), F32),
        grid=(depth, n // tn),
        in_specs=[pl.BlockSpec((8, d), lambda i, j: (0, 0)),
                  pl.BlockSpec((1, d, tn), lambda i, j: (i, 0, j)),
                  pl.BlockSpec((1, 1, tn), lambda i, j: (i, 0, j))],
        out_specs=pl.BlockSpec((1, 8, tn), lambda i, j: (i, 0, j)),
        compiler_params=_cparams(("parallel", "parallel"), 40),
        name="ada_mods",
    )(c_all, w_ada, b_ada.reshape(depth, 1, n))


def _mod_row(b, t):
    return jnp.where(t == 0, 4, b)


def _norm_mod_kernel(x_ref, g_ref, sh_ref, sc_ref, o_ref):
    x = x_ref[0]
    y = x * lax.rsqrt(jnp.mean(x * x, axis=-1, keepdims=True) + EPS) * g_ref[...]
    o_ref[0] = (y * (1.0 + sc_ref[0]) + sh_ref[0]).astype(o_ref.dtype)


def norm_modulate(xs, gain, mods, shift_col, out_dtype):
    b, l, d = xs.shape
    return pl.pallas_call(
        _norm_mod_kernel,
        out_shape=jax.ShapeDtypeStruct((b, l, d), out_dtype),
        grid=(b, l // ROW_TILE),
        in_specs=[pl.BlockSpec((1, ROW_TILE, d), lambda i, t: (i, t, 0)),
                  pl.BlockSpec((1, d), lambda i, t: (0, 0)),
                  pl.BlockSpec((1, 1, d), lambda i, t: (_mod_row(i, t), 0, shift_col)),
                  pl.BlockSpec((1, 1, d), lambda i, t: (_mod_row(i, t), 0, shift_col + 1))],
        out_specs=pl.BlockSpec((1, ROW_TILE, d), lambda i, t: (i, t, 0)),
        compiler_params=_cparams(("parallel", "parallel"), 40),
        name="norm_modulate",
    )(xs, gain.reshape(1, d), mods, mods)


def _mm_kernel(a_ref, w_ref, o_ref):
    o_ref[...] = jnp.dot(a_ref[...], w_ref[...], preferred_element_type=F32).astype(o_ref.dtype)


def matmul(a, w, tm, tn, out_dtype=F32):
    m, k = a.shape
    n = w.shape[1]
    return pl.pallas_call(
        _mm_kernel,
        out_shape=jax.ShapeDtypeStruct((m, n), out_dtype),
        grid=(m // tm, n // tn),
        in_specs=[pl.BlockSpec((tm, k), lambda i, j: (i, 0)),
                  pl.BlockSpec((k, tn), lambda i, j: (0, j))],
        out_specs=pl.BlockSpec((tm, tn), lambda i, j: (i, j)),
        compiler_params=_cparams(("parallel", "parallel"), 48),
        name="in_proj",
    )(a, w)


def _dprep_kernel(h_ref, w_ref, alog_ref, dtb_ref, o_ref):
    raw = jnp.dot(h_ref[...], w_ref[...], preferred_element_type=F32)
    beta = _sigmoid(raw)
    g = -jnp.exp(alog_ref[...]) * jax.nn.softplus(raw + dtb_ref[...])
    r = lax.broadcasted_iota(jnp.int32, (CHUNK, CHUNK), 0)
    c = lax.broadcasted_iota(jnp.int32, (CHUNK, CHUNK), 1)
    tri_lo = (c <= r).astype(F32)
    tri_up = (c >= r).astype(F32)
    lane = lax.broadcasted_iota(jnp.int32, (CHUNK, LANES), 1)
    for ch in range(ROW_TILE // CHUNK):
        sl = slice(ch * CHUNK, (ch + 1) * CHUNK)
        gch = g[sl]
        cum_f = jnp.dot(tri_lo, gch, preferred_element_type=F32, precision=lax.Precision.HIGHEST)
        cum_b = jnp.dot(tri_up, gch, preferred_element_type=F32, precision=lax.Precision.HIGHEST)
        gc = jnp.where(lane < 3 * DN_HEADS, cum_f, cum_b)
        o_ref[sl, :] = jnp.where(lane < 2 * DN_HEADS, beta[sl], gc)


def delta_gates(h2d, w_aux, a_log, dt_bias):
    m, d = h2d.shape
    lanes_vec = lambda v: jnp.zeros((1, LANES), F32).at[0, 2 * DN_HEADS:4 * DN_HEADS].set(v.reshape(-1))
    return pl.pallas_call(
        _dprep_kernel,
        out_shape=jax.ShapeDtypeStruct((m, LANES), F32),
        grid=(m // ROW_TILE,),
        in_specs=[pl.BlockSpec((ROW_TILE, d), lambda i: (i, 0)),
                  pl.BlockSpec((d, LANES), lambda i: (0, 0)),
                  pl.BlockSpec((1, LANES), lambda i: (0, 0)),
                  pl.BlockSpec((1, LANES), lambda i: (0, 0))],
        out_specs=pl.BlockSpec((ROW_TILE, LANES), lambda i: (i, 0)),
        compiler_params=_cparams(("parallel",), 32),
        name="delta_gates",
    )(h2d, w_aux, lanes_vec(a_log), lanes_vec(dt_bias))


def _split_bf16(x):
    hi = x.astype(BF16)
    lo = (x - hi.astype(F32)).astype(BF16)
    return hi, lo


def _mm3(x, y):
    xh, xl = _split_bf16(x)
    yh, yl = _split_bf16(y)
    dot = functools.partial(jnp.dot, preferred_element_type=F32)
    return dot(xh, yh) + (dot(xh, yl) + dot(xl, yh))


def _block_diag2(y):
    yy = jnp.concatenate([y, y], axis=0)
    r = lax.broadcasted_iota(jnp.int32, (2 * CHUNK, LANES), 0)
    c = lax.broadcasted_iota(jnp.int32, (2 * CHUNK, LANES), 1)
    return jnp.where((r // CHUNK) == (c // CHUNK), yy, 0.0)


def _unit_tri_inverse2(a2, row, col):
    eye2 = (row == col).astype(F32)
    same = lambda n: (row // n) == (col // n)
    a8 = jnp.where(same(8), a2, 0.0)
    p = _mm3(a8, _block_diag2(a8))
    t = eye2 - a8
    t = t + _mm3(t, _block_diag2(p))
    p = _mm3(p, _block_diag2(p))
    t = t + _mm3(t, _block_diag2(p))
    for n in (16, 32, 64):
        cpart = jnp.where(same(n) & jnp.logical_not(same(n // 2)), a2, 0.0)
        t = t - _mm3(_mm3(t, _block_diag2(cpart)), _block_diag2(t))
    return t


def _delta_kernel(q_ref, k_ref, v_ref, z_ref, aux_ref, wq_ref, wk_ref, wv_ref, ng_ref, o_ref,
                  qs, ks, vs, bcs, uw_s, qkf_s, qkb_s, of_s, ob_s, *, seq_len):
    l = seq_len
    n_chunks = l // CHUNK
    ctx_chunks = CTX_LEN // CHUNK
    head = pl.program_id(1)
    trow = lax.broadcasted_iota(jnp.int32, (l, 1), 0)

    def conv_silu(x_ref, w_ref):
        x = x_ref[0]
        w = w_ref[...]
        pad = (SHORT_CONV - 1) // 2
        acc = x * w[pad:pad + 1]
        for j in range(SHORT_CONV):
            s = j - pad
            if s == 0:
                continue
            shifted = pltpu.roll(x, (-s) % l, 0)
            ts = trow + s
            seg_lo = jnp.where(trow < CTX_LEN, 0, CTX_LEN)
            seg_hi = jnp.where(trow < CTX_LEN, CTX_LEN, l)
            valid = (ts >= seg_lo) & (ts < seg_hi)
            acc = acc + jnp.where(valid, shifted, 0.0) * w[j:j + 1]
        return acc * _sigmoid(acc)

    def l2norm(x):
        return x * lax.rsqrt(jnp.sum(x * x, axis=-1, keepdims=True) + EPS)

    qs[...] = l2norm(conv_silu(q_ref, wq_ref)) * (DN_D ** -0.5)
    ks[...] = l2norm(conv_silu(k_ref, wk_ref))
    vs[...] = conv_silu(v_ref, wv_ref)

    aux = aux_ref[0]
    alane = lax.broadcasted_iota(jnp.int32, (l, LANES), 1)
    for i in range(4):
        colv = jnp.sum(jnp.where(alane == head + i * DN_HEADS, aux, 0.0), axis=-1, keepdims=True)
        bcs[i] = jnp.broadcast_to(colv, (l, LANES))

    row = lax.broadcasted_iota(jnp.int32, (CHUNK, LANES), 0)
    lane = lax.broadcasted_iota(jnp.int32, (CHUNK, LANES), 1)
    fwd = lane < CHUNK
    col = jnp.where(fwd, lane, lane - CHUNK)
    sgn = jnp.where(fwd, 1, -1)
    incl = (row - col) * sgn >= 0
    strict = (row - col) * sgn > 0
    dot = functools.partial(jnp.dot, preferred_element_type=F32)

    def prep(c, carry):
        st = pl.multiple_of(c * CHUNK, CHUNK)
        sl = pl.ds(st, CHUNK)
        q, k, v = qs[sl, :], ks[sl, :], vs[sl, :]
        bf, bb, gcf, gcb = bcs[0, sl, :], bcs[1, sl, :], bcs[2, sl, :], bcs[3, sl, :]
        b2 = jnp.where(fwd, bf, bb)
        c2 = jnp.where(fwd, gcf, gcb)
        r2 = jnp.sum(jnp.where(row == col, c2, 0.0), axis=0, keepdims=True)
        decay = jnp.where(incl, jnp.exp(jnp.where(incl, c2 - r2, 0.0)), 0.0)
        kk = jnp.concatenate([k, k], axis=0).astype(BF16)
        nt = (((1,), (1,)), ((), ()))
        gram = lax.dot_general(k.astype(BF16), kk, nt, preferred_element_type=F32)
        qk = lax.dot_general(q.astype(BF16), kk, nt, preferred_element_type=F32) * decay
        a2 = jnp.where(strict, b2 * gram * decay, 0.0)
        t2 = _unit_tri_inverse2(a2, row, col)
        rhs = jnp.concatenate(
            [jnp.concatenate([v * bf, k * (bf * jnp.exp(gcf))], axis=1),
             jnp.concatenate([v * bb, k * (bb * jnp.exp(gcb))], axis=1)], axis=0)
        uw_s[c] = dot(_block_diag2(t2).astype(BF16), rhs.astype(BF16))
        qkf_s[c] = jnp.where(fwd, qk, 0.0).astype(BF16)
        qkb_s[c] = jnp.where(fwd, 0.0, qk).astype(BF16)
        return carry

    lax.fori_loop(0, n_chunks, prep, 0)

    def one_dir(c, s, is_fwd):
        st = pl.multiple_of(c * CHUNK, CHUNK)
        sl = pl.ds(st, CHUNK)
        q, k = qs[sl, :], ks[sl, :]
        if is_fwd:
            gc = bcs[2, sl, :]
            g_last = gc[CHUNK - 1:CHUNK, :]
            uw = uw_s[c, 0:CHUNK, :]
            qk = qkf_s[c]
        else:
            gc = bcs[3, sl, :]
            g_last = gc[0:1, :]
            uw = uw_s[c, CHUNK:2 * CHUNK, :]
            qk = qkb_s[c]
        u, w = uw[:, :DN_D], uw[:, DN_D:]
        qd = q * jnp.exp(gc)
        kd = k * jnp.exp(g_last - gc)
        ws = dot(jnp.concatenate([w, qd], axis=0).astype(BF16), s.astype(BF16))
        v_new = u - ws[:CHUNK]
        vb = v_new.astype(BF16)
        o = ws[CHUNK:] + dot(qk, jnp.concatenate([vb, vb], axis=0))
        tn = (((0,), (0,)), ((), ()))
        s = s * jnp.exp(g_last) + lax.dot_general(kd.astype(BF16), vb, tn, preferred_element_type=F32)
        return o, s, sl

    def scan(t, carry):
        s_f, s_b = carry
        o_f, s_f, sl_f = one_dir(t, s_f, True)
        of_s[sl_f, :] = o_f
        cb = jnp.where(t < ctx_chunks, ctx_chunks - 1 - t, n_chunks + ctx_chunks - 1 - t)
        o_b, s_b, sl_b = one_dir(cb, s_b, False)
        ob_s[sl_b, :] = o_b
        return s_f, s_b

    zero = jnp.zeros((DN_D, DN_D), F32)
    lax.fori_loop(0, n_chunks, scan, (zero, zero))

    o = of_s[...] + ob_s[...]
    o = o * lax.rsqrt(jnp.mean(o * o, axis=-1, keepdims=True) + EPS) * ng_ref[...]
    z = z_ref[0]
    o_ref[0] = (o * (z * _sigmoid(z))).astype(o_ref.dtype)


def deltanet_heads(p_main, aux, conv_w, norm_g):
    b, l, _ = p_main.shape
    n_chunks = l // CHUNK
    blk = lambda off: pl.BlockSpec((1, l, DN_D), lambda i, h, off=off: (i, 0, off + h))
    wblk = lambda off: pl.BlockSpec((SHORT_CONV, DN_D), lambda i, h, off=off: (0, off + h))
    seq = pltpu.VMEM((l, DN_D), F32)
    return pl.pallas_call(
        functools.partial(_delta_kernel, seq_len=l),
        out_shape=jax.ShapeDtypeStruct((b, l, DN_W), BF16),
        grid=(b, DN_HEADS),
        in_specs=[blk(0), blk(DN_HEADS), blk(2 * DN_HEADS), blk(3 * DN_HEADS),
                  pl.BlockSpec((1, l, LANES), lambda i, h: (i, 0, 0)),
                  wblk(0), wblk(DN_HEADS), wblk(2 * DN_HEADS),
                  pl.BlockSpec((1, DN_D), lambda i, h: (0, 0))],
        out_specs=pl.BlockSpec((1, l, DN_D), lambda i, h: (i, 0, h)),
        scratch_shapes=[seq, seq, seq,
                        pltpu.VMEM((4, l, LANES), F32),
                        pltpu.VMEM((n_chunks, 2 * CHUNK, 2 * DN_D), F32),
                        pltpu.VMEM((n_chunks, CHUNK, LANES), BF16),
                        pltpu.VMEM((n_chunks, CHUNK, LANES), BF16),
                        seq, seq],
        compiler_params=_cparams(("parallel", "parallel"), 48),
        name="deltanet_heads",
    )(p_main, p_main, p_main, p_main, aux, conv_w, conv_w, conv_w, norm_g.reshape(1, DN_D))


def _cf_kernel(val_ref, gate_ref, w_ref, g_ref, b_ref, o_ref, cbuf, lbuf, *, seq_len, vertical):
    lat = seq_len - CTX_LEN
    gate = gate_ref[0]
    u = val_ref[0] * _sigmoid(gate)
    w = w_ref[...]
    gain, bias = g_ref[...], b_ref[...]
    hpad = 16
    vpad = CF_PAD * GRID_W

    def finish(acc, row0):
        xc = acc - jnp.mean(acc, axis=-1, keepdims=True)
        y = xc * lax.rsqrt(jnp.mean(xc * xc, axis=-1, keepdims=True) + EPS) * gain + bias
        o_ref[0, pl.ds(row0, GRID_W), :] = (y * _sigmoid(y)).astype(o_ref.dtype)

    def window_conv(win, masked):
        r = lax.broadcasted_iota(jnp.int32, (GRID_W, 1), 0)
        acc = jnp.zeros((GRID_W, LANES), F32)
        for k in range(CF_KERNEL):
            off = hpad - CF_PAD + k
            piece = win[off:off + GRID_W]
            if masked:
                src = r + (k - CF_PAD)
                piece = jnp.where((src >= 0) & (src < GRID_W), piece, 0.0)
            acc = acc + piece * w[k:k + 1]
        return acc

    zeros_h = jnp.zeros((hpad, LANES), F32)
    cbuf[0:hpad, :] = zeros_h
    cbuf[hpad + CTX_LEN:, :] = zeros_h
    cbuf[hpad:hpad + CTX_LEN, :] = u[:CTX_LEN]

    def ctx_tile(g, carry):
        base = pl.multiple_of(g * GRID_W, GRID_W)
        finish(window_conv(cbuf[pl.ds(base, GRID_W + 2 * hpad), :], False), base)
        return carry

    lax.fori_loop(0, CTX_LEN // GRID_W, ctx_tile, 0)

    if vertical:
        zeros_v = jnp.zeros((vpad, LANES), F32)
        lbuf[0:vpad, :] = zeros_v
        lbuf[vpad + lat:, :] = zeros_v
        lbuf[vpad:vpad + lat, :] = u[CTX_LEN:]

        def lat_tile(g, carry):
            acc = jnp.zeros((GRID_W, LANES), F32)
            for k in range(CF_KERNEL):
                base = pl.multiple_of((g + k) * GRID_W, GRID_W)
                acc = acc + lbuf[pl.ds(base, GRID_W), :] * w[k:k + 1]
            finish(acc, pl.multiple_of(CTX_LEN + g * GRID_W, GRID_W))
            return carry
    else:
        lbuf[0:hpad, :] = zeros_h
        lbuf[hpad + lat:, :] = zeros_h
        lbuf[hpad:hpad + lat, :] = u[CTX_LEN:]

        def lat_tile(g, carry):
            base = pl.multiple_of(g * GRID_W, GRID_W)
            finish(window_conv(lbuf[pl.ds(base, GRID_W + 2 * hpad), :], True),
                   pl.multiple_of(CTX_LEN + base, GRID_W))
            return carry

    lax.fori_loop(0, lat // GRID_W, lat_tile, 0)


def conformer_heads(p_main, dw_w, ln_g, ln_b, vertical):
    b, l, _ = p_main.shape
    half = CF_W // 2
    lat = l - CTX_LEN
    first = half // LANES if vertical else 0
    val0 = 4 * DN_W // LANES + first
    gate0 = (4 * DN_W + CF_W) // LANES + first
    lrows = lat + (2 * CF_PAD * GRID_W if vertical else 32)
    chan = lambda rows: pl.BlockSpec((rows, LANES), lambda i, j: (0, first + j))
    return pl.pallas_call(
        functools.partial(_cf_kernel, seq_len=l, vertical=vertical),
        out_shape=jax.ShapeDtypeStruct((b, l, half), BF16),
        grid=(b, half // LANES),
        in_specs=[pl.BlockSpec((1, l, LANES), lambda i, j: (i, 0, val0 + j)),
                  pl.BlockSpec((1, l, LANES), lambda i, j: (i, 0, gate0 + j)),
                  chan(CF_KERNEL), chan(1), chan(1)],
        out_specs=pl.BlockSpec((1, l, LANES), lambda i, j: (i, 0, j)),
        scratch_shapes=[pltpu.VMEM((CTX_LEN + 32, LANES), F32), pltpu.VMEM((lrows, LANES), F32)],
        compiler_params=_cparams(("parallel", "parallel"), 32),
        name="conformer_v" if vertical else "conformer_h",
    )(p_main, p_main, dw_w, ln_g.reshape(1, CF_W), ln_b.reshape(1, CF_W))


def _out_proj_kernel(dn_ref, ch_ref, cv_ref, w0_ref, w1_ref, w2_ref, x_ref, gl_ref, gc_ref, o_ref, *, tm):
    dot = functools.partial(jnp.dot, preferred_element_type=F32)
    acc = dot(dn_ref[0], w0_ref[...]) + dot(ch_ref[0], w1_ref[...]) + dot(cv_ref[0], w2_ref[...])
    row = pl.program_id(1) * tm + lax.broadcasted_iota(jnp.int32, (tm, 1), 0)
    gate = jnp.where(row < CTX_LEN, gc_ref[0], gl_ref[0])
    o_ref[0] = x_ref[0] + gate * acc


def out_proj_residual(dn, cfh, cfv, w_out, xs, mods, gate_col):
    b, l, d = xs.shape
    tm, tn = l // 2, 512
    k0, k1 = dn.shape[-1], cfh.shape[-1]
    act = lambda kk: pl.BlockSpec((1, tm, kk), lambda i, t, j: (i, t, 0))
    gate = lambda r: pl.BlockSpec((1, 1, tn), lambda i, t, j, r=r: (i if r is None else r, 0, gate_col * (d // tn) + j))
    return pl.pallas_call(
        functools.partial(_out_proj_kernel, tm=tm),
        out_shape=jax.ShapeDtypeStruct((b, l, d), F32),
        grid=(b, l // tm, d // tn),
        in_specs=[act(k0), act(k1), act(k1),
                  pl.BlockSpec((k0, tn), lambda i, t, j: (0, j)),
                  pl.BlockSpec((k1, tn), lambda i, t, j: (k0 // k1, j)),
                  pl.BlockSpec((k1, tn), lambda i, t, j: (k0 // k1 + 1, j)),
                  pl.BlockSpec((1, tm, tn), lambda i, t, j: (i, t, j)),
                  gate(None), gate(4)],
        out_specs=pl.BlockSpec((1, tm, tn), lambda i, t, j: (i, t, j)),
        compiler_params=_cparams(("parallel", "parallel", "parallel"), 48),
        name="out_proj",
    )(dn, cfh, cfv, w_out, w_out, w_out, xs, mods, mods)


def _norm_router_kernel(x_ref, g_ref, sh_ref, sc_ref, rw_ref, rb_ref, h_ref, idx_ref, wt_ref):
    x = x_ref[0]
    y = x * lax.rsqrt(jnp.mean(x * x, axis=-1, keepdims=True) + EPS) * g_ref[...]
    h = y * (1.0 + sc_ref[0]) + sh_ref[0]
    h_ref[0] = h
    logits = lax.dot_general(rw_ref[...], h, (((1,), (1,)), ((), ())),
                             preferred_element_type=F32, precision=lax.Precision.HIGHEST)
    scores = _sigmoid(logits)
    biased = scores + rb_ref[...]
    rows = [biased[e:e + 1] for e in range(N_EXPERTS)]
    srows = [scores[e:e + 1] for e in range(N_EXPERTS)]

    def pair_max(v):
        best = v[0] + v[1]
        for a in range(len(v)):
            for bb in range(a + 1, len(v)):
                if (a, bb) != (0, 1):
                    best = jnp.maximum(best, v[a] + v[bb])
        return best

    gscore = [pair_max(rows[g * 4:(g + 1) * 4]) for g in range(N_GROUPS)]
    best_g = jnp.zeros_like(gscore[0], dtype=jnp.int32)
    best_v = gscore[0]
    for g in range(1, N_GROUPS):
        better = gscore[g] > best_v
        best_g = jnp.where(better, g, best_g)
        best_v = jnp.where(better, gscore[g], best_v)

    def pick(vals):
        out = []
        for j in range(EXPERTS_PER_GROUP):
            m = vals[j]
            for g in range(1, N_GROUPS):
                m = jnp.where(best_g == g, vals[g * 4 + j], m)
            out.append(m)
        return out

    m, s = pick(rows), pick(srows)
    i1 = jnp.zeros_like(best_g)
    v1 = m[0]
    for j in range(1, 4):
        better = m[j] > v1
        i1 = jnp.where(better, j, i1)
        v1 = jnp.where(better, m[j], v1)
    i2 = jnp.full_like(best_g, -1)
    v2 = jnp.full_like(v1, -jnp.inf)
    for j in range(4):
        better = (i1 != j) & ((m[j] > v2) | (i2 < 0))
        i2 = jnp.where(better, j, i2)
        v2 = jnp.where(better, m[j], v2)

    def sel(vals, i):
        out = vals[0]
        for j in range(1, 4):
            out = jnp.where(i == j, vals[j], out)
        return out

    w1, w2 = sel(s, i1), sel(s, i2)
    tot = w1 + w2
    idx_ref[0:1, :] = best_g * EXPERTS_PER_GROUP + i1
    idx_ref[1:2, :] = best_g * EXPERTS_PER_GROUP + i2
    wt_ref[0:1, :] = w1 / tot
    wt_ref[1:2, :] = w2 / tot


def norm_router(xs, gain, mods, shift_col, router_w, router_bias):
    b, l, d = xs.shape
    nt = l // ROW_TILE
    tok = lambda i, t: (0, i * nt + t)
    return pl.pallas_call(
        _norm_router_kernel,
        out_shape=(jax.ShapeDtypeStruct((b, l, d), F32),
                   jax.ShapeDtypeStruct((2, b * l), jnp.int32),
                   jax.ShapeDtypeStruct((2, b * l), F32)),
        grid=(b, nt),
        in_specs=[pl.BlockSpec((1, ROW_TILE, d), lambda i, t: (i, t, 0)),
                  pl.BlockSpec((1, d), lambda i, t: (0, 0)),
                  pl.BlockSpec((1, 1, d), lambda i, t: (_mod_row(i, t), 0, shift_col)),
                  pl.BlockSpec((1, 1, d), lambda i, t: (_mod_row(i, t), 0, shift_col + 1)),
                  pl.BlockSpec((N_EXPERTS, d), lambda i, t: (0, 0)),
                  pl.BlockSpec((N_EXPERTS, 1), lambda i, t: (0, 0))],
        out_specs=(pl.BlockSpec((1, ROW_TILE, d), lambda i, t: (i, t, 0)),
                   pl.BlockSpec((2, ROW_TILE), tok),
                   pl.BlockSpec((2, ROW_TILE), tok)),
        compiler_params=_cparams(("parallel", "parallel"), 48),
        name="norm_router",
    )(xs, gain.reshape(1, d), mods, mods, router_w.T, router_bias.reshape(N_EXPERTS, 1))


def _moe_kernel(bexp_ref, bval_ref, stok_ref, sdst_ref, h_hbm, wg_ref, wu_ref, wd_ref, y_hbm,
                xbuf, ybuf, gsem, ssem):
    i = pl.program_id(0)
    base = i * MOE_BLOCK

    @pl.when(bval_ref[i] > 0)
    def _():
        def gather(r):
            return pltpu.make_async_copy(h_hbm.at[pl.ds(stok_ref[base + r], 1)], xbuf.at[pl.ds(r, 1)], gsem)

        def scatter(r):
            return pltpu.make_async_copy(ybuf.at[pl.ds(r, 1)], y_hbm.at[pl.ds(sdst_ref[base + r], 1)], ssem)

        def issue(r, c):
            gather(r).start()
            return c

        def wait_in(r, c):
            gather(r).wait()
            return c

        lax.fori_loop(0, MOE_BLOCK, issue, 0)
        lax.fori_loop(0, MOE_BLOCK, wait_in, 0)
        x = xbuf[...].astype(BF16)
        dot = functools.partial(jnp.dot, preferred_element_type=F32)
        g = dot(x, wg_ref[0])
        u = dot(x, wu_ref[0])
        act = (g * _sigmoid(g) * u).astype(BF16)
        ybuf[...] = dot(act, wd_ref[0])

        def send(r, c):
            @pl.when(sdst_ref[base + r] >= 0)
            def _():
                scatter(r).start()
            return c

        def wait_out(r, c):
            @pl.when(sdst_ref[base + r] >= 0)
            def _():
                scatter(r).wait()
            return c

        lax.fori_loop(0, MOE_BLOCK, send, 0)
        lax.fori_loop(0, MOE_BLOCK, wait_out, 0)


def moe_experts(h_tok, w_gate, w_up, w_down, block_expert, block_valid, slot_tok, slot_dst):
    t, d = h_tok.shape
    n_blocks = block_expert.shape[0]
    f = w_gate.shape[-1]
    grid_spec = pltpu.PrefetchScalarGridSpec(
        num_scalar_prefetch=4,
        grid=(n_blocks,),
        in_specs=[pl.BlockSpec(memory_space=pl.ANY),
                  pl.BlockSpec((1, d, f), lambda i, be, bv, st, sd: (be[i], 0, 0), pipeline_mode=pl.Buffered(1)),
                  pl.BlockSpec((1, d, f), lambda i, be, bv, st, sd: (be[i], 0, 0), pipeline_mode=pl.Buffered(1)),
                  pl.BlockSpec((1, f, d), lambda i, be, bv, st, sd: (be[i], 0, 0), pipeline_mode=pl.Buffered(1))],
        out_specs=pl.BlockSpec(memory_space=pl.ANY),
        scratch_shapes=[pltpu.VMEM((MOE_BLOCK, d), F32), pltpu.VMEM((MOE_BLOCK, d), F32),
                        pltpu.SemaphoreType.DMA, pltpu.SemaphoreType.DMA],
    )
    return pl.pallas_call(
        _moe_kernel,
        out_shape=jax.ShapeDtypeStruct((2 * t, d), F32),
        grid_spec=grid_spec,
        compiler_params=_cparams(("arbitrary",), 56),
        name="moe_experts",
    )(block_expert, block_valid, slot_tok, slot_dst, h_tok, w_gate, w_up, w_down)


def route_slots(idx):
    t = idx.shape[1]
    n = 2 * t
    flat_e = idx.reshape(-1)
    onehot = (flat_e[:, None] == jnp.arange(N_EXPERTS)[None, :]).astype(jnp.int32)
    csum = jnp.cumsum(onehot, axis=0)
    rank = jnp.take_along_axis(csum, flat_e[:, None], axis=1)[:, 0] - 1
    counts = csum[-1]
    padded = (counts + MOE_BLOCK - 1) // MOE_BLOCK * MOE_BLOCK
    pad_end = jnp.cumsum(padded)
    dest = (pad_end - padded)[flat_e] + rank
    n_blocks = n // MOE_BLOCK + N_EXPERTS
    slot_src = jnp.full((n_blocks * MOE_BLOCK,), -1, jnp.int32).at[dest].set(jnp.arange(n, dtype=jnp.int32))
    starts = jnp.arange(n_blocks, dtype=jnp.int32) * MOE_BLOCK
    block_valid = (starts < pad_end[-1]).astype(jnp.int32)
    last_start = jnp.maximum(pad_end[-1] - MOE_BLOCK, 0)
    block_expert = jnp.minimum(jnp.searchsorted(pad_end, jnp.minimum(starts, last_start), side='right'),
                               N_EXPERTS - 1).astype(jnp.int32)
    slot_tok = jnp.where(slot_src >= 0, slot_src % t, 0).astype(jnp.int32)
    return block_expert, block_valid, slot_tok, slot_src


def _combine_kernel(x_ref, y0_ref, y1_ref, w_ref, g_ref, fg_ref, o_ref, *, final):
    w = w_ref[...]
    x = x_ref[0] + g_ref[0] * (y0_ref[...] * w[:, 0:1] + y1_ref[...] * w[:, 1:2])
    if final:
        x = x * lax.rsqrt(jnp.mean(x * x, axis=-1, keepdims=True) + EPS) * fg_ref[...]
    o_ref[0] = x


def moe_combine(xs, yk, wts, mods, gate_col, final_g, final):
    b, l, d = xs.shape
    nt = l // ROW_TILE
    t_all = b * l
    t0 = 1 if final else 0
    n_out = nt - t0
    return pl.pallas_call(
        functools.partial(_combine_kernel, final=final),
        out_shape=jax.ShapeDtypeStruct((b, n_out * ROW_TILE, d), F32),
        grid=(b, n_out),
        in_specs=[pl.BlockSpec((1, ROW_TILE, d), lambda i, t: (i, t + t0, 0)),
                  pl.BlockSpec((ROW_TILE, d), lambda i, t: (i * nt + t + t0, 0)),
                  pl.BlockSpec((ROW_TILE, d), lambda i, t: (t_all // ROW_TILE + i * nt + t + t0, 0)),
                  pl.BlockSpec((ROW_TILE, 2), lambda i, t: (i * nt + t + t0, 0)),
                  pl.BlockSpec((1, 1, d), lambda i, t: (_mod_row(i, t + t0), 0, gate_col)),
                  pl.BlockSpec((1, d), lambda i, t: (0, 0))],
        out_specs=pl.BlockSpec((1, ROW_TILE, d), lambda i, t: (i, t, 0)),
        compiler_params=_cparams(("parallel", "parallel"), 48),
        name="moe_combine",
    )(xs, yk, yk, wts, mods, final_g.reshape(1, d))


def _layer(xs, mods, norm1_g, norm2_g, w_in, conv_qkv_w, a_log, dt_bias, dn_norm_g, cf_dw_w,
           cf_ln_g, cf_ln_b, w_out, router_w, router_bias, w_gate, w_up, w_down, final_g, final):
    b, l, d = xs.shape
    qkvz = 4 * DN_W
    gates0 = qkvz + 4 * DN_HEADS
    w_main = jnp.concatenate([w_in[:, :qkvz], w_in[:, gates0:]], axis=1).astype(BF16)
    w_aux = jnp.pad(w_in[:, qkvz:gates0], ((0, 0), (0, LANES - 4 * DN_HEADS))).astype(BF16)

    h = norm_modulate(xs, norm1_g, mods, 0, BF16).reshape(b * l, d)
    p_main = matmul(h, w_main, 1024, 512).reshape(b, l, MAIN_COLS)
    aux = delta_gates(h, w_aux, a_log, dt_bias).reshape(b, l, LANES)
    dn = deltanet_heads(p_main, aux, conv_qkv_w, dn_norm_g)
    cfh = conformer_heads(p_main, cf_dw_w, cf_ln_g, cf_ln_b, False)
    cfv = conformer_heads(p_main, cf_dw_w, cf_ln_g, cf_ln_b, True)
    xs = out_proj_residual(dn, cfh, cfv, w_out.astype(BF16), xs, mods, 2)

    h2, idx, wts = norm_router(xs, norm2_g, mods, 3, router_w, router_bias)
    block_expert, block_valid, slot_tok, slot_dst = route_slots(idx)
    yk = moe_experts(h2.reshape(b * l, d), w_gate.astype(BF16), w_up.astype(BF16), w_down.astype(BF16),
                     block_expert, block_valid, slot_tok, slot_dst)
    return moe_combine(xs, yk, wts.T, mods, 5, final_g, final)


def kernel(x, c, ctx, c_ctx, w_ada, b_ada, norm1_g, norm2_g, w_in, conv_qkv_w, a_log, dt_bias,
           dn_norm_g, cf_dw_w, cf_ln_g, cf_ln_b, w_out, router_w, router_bias, w_gate, w_up,
           w_down, final_g):
    depth = w_ada.shape[0]
    bsz = x.shape[0]
    xs = jnp.concatenate([ctx, x], axis=1)
    c_all = jnp.concatenate([c, c_ctx[None, :], jnp.zeros((8 - bsz - 1, c.shape[1]), F32)], axis=0)
    mods = ada_mods(c_all, w_ada, b_ada)
    for i in range(depth):
        mods_i = mods[i].reshape(8, 1, -1)
        xs = _layer(xs, mods_i, norm1_g[i], norm2_g[i], w_in[i], conv_qkv_w[i], a_log[i], dt_bias[i],
                    dn_norm_g[i], cf_dw_w[i], cf_ln_g[i], cf_ln_b[i], w_out[i], router_w, router_bias,
                    w_gate[i], w_up[i], w_down[i], final_g, i == depth - 1)
    return xs
```

```python
import functools
import math

import jax
import jax.numpy as jnp
from jax import lax
from jax.experimental import pallas as pl
from jax.experimental.pallas import tpu as pltpu

F32 = jnp.float32
BF16 = jnp.bfloat16

D_MODEL = 4096
CTX_LEN = 256
GRID_W = 64
EPS = 1e-6
DN_HEADS = 16
DN_D = 128
DN_W = DN_HEADS * DN_D
SHORT_CONV = 5
CHUNK = 64
CF_W = 2048
CF_KERNEL = 31
CF_PAD = (CF_KERNEL - 1) // 2
N_EXPERTS = 16
N_GROUPS = 4
EXPERTS_PER_GROUP = 4
D_FF = 1024
MOE_BLOCK = 128
MOE_DUMP_ROWS = 2 * MOE_BLOCK
LANES = 128
ROW_TILE = 256
PREP_UNROLL = 12
MIB = 1024 * 1024


def _cparams(sem, vmem_mib):
    return pltpu.CompilerParams(dimension_semantics=sem, vmem_limit_bytes=vmem_mib * MIB)


def _sigmoid(x):
    return 1.0 / (1.0 + jnp.exp(-x))


def _ada_kernel(c_ref, w_ref, b_ref, o_ref):
    c = c_ref[...]
    act = (c * _sigmoid(c)).astype(BF16)
    o_ref[0] = jnp.dot(act, w_ref[0].astype(BF16), preferred_element_type=F32) + b_ref[0]


def ada_mods(c_all, w_ada, b_ada):
    depth, d, n = w_ada.shape
    tn = 512
    return pl.pallas_call(
        _ada_kernel,
        out_shape=jax.ShapeDtypeStruct((depth, 8, n), F32),
        grid=(depth, n // tn),
        in_specs=[pl.BlockSpec((8, d), lambda i, j: (0, 0)),
                  pl.BlockSpec((1, d, tn), lambda i, j: (i, 0, j)),
                  pl.BlockSpec((1, 1, tn), lambda i, j: (i, 0, j))],
        out_specs=pl.BlockSpec((1, 8, tn), lambda i, j: (i, 0, j)),
        compiler_params=_cparams(("parallel", "parallel"), 40),
        name="ada_mods",
    )(c_all, w_ada, b_ada.reshape(depth, 1, n))


def _mod_row(b, t):
    return jnp.where(t == 0, 4, b)


def _norm_mod_kernel(x_ref, g_ref, sh_ref, sc_ref, o_ref):
    x = x_ref[0]
    y = x * lax.rsqrt(jnp.mean(x * x, axis=-1, keepdims=True) + EPS) * g_ref[...]
    o_ref[0] = (y * (1.0 + sc_ref[0]) + sh_ref[0]).astype(o_ref.dtype)


def norm_modulate(xs, gain, mods, shift_col, out_dtype):
    b, l, d = xs.shape
    return pl.pallas_call(
        _norm_mod_kernel,
        out_shape=jax.ShapeDtypeStruct((b, l, d), out_dtype),
        grid=(b, l // ROW_TILE),
        in_specs=[pl.BlockSpec((1, ROW_TILE, d), lambda i, t: (i, t, 0)),
                  pl.BlockSpec((1, d), lambda i, t: (0, 0)),
                  pl.BlockSpec((1, 1, d), lambda i, t: (_mod_row(i, t), 0, shift_col)),
                  pl.BlockSpec((1, 1, d), lambda i, t: (_mod_row(i, t), 0, shift_col + 1))],
        out_specs=pl.BlockSpec((1, ROW_TILE, d), lambda i, t: (i, t, 0)),
        compiler_params=_cparams(("parallel", "parallel"), 40),
        name="norm_modulate",
    )(xs, gain.reshape(1, d), mods, mods)


def _mm_kernel(a_ref, w_ref, o_ref):
    o_ref[...] = jnp.dot(a_ref[...], w_ref[...].astype(BF16), preferred_element_type=F32).astype(o_ref.dtype)


def matmul(a, w, n, tm, tn, out_dtype=F32):
    m, k = a.shape
    return pl.pallas_call(
        _mm_kernel,
        out_shape=jax.ShapeDtypeStruct((m, n), out_dtype),
        grid=(m // tm, n // tn),
        in_specs=[pl.BlockSpec((tm, k), lambda i, j: (i, 0)),
                  pl.BlockSpec((k, tn), lambda i, j: (0, j))],
        out_specs=pl.BlockSpec((tm, tn), lambda i, j: (i, j)),
        compiler_params=_cparams(("parallel", "parallel"), 48),
        name="in_proj",
    )(a, w)


def _dprep_kernel(h_ref, w_ref, alog_ref, dtb_ref, o_ref):
    raw = jnp.dot(h_ref[...], w_ref[...], preferred_element_type=F32)
    beta = _sigmoid(raw)
    g = -jnp.exp(alog_ref[...]) * jax.nn.softplus(raw + dtb_ref[...])
    r = lax.broadcasted_iota(jnp.int32, (CHUNK, CHUNK), 0)
    c = lax.broadcasted_iota(jnp.int32, (CHUNK, CHUNK), 1)
    tri_lo = (c <= r).astype(F32)
    tri_up = (c >= r).astype(F32)
    lane = lax.broadcasted_iota(jnp.int32, (CHUNK, LANES), 1)
    for ch in range(ROW_TILE // CHUNK):
        sl = slice(ch * CHUNK, (ch + 1) * CHUNK)
        gch = g[sl]
        cum_f = jnp.dot(tri_lo, gch, preferred_element_type=F32, precision=lax.Precision.HIGHEST)
        cum_b = jnp.dot(tri_up, gch, preferred_element_type=F32, precision=lax.Precision.HIGHEST)
        gc = jnp.where(lane < 3 * DN_HEADS, cum_f, cum_b)
        o_ref[sl, :] = jnp.where(lane < 2 * DN_HEADS, beta[sl], gc)


def delta_gates(h2d, w_aux, a_log, dt_bias):
    m, d = h2d.shape
    lanes_vec = lambda v: jnp.zeros((1, LANES), F32).at[0, 2 * DN_HEADS:4 * DN_HEADS].set(v.reshape(-1))
    return pl.pallas_call(
        _dprep_kernel,
        out_shape=jax.ShapeDtypeStruct((m, LANES), F32),
        grid=(m // ROW_TILE,),
        in_specs=[pl.BlockSpec((ROW_TILE, d), lambda i: (i, 0)),
                  pl.BlockSpec((d, LANES), lambda i: (0, 0)),
                  pl.BlockSpec((1, LANES), lambda i: (0, 0)),
                  pl.BlockSpec((1, LANES), lambda i: (0, 0))],
        out_specs=pl.BlockSpec((ROW_TILE, LANES), lambda i: (i, 0)),
        compiler_params=_cparams(("parallel",), 32),
        name="delta_gates",
    )(h2d, w_aux, lanes_vec(a_log), lanes_vec(dt_bias))


def _mm_pairs(xs, ys):
    xb = [x.astype(BF16) for x in xs]
    yb = [_block_diag2(y.astype(BF16)) for y in ys]
    return [jnp.dot(x, y, preferred_element_type=F32) for x, y in zip(xb, yb)]


def _block_diag2(y):
    yy = jnp.concatenate([y, y], axis=0)
    r = lax.broadcasted_iota(jnp.int32, (2 * CHUNK, LANES), 0)
    c = lax.broadcasted_iota(jnp.int32, (2 * CHUNK, LANES), 1)
    return jnp.where((r // CHUNK) == (c // CHUNK), yy, jnp.zeros_like(yy))


def _unit_tri_inverse2(a2s, row, col):
    eye2 = (row == col).astype(F32)
    same = lambda n: (row // n) == (col // n)
    a8 = [jnp.where(same(8), a2, 0.0) for a2 in a2s]
    p = _mm_pairs(a8, a8)
    t = [eye2 - a for a in a8]
    t = [x + y for x, y in zip(t, _mm_pairs(t, p))]
    p = _mm_pairs(p, p)
    t = [x + y for x, y in zip(t, _mm_pairs(t, p))]
    for n in (16, 32, 64):
        off = same(n) & jnp.logical_not(same(n // 2))
        tc = _mm_pairs(t, [jnp.where(off, a2, 0.0) for a2 in a2s])
        t = [x - y for x, y in zip(t, _mm_pairs(tc, t))]
    return t


def _delta_kernel(q_ref, k_ref, v_ref, z_ref, aux_ref, wq_ref, wk_ref, wv_ref, ng_ref, o_ref,
                  qs, ks, vs, bcs, kp_s, n_s, of_s, ob_s, *, seq_len):
    l = seq_len
    n_chunks = l // CHUNK
    ctx_chunks = CTX_LEN // CHUNK
    head = pl.program_id(1)
    trow = lax.broadcasted_iota(jnp.int32, (l, 1), 0)

    def conv_silu(x_ref, w_ref):
        x = x_ref[0]
        w = w_ref[...]
        pad = (SHORT_CONV - 1) // 2
        acc = x * w[pad:pad + 1]
        for j in range(SHORT_CONV):
            s = j - pad
            if s == 0:
                continue
            shifted = pltpu.roll(x, (-s) % l, 0)
            ts = trow + s
            seg_lo = jnp.where(trow < CTX_LEN, 0, CTX_LEN)
            seg_hi = jnp.where(trow < CTX_LEN, CTX_LEN, l)
            valid = (ts >= seg_lo) & (ts < seg_hi)
            acc = acc + jnp.where(valid, shifted, 0.0) * w[j:j + 1]
        return acc * _sigmoid(acc)

    def l2norm(x):
        return x * lax.rsqrt(jnp.sum(x * x, axis=-1, keepdims=True) + EPS)

    qs[...] = l2norm(conv_silu(q_ref, wq_ref)) * (DN_D ** -0.5)
    ks[...] = l2norm(conv_silu(k_ref, wk_ref))
    vs[...] = conv_silu(v_ref, wv_ref)

    aux = aux_ref[0]
    alane = lax.broadcasted_iota(jnp.int32, (l, LANES), 1)
    for i in range(4):
        colv = jnp.sum(jnp.where(alane == head + i * DN_HEADS, aux, 0.0), axis=-1, keepdims=True)
        bcs[i] = jnp.broadcast_to(colv, (l, LANES))

    row = lax.broadcasted_iota(jnp.int32, (CHUNK, LANES), 0)
    lane = lax.broadcasted_iota(jnp.int32, (CHUNK, LANES), 1)
    fwd = lane < CHUNK
    col = jnp.where(fwd, lane, lane - CHUNK)
    sgn = jnp.where(fwd, 1, -1)
    incl = (row - col) * sgn >= 0
    strict = (row - col) * sgn > 0
    dot = functools.partial(jnp.dot, preferred_element_type=F32)
    nt = (((1,), (1,)), ((), ()))
    tn = (((0,), (0,)), ((), ()))
    acc_refs = (of_s, ob_s)

    def g_last_row(c, d):
        return bcs[2 + d, pl.ds(c * CHUNK + (CHUNK - 1 if d == 0 else 0), 1), :]

    def prep(i, carry):
        chunks = [i * PREP_UNROLL + j for j in range(PREP_UNROLL)]
        sls = [pl.ds(pl.multiple_of(c * CHUNK, CHUNK), CHUNK) for c in chunks]
        qkv = [(qs[sl, :], ks[sl, :], vs[sl, :]) for sl in sls]
        gates = [(bcs[0, sl, :], bcs[1, sl, :], bcs[2, sl, :], bcs[3, sl, :]) for sl in sls]
        a2s, qks = [], []
        for (q, k, v), (bf, bb, gcf, gcb) in zip(qkv, gates):
            b2 = jnp.where(fwd, bf, bb)
            c2 = jnp.where(fwd, gcf, gcb)
            r2 = jnp.sum(jnp.where(row == col, c2, 0.0), axis=0, keepdims=True)
            decay = jnp.where(incl, jnp.exp(jnp.where(incl, c2 - r2, 0.0)), 0.0)
            kk = jnp.concatenate([k, k], axis=0).astype(BF16)
            gram = lax.dot_general(k.astype(BF16), kk, nt, preferred_element_type=F32)
            qks.append(lax.dot_general(q.astype(BF16), kk, nt, preferred_element_type=F32) * decay)
            a2s.append(jnp.where(strict, b2 * gram * decay, 0.0))
        t2s = _unit_tri_inverse2(a2s, row, col)
        uws = []
        for (q, k, v), (bf, bb, gcf, gcb), t2 in zip(qkv, gates, t2s):
            rhs = jnp.concatenate(
                [jnp.concatenate([v * bf, k * (bf * jnp.exp(gcf))], axis=1),
                 jnp.concatenate([v * bb, k * (bb * jnp.exp(gcb))], axis=1)], axis=0)
            uws.append(dot(_block_diag2(t2.astype(BF16)), rhs.astype(BF16)))
        for c, sl, (q, k, v), g4, qk, uw in zip(chunks, sls, qkv, gates, qks, uws):
            for d in range(2):
                gc = g4[2 + d]
                uw_d = uw[d * CHUNK:(d + 1) * CHUNK].astype(BF16)
                kd = k * jnp.exp(g_last_row(c, d) - gc)
                kn = lax.dot_general(kd.astype(BF16), uw_d, tn, preferred_element_type=F32)
                qk_d = jnp.where(fwd == (d == 0), qk, 0.0).astype(BF16)
                po = dot(qk_d, jnp.concatenate([uw_d, uw_d], axis=0))
                p_mat = q * jnp.exp(gc) - po[:, DN_D:]
                kp_s[c, d] = jnp.concatenate([kn[:, DN_D:], p_mat], axis=0).astype(BF16)
                n_s[c, d] = kn[:, :DN_D]
                acc_refs[d][sl, :] = po[:, :DN_D]
        return carry

    lax.fori_loop(0, n_chunks // PREP_UNROLL, prep, 0)

    def one_dir(c, s, d):
        sl = pl.ds(pl.multiple_of(c * CHUNK, CHUNK), CHUNK)
        r = dot(kp_s[c, d], s.astype(BF16))
        acc_refs[d][sl, :] += r[DN_D:]
        return s * jnp.exp(g_last_row(c, d)) - r[:DN_D] + n_s[c, d]

    def scan(t, carry):
        s_f, s_b = carry
        cb = jnp.where(t < ctx_chunks, ctx_chunks - 1 - t, n_chunks + ctx_chunks - 1 - t)
        return one_dir(t, s_f, 0), one_dir(cb, s_b, 1)

    zero = jnp.zeros((DN_D, DN_D), F32)
    lax.fori_loop(0, n_chunks, scan, (zero, zero))

    o = of_s[...] + ob_s[...]
    o = o * lax.rsqrt(jnp.mean(o * o, axis=-1, keepdims=True) + EPS) * ng_ref[...]
    z = z_ref[0]
    o_ref[0] = (o * (z * _sigmoid(z))).astype(o_ref.dtype)


def deltanet_heads(p_main, aux, conv_w, norm_g):
    b, l, _ = p_main.shape
    n_chunks = l // CHUNK
    blk = lambda off: pl.BlockSpec((1, l, DN_D), lambda i, h, off=off: (i, 0, off + h))
    wblk = lambda off: pl.BlockSpec((SHORT_CONV, DN_D), lambda i, h, off=off: (0, off + h))
    seq = pltpu.VMEM((l, DN_D), F32)
    return pl.pallas_call(
        functools.partial(_delta_kernel, seq_len=l),
        out_shape=jax.ShapeDtypeStruct((b, l, DN_W), BF16),
        grid=(b, DN_HEADS),
        in_specs=[blk(0), blk(DN_HEADS), blk(2 * DN_HEADS), blk(3 * DN_HEADS),
                  pl.BlockSpec((1, l, LANES), lambda i, h: (i, 0, 0)),
                  wblk(0), wblk(DN_HEADS), wblk(2 * DN_HEADS),
                  pl.BlockSpec((1, DN_D), lambda i, h: (0, 0))],
        out_specs=pl.BlockSpec((1, l, DN_D), lambda i, h: (i, 0, h)),
        scratch_shapes=[seq, seq, seq,
                        pltpu.VMEM((4, l, LANES), F32),
                        pltpu.VMEM((n_chunks, 2, DN_D + CHUNK, DN_D), BF16),
                        pltpu.VMEM((n_chunks, 2, DN_D, DN_D), F32),
                        seq, seq],
        compiler_params=_cparams(("parallel", "parallel"), 48),
        name="deltanet_heads",
    )(p_main, p_main, p_main, p_main, aux, conv_w, conv_w, conv_w, norm_g.reshape(1, DN_D))


def _cf_kernel(val_ref, gate_ref, w_ref, g_ref, b_ref, o_ref, cbuf, lbuf, *, seq_len, vertical):
    lat = seq_len - CTX_LEN
    gate = gate_ref[0]
    u = val_ref[0] * _sigmoid(gate)
    w = w_ref[...]
    gain, bias = g_ref[...], b_ref[...]
    hpad = 16
    vpad = CF_PAD * GRID_W

    def finish(acc, row0):
        xc = acc - jnp.mean(acc, axis=-1, keepdims=True)
        y = xc * lax.rsqrt(jnp.mean(xc * xc, axis=-1, keepdims=True) + EPS) * gain + bias
        o_ref[0, pl.ds(row0, GRID_W), :] = (y * _sigmoid(y)).astype(o_ref.dtype)

    def window_conv(win, masked):
        r = lax.broadcasted_iota(jnp.int32, (GRID_W, 1), 0)
        acc = jnp.zeros((GRID_W, LANES), F32)
        for k in range(CF_KERNEL):
            off = hpad - CF_PAD + k
            piece = win[off:off + GRID_W]
            if masked:
                src = r + (k - CF_PAD)
                piece = jnp.where((src >= 0) & (src < GRID_W), piece, 0.0)
            acc = acc + piece * w[k:k + 1]
        return acc

    zeros_h = jnp.zeros((hpad, LANES), F32)
    cbuf[0:hpad, :] = zeros_h
    cbuf[hpad + CTX_LEN:, :] = zeros_h
    cbuf[hpad:hpad + CTX_LEN, :] = u[:CTX_LEN]

    def ctx_tile(g, carry):
        base = pl.multiple_of(g * GRID_W, GRID_W)
        finish(window_conv(cbuf[pl.ds(base, GRID_W + 2 * hpad), :], False), base)
        return carry

    lax.fori_loop(0, CTX_LEN // GRID_W, ctx_tile, 0)

    if vertical:
        zeros_v = jnp.zeros((vpad, LANES), F32)
        lbuf[0:vpad, :] = zeros_v
        lbuf[vpad + lat:, :] = zeros_v
        lbuf[vpad:vpad + lat, :] = u[CTX_LEN:]

        def lat_tile(g, carry):
            acc = jnp.zeros((GRID_W, LANES), F32)
            for k in range(CF_KERNEL):
                base = pl.multiple_of((g + k) * GRID_W, GRID_W)
                acc = acc + lbuf[pl.ds(base, GRID_W), :] * w[k:k + 1]
            finish(acc, pl.multiple_of(CTX_LEN + g * GRID_W, GRID_W))
            return carry
    else:
        lbuf[0:hpad, :] = zeros_h
        lbuf[hpad + lat:, :] = zeros_h
        lbuf[hpad:hpad + lat, :] = u[CTX_LEN:]

        def lat_tile(g, carry):
            base = pl.multiple_of(g * GRID_W, GRID_W)
            finish(window_conv(lbuf[pl.ds(base, GRID_W + 2 * hpad), :], True),
                   pl.multiple_of(CTX_LEN + base, GRID_W))
            return carry

    lax.fori_loop(0, lat // GRID_W, lat_tile, 0)


def conformer_heads(p_cf, dw_w, ln_g, ln_b, vertical):
    b, l, _ = p_cf.shape
    half = CF_W // 2
    lat = l - CTX_LEN
    first = half // LANES if vertical else 0
    val0 = first
    gate0 = CF_W // LANES + first
    lrows = lat + (2 * CF_PAD * GRID_W if vertical else 32)
    chan = lambda rows: pl.BlockSpec((rows, LANES), lambda i, j: (0, first + j))
    return pl.pallas_call(
        functools.partial(_cf_kernel, seq_len=l, vertical=vertical),
        out_shape=jax.ShapeDtypeStruct((b, l, half), BF16),
        grid=(b, half // LANES),
        in_specs=[pl.BlockSpec((1, l, LANES), lambda i, j: (i, 0, val0 + j)),
                  pl.BlockSpec((1, l, LANES), lambda i, j: (i, 0, gate0 + j)),
                  chan(CF_KERNEL), chan(1), chan(1)],
        out_specs=pl.BlockSpec((1, l, LANES), lambda i, j: (i, 0, j)),
        scratch_shapes=[pltpu.VMEM((CTX_LEN + 32, LANES), F32), pltpu.VMEM((lrows, LANES), F32)],
        compiler_params=_cparams(("parallel", "parallel"), 32),
        name="conformer_v" if vertical else "conformer_h",
    )(p_cf, p_cf, dw_w, ln_g.reshape(1, CF_W), ln_b.reshape(1, CF_W))


def _out_proj_kernel(dn_ref, ch_ref, cv_ref, w0_ref, w1_ref, w2_ref, x_ref, gl_ref, gc_ref, o_ref, *, tm):
    dot = functools.partial(jnp.dot, preferred_element_type=F32)
    acc = (dot(dn_ref[0], w0_ref[...].astype(BF16)) + dot(ch_ref[0], w1_ref[...].astype(BF16))
           + dot(cv_ref[0], w2_ref[...].astype(BF16)))
    row = pl.program_id(1) * tm + lax.broadcasted_iota(jnp.int32, (tm, 1), 0)
    gate = jnp.where(row < CTX_LEN, gc_ref[0], gl_ref[0])
    o_ref[0] = x_ref[0] + gate * acc


def out_proj_residual(dn, cfh, cfv, w_out, xs, mods, gate_col):
    b, l, d = xs.shape
    tm, tn = l // 2, 512
    k0, k1 = dn.shape[-1], cfh.shape[-1]
    act = lambda kk: pl.BlockSpec((1, tm, kk), lambda i, t, j: (i, t, 0))
    gate = lambda r: pl.BlockSpec((1, 1, tn), lambda i, t, j, r=r: (i if r is None else r, 0, gate_col * (d // tn) + j))
    return pl.pallas_call(
        functools.partial(_out_proj_kernel, tm=tm),
        out_shape=jax.ShapeDtypeStruct((b, l, d), F32),
        grid=(b, l // tm, d // tn),
        in_specs=[act(k0), act(k1), act(k1),
                  pl.BlockSpec((k0, tn), lambda i, t, j: (0, j)),
                  pl.BlockSpec((k1, tn), lambda i, t, j: (k0 // k1, j)),
                  pl.BlockSpec((k1, tn), lambda i, t, j: (k0 // k1 + 1, j)),
                  pl.BlockSpec((1, tm, tn), lambda i, t, j: (i, t, j)),
                  gate(None), gate(4)],
        out_specs=pl.BlockSpec((1, tm, tn), lambda i, t, j: (i, t, j)),
        compiler_params=_cparams(("parallel", "parallel", "parallel"), 56),
        name="out_proj",
    )(dn, cfh, cfv, w_out, w_out, w_out, xs, mods, mods)


def _norm_router_kernel(x_ref, g_ref, sh_ref, sc_ref, rw_ref, rb_ref, h_ref, idx_ref, wt_ref):
    x = x_ref[0]
    y = x * lax.rsqrt(jnp.mean(x * x, axis=-1, keepdims=True) + EPS) * g_ref[...]
    h = y * (1.0 + sc_ref[0]) + sh_ref[0]
    h_ref[0] = h
    logits = lax.dot_general(rw_ref[...], h, (((1,), (1,)), ((), ())),
                             preferred_element_type=F32, precision=lax.Precision.HIGHEST)
    scores = _sigmoid(logits)
    biased = scores + rb_ref[...]
    rows = [biased[e:e + 1] for e in range(N_EXPERTS)]
    srows = [scores[e:e + 1] for e in range(N_EXPERTS)]

    def pair_max(v):
        best = v[0] + v[1]
        for a in range(len(v)):
            for bb in range(a + 1, len(v)):
                if (a, bb) != (0, 1):
                    best = jnp.maximum(best, v[a] + v[bb])
        return best

    gscore = [pair_max(rows[g * 4:(g + 1) * 4]) for g in range(N_GROUPS)]
    best_g = jnp.zeros_like(gscore[0], dtype=jnp.int32)
    best_v = gscore[0]
    for g in range(1, N_GROUPS):
        better = gscore[g] > best_v
        best_g = jnp.where(better, g, best_g)
        best_v = jnp.where(better, gscore[g], best_v)

    def pick(vals):
        out = []
        for j in range(EXPERTS_PER_GROUP):
            m = vals[j]
            for g in range(1, N_GROUPS):
                m = jnp.where(best_g == g, vals[g * 4 + j], m)
            out.append(m)
        return out

    m, s = pick(rows), pick(srows)
    i1 = jnp.zeros_like(best_g)
    v1 = m[0]
    for j in range(1, 4):
        better = m[j] > v1
        i1 = jnp.where(better, j, i1)
        v1 = jnp.where(better, m[j], v1)
    i2 = jnp.full_like(best_g, -1)
    v2 = jnp.full_like(v1, -jnp.inf)
    for j in range(4):
        better = (i1 != j) & ((m[j] > v2) | (i2 < 0))
        i2 = jnp.where(better, j, i2)
        v2 = jnp.where(better, m[j], v2)

    def sel(vals, i):
        out = vals[0]
        for j in range(1, 4):
            out = jnp.where(i == j, vals[j], out)
        return out

    w1, w2 = sel(s, i1), sel(s, i2)
    tot = w1 + w2
    idx_ref[0:1, :] = best_g * EXPERTS_PER_GROUP + i1
    idx_ref[1:2, :] = best_g * EXPERTS_PER_GROUP + i2
    wt_ref[0:1, :] = w1 / tot
    wt_ref[1:2, :] = w2 / tot


def norm_router(xs, gain, mods, shift_col, router_w, router_bias):
    b, l, d = xs.shape
    nt = l // ROW_TILE
    tok = lambda i, t: (0, i * nt + t)
    return pl.pallas_call(
        _norm_router_kernel,
        out_shape=(jax.ShapeDtypeStruct((b, l, d), F32),
                   jax.ShapeDtypeStruct((2, b * l), jnp.int32),
                   jax.ShapeDtypeStruct((2, b * l), F32)),
        grid=(b, nt),
        in_specs=[pl.BlockSpec((1, ROW_TILE, d), lambda i, t: (i, t, 0)),
                  pl.BlockSpec((1, d), lambda i, t: (0, 0)),
                  pl.BlockSpec((1, 1, d), lambda i, t: (_mod_row(i, t), 0, shift_col)),
                  pl.BlockSpec((1, 1, d), lambda i, t: (_mod_row(i, t), 0, shift_col + 1)),
                  pl.BlockSpec((N_EXPERTS, d), lambda i, t: (0, 0)),
                  pl.BlockSpec((N_EXPERTS, 1), lambda i, t: (0, 0))],
        out_specs=(pl.BlockSpec((1, ROW_TILE, d), lambda i, t: (i, t, 0)),
                   pl.BlockSpec((2, ROW_TILE), tok),
                   pl.BlockSpec((2, ROW_TILE), tok)),
        compiler_params=_cparams(("parallel", "parallel"), 48),
        name="norm_router",
    )(xs, gain.reshape(1, d), mods, mods, router_w.T, router_bias.reshape(N_EXPERTS, 1))


def _moe_kernel(bexp_ref, stok_ref, sdst_ref, h_hbm, wg_ref, wu_ref, wd_ref, y_hbm,
                xbuf, ybuf, gsem, ssem):
    i = pl.program_id(0)
    n = pl.num_programs(0)
    slot = lax.rem(i, 2)
    other = 1 - slot

    def gather_start(block, buf):
        base = block * MOE_BLOCK
        for r in range(MOE_BLOCK):
            pltpu.make_async_copy(h_hbm.at[pl.ds(stok_ref[base + r], 1)], xbuf.at[buf, pl.ds(r, 1)],
                                  gsem.at[buf]).start()

    def gather_wait(buf):
        pltpu.make_async_copy(h_hbm.at[pl.ds(0, MOE_BLOCK)], xbuf.at[buf], gsem.at[buf]).wait()

    def scatter_start(block, buf):
        base = (block + 1) * MOE_BLOCK
        for r in range(MOE_BLOCK):
            pltpu.make_async_copy(ybuf.at[buf, pl.ds(r, 1)], y_hbm.at[pl.ds(sdst_ref[base + r], 1)],
                                  ssem.at[buf]).start()

    def scatter_wait(buf):
        pltpu.make_async_copy(ybuf.at[buf], y_hbm.at[pl.ds(0, MOE_BLOCK)], ssem.at[buf]).wait()

    @pl.when(i == 0)
    def _():
        gather_start(0, 0)
        ybuf[1] = jnp.zeros(ybuf.shape[1:], F32)

    gather_wait(slot)

    @pl.when(i >= 1)
    def _():
        scatter_wait(slot)

    gather_start(jnp.minimum(i + 1, n - 1), other)
    scatter_start(i - 1, other)
    x = xbuf[slot].astype(BF16)
    dot = functools.partial(jnp.dot, preferred_element_type=F32)
    g = dot(x, wg_ref[0])
    u = dot(x, wu_ref[0])
    act = (g * _sigmoid(g) * u).astype(BF16)
    ybuf[slot] = dot(act, wd_ref[0])

    @pl.when(i == n - 1)
    def _():
        scatter_start(i, slot)
        gather_wait(other)
        scatter_wait(other)
        scatter_wait(slot)


def moe_experts(h_tok, w_gate, w_up, w_down, block_expert, slot_tok, slot_dst):
    t, d = h_tok.shape
    n_blocks = block_expert.shape[0]
    f = w_gate.shape[-1]
    wspec = lambda shape: pl.BlockSpec(shape, lambda i, be, st, sd: (be[i], 0, 0), pipeline_mode=pl.Buffered(1))
    grid_spec = pltpu.PrefetchScalarGridSpec(
        num_scalar_prefetch=3,
        grid=(n_blocks,),
        in_specs=[pl.BlockSpec(memory_space=pl.ANY), wspec((1, d, f)), wspec((1, d, f)), wspec((1, f, d))],
        out_specs=pl.BlockSpec(memory_space=pl.ANY),
        scratch_shapes=[pltpu.VMEM((2, MOE_BLOCK, d), F32), pltpu.VMEM((2, MOE_BLOCK, d), F32),
                        pltpu.SemaphoreType.DMA((2,)), pltpu.SemaphoreType.DMA((2,))],
    )
    return pl.pallas_call(
        _moe_kernel,
        out_shape=jax.ShapeDtypeStruct((2 * t + MOE_DUMP_ROWS, d), F32),
        grid_spec=grid_spec,
        compiler_params=_cparams(("arbitrary",), 56),
        name="moe_experts",
    )(block_expert, slot_tok, slot_dst, h_tok, w_gate, w_up, w_down)


def route_slots(idx):
    t = idx.shape[1]
    n = 2 * t
    flat_e = idx.reshape(-1)
    onehot = (flat_e[:, None] == jnp.arange(N_EXPERTS)[None, :]).astype(jnp.int32)
    csum = jnp.cumsum(onehot, axis=0)
    rank = jnp.take_along_axis(csum, flat_e[:, None], axis=1)[:, 0] - 1
    counts = csum[-1]
    padded = (counts + MOE_BLOCK - 1) // MOE_BLOCK * MOE_BLOCK
    pad_end = jnp.cumsum(padded)
    dest = (pad_end - padded)[flat_e] + rank
    n_blocks = n // MOE_BLOCK + N_EXPERTS
    cap = n_blocks * MOE_BLOCK
    slot_src = jnp.full((cap,), -1, jnp.int32).at[dest].set(jnp.arange(n, dtype=jnp.int32))
    starts = jnp.arange(n_blocks, dtype=jnp.int32) * MOE_BLOCK
    last_start = jnp.maximum(pad_end[-1] - MOE_BLOCK, 0)
    block_expert = jnp.minimum(jnp.searchsorted(pad_end, jnp.minimum(starts, last_start), side='right'),
                               N_EXPERTS - 1).astype(jnp.int32)
    slot_tok = jnp.where(slot_src >= 0, slot_src % t, 0).astype(jnp.int32)
    pos = jnp.arange(-MOE_BLOCK, cap, dtype=jnp.int32)
    src_ext = jnp.concatenate([jnp.full((MOE_BLOCK,), -1, jnp.int32), slot_src])
    slot_dst = jnp.where(src_ext >= 0, src_ext, n + pos % MOE_DUMP_ROWS).astype(jnp.int32)
    return block_expert, slot_tok, slot_dst


def _combine_kernel(x_ref, y0_ref, y1_ref, w_ref, g_ref, fg_ref, o_ref, *, final):
    w = w_ref[...]
    x = x_ref[0] + g_ref[0] * (y0_ref[...] * w[:, 0:1] + y1_ref[...] * w[:, 1:2])
    if final:
        x = x * lax.rsqrt(jnp.mean(x * x, axis=-1, keepdims=True) + EPS) * fg_ref[...]
    o_ref[0] = x


def moe_combine(xs, yk, wts, mods, gate_col, final_g, final):
    b, l, d = xs.shape
    nt = l // ROW_TILE
    t_all = b * l
    t0 = 1 if final else 0
    n_out = nt - t0
    return pl.pallas_call(
        functools.partial(_combine_kernel, final=final),
        out_shape=jax.ShapeDtypeStruct((b, n_out * ROW_TILE, d), F32),
        grid=(b, n_out),
        in_specs=[pl.BlockSpec((1, ROW_TILE, d), lambda i, t: (i, t + t0, 0)),
                  pl.BlockSpec((ROW_TILE, d), lambda i, t: (i * nt + t + t0, 0)),
                  pl.BlockSpec((ROW_TILE, d), lambda i, t: (t_all // ROW_TILE + i * nt + t + t0, 0)),
                  pl.BlockSpec((ROW_TILE, 2), lambda i, t: (i * nt + t + t0, 0)),
                  pl.BlockSpec((1, 1, d), lambda i, t: (_mod_row(i, t + t0), 0, gate_col)),
                  pl.BlockSpec((1, d), lambda i, t: (0, 0))],
        out_specs=pl.BlockSpec((1, ROW_TILE, d), lambda i, t: (i, t, 0)),
        compiler_params=_cparams(("parallel", "parallel"), 48),
        name="moe_combine",
    )(xs, yk, yk, wts, mods, final_g.reshape(1, d))


def _layer(xs, mods, norm1_g, norm2_g, w_in, conv_qkv_w, a_log, dt_bias, dn_norm_g, cf_dw_w,
           cf_ln_g, cf_ln_b, w_out, router_w, router_bias, w_gate, w_up, w_down, final_g, final):
    b, l, d = xs.shape
    qkvz = 4 * DN_W
    gates0 = qkvz + 4 * DN_HEADS
    w_cf = w_in[:, gates0:].astype(BF16)
    w_aux = jnp.pad(w_in[:, qkvz:gates0], ((0, 0), (0, LANES - 4 * DN_HEADS))).astype(BF16)

    h = norm_modulate(xs, norm1_g, mods, 0, BF16).reshape(b * l, d)
    p_dn = matmul(h, w_in, qkvz, 1024, 512).reshape(b, l, qkvz)
    p_cf = matmul(h, w_cf, 2 * CF_W, 1024, 512).reshape(b, l, 2 * CF_W)
    aux = delta_gates(h, w_aux, a_log, dt_bias).reshape(b, l, LANES)
    dn = deltanet_heads(p_dn, aux, conv_qkv_w, dn_norm_g)
    cfh = conformer_heads(p_cf, cf_dw_w, cf_ln_g, cf_ln_b, False)
    cfv = conformer_heads(p_cf, cf_dw_w, cf_ln_g, cf_ln_b, True)
    xs = out_proj_residual(dn, cfh, cfv, w_out, xs, mods, 2)

    h2, idx, wts = norm_router(xs, norm2_g, mods, 3, router_w, router_bias)
    block_expert, slot_tok, slot_dst = route_slots(idx)
    yk = moe_experts(h2.reshape(b * l, d), w_gate.astype(BF16), w_up.astype(BF16), w_down.astype(BF16),
                     block_expert, slot_tok, slot_dst)
    return moe_combine(xs, yk, wts.T, mods, 5, final_g, final)


def kernel(x, c, ctx, c_ctx, w_ada, b_ada, norm1_g, norm2_g, w_in, conv_qkv_w, a_log, dt_bias,
           dn_norm_g, cf_dw_w, cf_ln_g, cf_ln_b, w_out, router_w, router_bias, w_gate, w_up,
           w_down, final_g):
    depth = w_ada.shape[0]
    bsz = x.shape[0]
    xs = jnp.concatenate([ctx, x], axis=1)
    c_all = jnp.concatenate([c, c_ctx[None, :], jnp.zeros((8 - bsz - 1, c.shape[1]), F32)], axis=0)
    mods = ada_mods(c_all, w_ada, b_ada)
    for i in range(depth):
        mods_i = mods[i].reshape(8, 1, -1)
        xs = _layer(xs, mods_i, norm1_g[i], norm2_g[i], w_in[i], conv_qkv_w[i], a_log[i], dt_bias[i],
                    dn_norm_g[i], cf_dw_w[i], cf_ln_g[i], cf_ln_b[i], w_out[i], router_w, router_bias,
                    w_gate[i], w_up[i], w_down[i], final_g, i == depth - 1)
    return xs
```

```python
import functools
import math

import jax
import jax.numpy as jnp
from jax import lax
from jax.experimental import pallas as pl
from jax.experimental.pallas import tpu as pltpu

F32 = jnp.float32
BF16 = jnp.bfloat16

D_MODEL = 4096
CTX_LEN = 256
GRID_W = 64
EPS = 1e-6
DN_HEADS = 16
DN_D = 128
DN_W = DN_HEADS * DN_D
SHORT_CONV = 5
CHUNK = 64
CF_W = 2048
CF_KERNEL = 31
CF_PAD = (CF_KERNEL - 1) // 2
N_EXPERTS = 16
N_GROUPS = 4
EXPERTS_PER_GROUP = 4
D_FF = 1024
MOE_BLOCK = 128
MOE_DUMP_ROWS = 2 * MOE_BLOCK
W_STAGE_ELEMS = 512 * 1024
LANES = 128
ROW_TILE = 256
PREP_UNROLL = 12
MIB = 1024 * 1024


def _cparams(sem, vmem_mib):
    return pltpu.CompilerParams(dimension_semantics=sem, vmem_limit_bytes=vmem_mib * MIB)


def _sigmoid(x):
    return 1.0 / (1.0 + jnp.exp(-x))


def _ada_kernel(c_ref, w_ref, b_ref, o_ref):
    c = c_ref[...]
    act = (c * _sigmoid(c)).astype(BF16)
    o_ref[0] = jnp.dot(act, w_ref[0].astype(BF16), preferred_element_type=F32) + b_ref[0]


def ada_mods(c_all, w_ada, b_ada):
    depth, d, n = w_ada.shape
    tn = 512
    return pl.pallas_call(
        _ada_kernel,
        out_shape=jax.ShapeDtypeStruct((depth, 8, n), F32),
        grid=(depth, n // tn),
        in_specs=[pl.BlockSpec((8, d), lambda i, j: (0, 0)),
                  pl.BlockSpec((1, d, tn), lambda i, j: (i, 0, j)),
                  pl.BlockSpec((1, 1, tn), lambda i, j: (i, 0, j))],
        out_specs=pl.BlockSpec((1, 8, tn), lambda i, j: (i, 0, j)),
        compiler_params=_cparams(("parallel", "parallel"), 40),
        name="ada_mods",
    )(c_all, w_ada, b_ada.reshape(depth, 1, n))


def _mod_row(b, t):
    return jnp.where(t == 0, 4, b)


def _norm_mod_kernel(x_ref, g_ref, sh_ref, sc_ref, o_ref):
    x = x_ref[0]
    y = x * lax.rsqrt(jnp.mean(x * x, axis=-1, keepdims=True) + EPS) * g_ref[...]
    o_ref[0] = (y * (1.0 + sc_ref[0]) + sh_ref[0]).astype(o_ref.dtype)


def norm_modulate(xs, gain, mods, shift_col, out_dtype):
    b, l, d = xs.shape
    return pl.pallas_call(
        _norm_mod_kernel,
        out_shape=jax.ShapeDtypeStruct((b, l, d), out_dtype),
        grid=(b, l // ROW_TILE),
        in_specs=[pl.BlockSpec((1, ROW_TILE, d), lambda i, t: (i, t, 0)),
                  pl.BlockSpec((1, d), lambda i, t: (0, 0)),
                  pl.BlockSpec((1, 1, d), lambda i, t: (_mod_row(i, t), 0, shift_col)),
                  pl.BlockSpec((1, 1, d), lambda i, t: (_mod_row(i, t), 0, shift_col + 1))],
        out_specs=pl.BlockSpec((1, ROW_TILE, d), lambda i, t: (i, t, 0)),
        compiler_params=_cparams(("parallel", "parallel"), 40),
        name="norm_modulate",
    )(xs, gain.reshape(1, d), mods, mods)


def _mm_kernel(a_ref, w_ref, o_ref):
    o_ref[...] = jnp.dot(a_ref[...], w_ref[...].astype(BF16), preferred_element_type=F32).astype(o_ref.dtype)


def matmul(a, w, n, tm, tn, layer=None, out_dtype=F32):
    m, k = a.shape
    if layer is None:
        w_spec = pl.BlockSpec((k, tn), lambda i, j: (0, j))
    else:
        w_spec = pl.BlockSpec((None, k, tn), lambda i, j: (layer, 0, j))
    return pl.pallas_call(
        _mm_kernel,
        out_shape=jax.ShapeDtypeStruct((m, n), out_dtype),
        grid=(m // tm, n // tn),
        in_specs=[pl.BlockSpec((tm, k), lambda i, j: (i, 0)), w_spec],
        out_specs=pl.BlockSpec((tm, tn), lambda i, j: (i, j)),
        compiler_params=_cparams(("parallel", "parallel"), 48),
        name="in_proj",
    )(a, w)


def _dprep_kernel(h_ref, w_ref, alog_ref, dtb_ref, o_ref):
    raw = jnp.dot(h_ref[...], w_ref[...], preferred_element_type=F32)
    beta = _sigmoid(raw)
    g = -jnp.exp(alog_ref[...]) * jax.nn.softplus(raw + dtb_ref[...])
    r = lax.broadcasted_iota(jnp.int32, (CHUNK, CHUNK), 0)
    c = lax.broadcasted_iota(jnp.int32, (CHUNK, CHUNK), 1)
    tri_lo = (c <= r).astype(F32)
    tri_up = (c >= r).astype(F32)
    lane = lax.broadcasted_iota(jnp.int32, (CHUNK, LANES), 1)
    for ch in range(ROW_TILE // CHUNK):
        sl = slice(ch * CHUNK, (ch + 1) * CHUNK)
        gch = g[sl]
        cum_f = jnp.dot(tri_lo, gch, preferred_element_type=F32, precision=lax.Precision.HIGHEST)
        cum_b = jnp.dot(tri_up, gch, preferred_element_type=F32, precision=lax.Precision.HIGHEST)
        gc = jnp.where(lane < 3 * DN_HEADS, cum_f, cum_b)
        o_ref[sl, :] = jnp.where(lane < 2 * DN_HEADS, beta[sl], gc)


def delta_gates(h2d, w_aux, a_log, dt_bias):
    m, d = h2d.shape
    lanes_vec = lambda v: jnp.zeros((1, LANES), F32).at[0, 2 * DN_HEADS:4 * DN_HEADS].set(v.reshape(-1))
    return pl.pallas_call(
        _dprep_kernel,
        out_shape=jax.ShapeDtypeStruct((m, LANES), F32),
        grid=(m // ROW_TILE,),
        in_specs=[pl.BlockSpec((ROW_TILE, d), lambda i: (i, 0)),
                  pl.BlockSpec((d, LANES), lambda i: (0, 0)),
                  pl.BlockSpec((1, LANES), lambda i: (0, 0)),
                  pl.BlockSpec((1, LANES), lambda i: (0, 0))],
        out_specs=pl.BlockSpec((ROW_TILE, LANES), lambda i: (i, 0)),
        compiler_params=_cparams(("parallel",), 32),
        name="delta_gates",
    )(h2d, w_aux, lanes_vec(a_log), lanes_vec(dt_bias))


def _mm_pairs(xs, ys):
    xb = [x.astype(BF16) for x in xs]
    yb = [_block_diag2(y.astype(BF16)) for y in ys]
    return [jnp.dot(x, y, preferred_element_type=F32) for x, y in zip(xb, yb)]


def _block_diag2(y):
    yy = jnp.concatenate([y, y], axis=0)
    r = lax.broadcasted_iota(jnp.int32, (2 * CHUNK, LANES), 0)
    c = lax.broadcasted_iota(jnp.int32, (2 * CHUNK, LANES), 1)
    return jnp.where((r // CHUNK) == (c // CHUNK), yy, jnp.zeros_like(yy))


def _unit_tri_inverse2(a2s, row, col):
    eye2 = (row == col).astype(F32)
    same = lambda n: (row // n) == (col // n)
    a8 = [jnp.where(same(8), a2, 0.0) for a2 in a2s]
    p = _mm_pairs(a8, a8)
    t = [eye2 - a for a in a8]
    t = [x + y for x, y in zip(t, _mm_pairs(t, p))]
    p = _mm_pairs(p, p)
    t = [x + y for x, y in zip(t, _mm_pairs(t, p))]
    for n in (16, 32, 64):
        off = same(n) & jnp.logical_not(same(n // 2))
        tc = _mm_pairs(t, [jnp.where(off, a2, 0.0) for a2 in a2s])
        t = [x - y for x, y in zip(t, _mm_pairs(tc, t))]
    return t


def _delta_kernel(q_ref, k_ref, v_ref, z_ref, aux_ref, wq_ref, wk_ref, wv_ref, ng_ref, o_ref,
                  qs, ks, vs, bcs, kp_s, n_s, of_s, ob_s, *, seq_len):
    l = seq_len
    n_chunks = l // CHUNK
    ctx_chunks = CTX_LEN // CHUNK
    head = pl.program_id(1)
    trow = lax.broadcasted_iota(jnp.int32, (l, 1), 0)

    def conv_silu(x_ref, w_ref):
        x = x_ref[0]
        w = w_ref[...]
        pad = (SHORT_CONV - 1) // 2
        acc = x * w[pad:pad + 1]
        for j in range(SHORT_CONV):
            s = j - pad
            if s == 0:
                continue
            shifted = pltpu.roll(x, (-s) % l, 0)
            ts = trow + s
            seg_lo = jnp.where(trow < CTX_LEN, 0, CTX_LEN)
            seg_hi = jnp.where(trow < CTX_LEN, CTX_LEN, l)
            valid = (ts >= seg_lo) & (ts < seg_hi)
            acc = acc + jnp.where(valid, shifted, 0.0) * w[j:j + 1]
        return acc * _sigmoid(acc)

    def l2norm(x):
        return x * lax.rsqrt(jnp.sum(x * x, axis=-1, keepdims=True) + EPS)

    qs[...] = l2norm(conv_silu(q_ref, wq_ref)) * (DN_D ** -0.5)
    ks[...] = l2norm(conv_silu(k_ref, wk_ref))
    vs[...] = conv_silu(v_ref, wv_ref)

    aux = aux_ref[0]
    alane = lax.broadcasted_iota(jnp.int32, (l, LANES), 1)
    for i in range(4):
        colv = jnp.sum(jnp.where(alane == head + i * DN_HEADS, aux, 0.0), axis=-1, keepdims=True)
        bcs[i] = jnp.broadcast_to(colv, (l, LANES))

    row = lax.broadcasted_iota(jnp.int32, (CHUNK, LANES), 0)
    lane = lax.broadcasted_iota(jnp.int32, (CHUNK, LANES), 1)
    fwd = lane < CHUNK
    col = jnp.where(fwd, lane, lane - CHUNK)
    sgn = jnp.where(fwd, 1, -1)
    incl = (row - col) * sgn >= 0
    strict = (row - col) * sgn > 0
    dot = functools.partial(jnp.dot, preferred_element_type=F32)
    nt = (((1,), (1,)), ((), ()))
    tn = (((0,), (0,)), ((), ()))
    acc_refs = (of_s, ob_s)

    def g_last_row(c, d):
        return bcs[2 + d, pl.ds(c * CHUNK + (CHUNK - 1 if d == 0 else 0), 1), :]

    def prep(i, carry):
        chunks = [i * PREP_UNROLL + j for j in range(PREP_UNROLL)]
        sls = [pl.ds(pl.multiple_of(c * CHUNK, CHUNK), CHUNK) for c in chunks]
        qkv = [(qs[sl, :], ks[sl, :], vs[sl, :]) for sl in sls]
        gates = [(bcs[0, sl, :], bcs[1, sl, :], bcs[2, sl, :], bcs[3, sl, :]) for sl in sls]
        a2s, qks = [], []
        for (q, k, v), (bf, bb, gcf, gcb) in zip(qkv, gates):
            b2 = jnp.where(fwd, bf, bb)
            c2 = jnp.where(fwd, gcf, gcb)
            r2 = jnp.sum(jnp.where(row == col, c2, 0.0), axis=0, keepdims=True)
            decay = jnp.where(incl, jnp.exp(jnp.where(incl, c2 - r2, 0.0)), 0.0)
            kk = jnp.concatenate([k, k], axis=0).astype(BF16)
            gram = lax.dot_general(k.astype(BF16), kk, nt, preferred_element_type=F32)
            qks.append(lax.dot_general(q.astype(BF16), kk, nt, preferred_element_type=F32) * decay)
            a2s.append(jnp.where(strict, b2 * gram * decay, 0.0))
        t2s = _unit_tri_inverse2(a2s, row, col)
        uws = []
        for (q, k, v), (bf, bb, gcf, gcb), t2 in zip(qkv, gates, t2s):
            rhs = jnp.concatenate(
                [jnp.concatenate([v * bf, k * (bf * jnp.exp(gcf))], axis=1),
                 jnp.concatenate([v * bb, k * (bb * jnp.exp(gcb))], axis=1)], axis=0)
            uws.append(dot(_block_diag2(t2.astype(BF16)), rhs.astype(BF16)))
        for c, sl, (q, k, v), g4, qk, uw in zip(chunks, sls, qkv, gates, qks, uws):
            for d in range(2):
                gc = g4[2 + d]
                uw_d = uw[d * CHUNK:(d + 1) * CHUNK].astype(BF16)
                kd = k * jnp.exp(g_last_row(c, d) - gc)
                kn = lax.dot_general(kd.astype(BF16), uw_d, tn, preferred_element_type=F32)
                qk_d = jnp.where(fwd == (d == 0), qk, 0.0).astype(BF16)
                po = dot(qk_d, jnp.concatenate([uw_d, uw_d], axis=0))
                p_mat = q * jnp.exp(gc) - po[:, DN_D:]
                kp_s[c, d] = jnp.concatenate([kn[:, DN_D:], p_mat], axis=0).astype(BF16)
                n_s[c, d] = kn[:, :DN_D]
                acc_refs[d][sl, :] = po[:, :DN_D]
        return carry

    lax.fori_loop(0, n_chunks // PREP_UNROLL, prep, 0)

    def one_dir(c, s, d):
        sl = pl.ds(pl.multiple_of(c * CHUNK, CHUNK), CHUNK)
        r = dot(kp_s[c, d], s.astype(BF16))
        acc_refs[d][sl, :] += r[DN_D:]
        return s * jnp.exp(g_last_row(c, d)) - r[:DN_D] + n_s[c, d]

    def scan(t, carry):
        s_f, s_b = carry
        cb = jnp.where(t < ctx_chunks, ctx_chunks - 1 - t, n_chunks + ctx_chunks - 1 - t)
        return one_dir(t, s_f, 0), one_dir(cb, s_b, 1)

    zero = jnp.zeros((DN_D, DN_D), F32)
    lax.fori_loop(0, n_chunks, scan, (zero, zero))

    o = of_s[...] + ob_s[...]
    o = o * lax.rsqrt(jnp.mean(o * o, axis=-1, keepdims=True) + EPS) * ng_ref[...]
    z = z_ref[0]
    o_ref[0] = (o * (z * _sigmoid(z))).astype(o_ref.dtype)


def deltanet_heads(p_main, aux, conv_w, norm_g):
    b, l, _ = p_main.shape
    n_chunks = l // CHUNK
    blk = lambda off: pl.BlockSpec((1, l, DN_D), lambda i, h, off=off: (i, 0, off + h))
    wblk = lambda off: pl.BlockSpec((SHORT_CONV, DN_D), lambda i, h, off=off: (0, off + h))
    seq = pltpu.VMEM((l, DN_D), F32)
    return pl.pallas_call(
        functools.partial(_delta_kernel, seq_len=l),
        out_shape=jax.ShapeDtypeStruct((b, l, DN_W), BF16),
        grid=(b, DN_HEADS),
        in_specs=[blk(0), blk(DN_HEADS), blk(2 * DN_HEADS), blk(3 * DN_HEADS),
                  pl.BlockSpec((1, l, LANES), lambda i, h: (i, 0, 0)),
                  wblk(0), wblk(DN_HEADS), wblk(2 * DN_HEADS),
                  pl.BlockSpec((1, DN_D), lambda i, h: (0, 0))],
        out_specs=pl.BlockSpec((1, l, DN_D), lambda i, h: (i, 0, h)),
        scratch_shapes=[seq, seq, seq,
                        pltpu.VMEM((4, l, LANES), F32),
                        pltpu.VMEM((n_chunks, 2, DN_D + CHUNK, DN_D), BF16),
                        pltpu.VMEM((n_chunks, 2, DN_D, DN_D), F32),
                        seq, seq],
        compiler_params=_cparams(("parallel", "parallel"), 48),
        name="deltanet_heads",
    )(p_main, p_main, p_main, p_main, aux, conv_w, conv_w, conv_w, norm_g.reshape(1, DN_D))


def _cf_kernel(val_ref, gate_ref, w_ref, g_ref, b_ref, o_ref, cbuf, lbuf, *, seq_len, vertical):
    lat = seq_len - CTX_LEN
    gate = gate_ref[0]
    u = val_ref[0] * _sigmoid(gate)
    w = w_ref[...]
    gain, bias = g_ref[...], b_ref[...]
    hpad = 16
    vpad = CF_PAD * GRID_W

    def finish(acc, row0):
        xc = acc - jnp.mean(acc, axis=-1, keepdims=True)
        y = xc * lax.rsqrt(jnp.mean(xc * xc, axis=-1, keepdims=True) + EPS) * gain + bias
        o_ref[0, pl.ds(row0, GRID_W), :] = (y * _sigmoid(y)).astype(o_ref.dtype)

    def window_conv(win, masked):
        r = lax.broadcasted_iota(jnp.int32, (GRID_W, 1), 0)
        acc = jnp.zeros((GRID_W, LANES), F32)
        for k in range(CF_KERNEL):
            off = hpad - CF_PAD + k
            piece = win[off:off + GRID_W]
            if masked:
                src = r + (k - CF_PAD)
                piece = jnp.where((src >= 0) & (src < GRID_W), piece, 0.0)
            acc = acc + piece * w[k:k + 1]
        return acc

    zeros_h = jnp.zeros((hpad, LANES), F32)
    cbuf[0:hpad, :] = zeros_h
    cbuf[hpad + CTX_LEN:, :] = zeros_h
    cbuf[hpad:hpad + CTX_LEN, :] = u[:CTX_LEN]

    def ctx_tile(g, carry):
        base = pl.multiple_of(g * GRID_W, GRID_W)
        finish(window_conv(cbuf[pl.ds(base, GRID_W + 2 * hpad), :], False), base)
        return carry

    lax.fori_loop(0, CTX_LEN // GRID_W, ctx_tile, 0)

    if vertical:
        zeros_v = jnp.zeros((vpad, LANES), F32)
        lbuf[0:vpad, :] = zeros_v
        lbuf[vpad + lat:, :] = zeros_v
        lbuf[vpad:vpad + lat, :] = u[CTX_LEN:]

        def lat_tile(g, carry):
            acc = jnp.zeros((GRID_W, LANES), F32)
            for k in range(CF_KERNEL):
                base = pl.multiple_of((g + k) * GRID_W, GRID_W)
                acc = acc + lbuf[pl.ds(base, GRID_W), :] * w[k:k + 1]
            finish(acc, pl.multiple_of(CTX_LEN + g * GRID_W, GRID_W))
            return carry
    else:
        lbuf[0:hpad, :] = zeros_h
        lbuf[hpad + lat:, :] = zeros_h
        lbuf[hpad:hpad + lat, :] = u[CTX_LEN:]

        def lat_tile(g, carry):
            base = pl.multiple_of(g * GRID_W, GRID_W)
            finish(window_conv(lbuf[pl.ds(base, GRID_W + 2 * hpad), :], True),
                   pl.multiple_of(CTX_LEN + base, GRID_W))
            return carry

    lax.fori_loop(0, lat // GRID_W, lat_tile, 0)


def conformer_heads(p_cf, dw_w, ln_g, ln_b, vertical):
    b, l, _ = p_cf.shape
    half = CF_W // 2
    lat = l - CTX_LEN
    first = half // LANES if vertical else 0
    val0 = first
    gate0 = CF_W // LANES + first
    lrows = lat + (2 * CF_PAD * GRID_W if vertical else 32)
    chan = lambda rows: pl.BlockSpec((rows, LANES), lambda i, j: (0, first + j))
    return pl.pallas_call(
        functools.partial(_cf_kernel, seq_len=l, vertical=vertical),
        out_shape=jax.ShapeDtypeStruct((b, l, half), BF16),
        grid=(b, half // LANES),
        in_specs=[pl.BlockSpec((1, l, LANES), lambda i, j: (i, 0, val0 + j)),
                  pl.BlockSpec((1, l, LANES), lambda i, j: (i, 0, gate0 + j)),
                  chan(CF_KERNEL), chan(1), chan(1)],
        out_specs=pl.BlockSpec((1, l, LANES), lambda i, j: (i, 0, j)),
        scratch_shapes=[pltpu.VMEM((CTX_LEN + 32, LANES), F32), pltpu.VMEM((lrows, LANES), F32)],
        compiler_params=_cparams(("parallel", "parallel"), 32),
        name="conformer_v" if vertical else "conformer_h",
    )(p_cf, p_cf, dw_w, ln_g.reshape(1, CF_W), ln_b.reshape(1, CF_W))


def _out_proj_kernel(dn_ref, ch_ref, cv_ref, w0_ref, w1_ref, w2_ref, x_ref, gl_ref, gc_ref, o_ref, *, tm):
    dot = functools.partial(jnp.dot, preferred_element_type=F32)
    acc = (dot(dn_ref[0], w0_ref[...].astype(BF16)) + dot(ch_ref[0], w1_ref[...].astype(BF16))
           + dot(cv_ref[0], w2_ref[...].astype(BF16)))
    row = pl.program_id(1) * tm + lax.broadcasted_iota(jnp.int32, (tm, 1), 0)
    gate = jnp.where(row < CTX_LEN, gc_ref[0], gl_ref[0])
    o_ref[0] = x_ref[0] + gate * acc


def out_proj_residual(dn, cfh, cfv, w_out, layer, xs, mods, gate_col):
    b, l, d = xs.shape
    tm, tn = l // 2, 512
    k0, k1 = dn.shape[-1], cfh.shape[-1]
    act = lambda kk: pl.BlockSpec((1, tm, kk), lambda i, t, j: (i, t, 0))
    gate = lambda r: pl.BlockSpec((1, 1, tn), lambda i, t, j, r=r: (i if r is None else r, 0, gate_col * (d // tn) + j))
    return pl.pallas_call(
        functools.partial(_out_proj_kernel, tm=tm),
        out_shape=jax.ShapeDtypeStruct((b, l, d), F32),
        grid=(b, l // tm, d // tn),
        in_specs=[act(k0), act(k1), act(k1),
                  pl.BlockSpec((None, k0, tn), lambda i, t, j: (layer, 0, j)),
                  pl.BlockSpec((None, k1, tn), lambda i, t, j: (layer, k0 // k1, j)),
                  pl.BlockSpec((None, k1, tn), lambda i, t, j: (layer, k0 // k1 + 1, j)),
                  pl.BlockSpec((1, tm, tn), lambda i, t, j: (i, t, j)),
                  gate(None), gate(4)],
        out_specs=pl.BlockSpec((1, tm, tn), lambda i, t, j: (i, t, j)),
        compiler_params=_cparams(("parallel", "parallel", "parallel"), 56),
        name="out_proj",
    )(dn, cfh, cfv, w_out, w_out, w_out, xs, mods, mods)


def _norm_router_kernel(x_ref, g_ref, sh_ref, sc_ref, rw_ref, rb_ref, h_ref, idx_ref, wt_ref):
    x = x_ref[0]
    y = x * lax.rsqrt(jnp.mean(x * x, axis=-1, keepdims=True) + EPS) * g_ref[...]
    h = y * (1.0 + sc_ref[0]) + sh_ref[0]
    h_ref[0] = h
    logits = lax.dot_general(rw_ref[...], h, (((1,), (1,)), ((), ())),
                             preferred_element_type=F32, precision=lax.Precision.HIGHEST)
    scores = _sigmoid(logits)
    biased = scores + rb_ref[...]
    rows = [biased[e:e + 1] for e in range(N_EXPERTS)]
    srows = [scores[e:e + 1] for e in range(N_EXPERTS)]

    def pair_max(v):
        best = v[0] + v[1]
        for a in range(len(v)):
            for bb in range(a + 1, len(v)):
                if (a, bb) != (0, 1):
                    best = jnp.maximum(best, v[a] + v[bb])
        return best

    gscore = [pair_max(rows[g * 4:(g + 1) * 4]) for g in range(N_GROUPS)]
    best_g = jnp.zeros_like(gscore[0], dtype=jnp.int32)
    best_v = gscore[0]
    for g in range(1, N_GROUPS):
        better = gscore[g] > best_v
        best_g = jnp.where(better, g, best_g)
        best_v = jnp.where(better, gscore[g], best_v)

    def pick(vals):
        out = []
        for j in range(EXPERTS_PER_GROUP):
            m = vals[j]
            for g in range(1, N_GROUPS):
                m = jnp.where(best_g == g, vals[g * 4 + j], m)
            out.append(m)
        return out

    m, s = pick(rows), pick(srows)
    i1 = jnp.zeros_like(best_g)
    v1 = m[0]
    for j in range(1, 4):
        better = m[j] > v1
        i1 = jnp.where(better, j, i1)
        v1 = jnp.where(better, m[j], v1)
    i2 = jnp.full_like(best_g, -1)
    v2 = jnp.full_like(v1, -jnp.inf)
    for j in range(4):
        better = (i1 != j) & ((m[j] > v2) | (i2 < 0))
        i2 = jnp.where(better, j, i2)
        v2 = jnp.where(better, m[j], v2)

    def sel(vals, i):
        out = vals[0]
        for j in range(1, 4):
            out = jnp.where(i == j, vals[j], out)
        return out

    w1, w2 = sel(s, i1), sel(s, i2)
    tot = w1 + w2
    idx_ref[0:1, :] = best_g * EXPERTS_PER_GROUP + i1
    idx_ref[1:2, :] = best_g * EXPERTS_PER_GROUP + i2
    wt_ref[0:1, :] = w1 / tot
    wt_ref[1:2, :] = w2 / tot


def norm_router(xs, gain, mods, shift_col, router_w, router_bias):
    b, l, d = xs.shape
    nt = l // ROW_TILE
    tok = lambda i, t: (0, i * nt + t)
    return pl.pallas_call(
        _norm_router_kernel,
        out_shape=(jax.ShapeDtypeStruct((b, l, d), F32),
                   jax.ShapeDtypeStruct((2, b * l), jnp.int32),
                   jax.ShapeDtypeStruct((2, b * l), F32)),
        grid=(b, nt),
        in_specs=[pl.BlockSpec((1, ROW_TILE, d), lambda i, t: (i, t, 0)),
                  pl.BlockSpec((1, d), lambda i, t: (0, 0)),
                  pl.BlockSpec((1, 1, d), lambda i, t: (_mod_row(i, t), 0, shift_col)),
                  pl.BlockSpec((1, 1, d), lambda i, t: (_mod_row(i, t), 0, shift_col + 1)),
                  pl.BlockSpec((N_EXPERTS, d), lambda i, t: (0, 0)),
                  pl.BlockSpec((N_EXPERTS, 1), lambda i, t: (0, 0))],
        out_specs=(pl.BlockSpec((1, ROW_TILE, d), lambda i, t: (i, t, 0)),
                   pl.BlockSpec((2, ROW_TILE), tok),
                   pl.BlockSpec((2, ROW_TILE), tok)),
        compiler_params=_cparams(("parallel", "parallel"), 48),
        name="norm_router",
    )(xs, gain.reshape(1, d), mods, mods, router_w.T, router_bias.reshape(N_EXPERTS, 1))


def _moe_kernel(bexp_ref, stok_ref, sdst_ref, h_hbm, wg_hbm, wu_hbm, wd_hbm, y_hbm,
                xbuf, ybuf, wg_s, wu_s, wd_s, stage_a, stage_b, gsem, ssem, wsem, *, layer):
    i = pl.program_id(0)
    n = pl.num_programs(0)
    slot = lax.rem(i, 2)
    other = 1 - slot
    d, f = wg_s.shape

    def gather_start(block, buf):
        base = block * MOE_BLOCK
        for r in range(MOE_BLOCK):
            pltpu.make_async_copy(h_hbm.at[pl.ds(stok_ref[base + r], 1)], xbuf.at[buf, pl.ds(r, 1)],
                                  gsem.at[buf]).start()

    def gather_wait(buf):
        pltpu.make_async_copy(h_hbm.at[pl.ds(0, MOE_BLOCK)], xbuf.at[buf], gsem.at[buf]).wait()

    def scatter_start(block, buf):
        base = (block + 1) * MOE_BLOCK
        for r in range(MOE_BLOCK):
            pltpu.make_async_copy(ybuf.at[buf, pl.ds(r, 1)], y_hbm.at[pl.ds(sdst_ref[base + r], 1)],
                                  ssem.at[buf]).start()

    def scatter_wait(buf):
        pltpu.make_async_copy(ybuf.at[buf], y_hbm.at[pl.ds(0, MOE_BLOCK)], ssem.at[buf]).wait()

    def load_expert(e):
        rows_a, rows_b = stage_a.shape[1], stage_b.shape[1]
        chunks = []
        for w_hbm, w_s in ((wg_hbm, wg_s), (wu_hbm, wu_s)):
            for c in range(d // rows_a):
                rows = pl.ds(c * rows_a, rows_a)
                chunks.append((w_hbm.at[layer, e, rows], stage_a, w_s.at[rows]))
        for c in range(f // rows_b):
            rows = pl.ds(c * rows_b, rows_b)
            chunks.append((wd_hbm.at[layer, e, rows], stage_b, wd_s.at[rows]))
        copies = [pltpu.make_async_copy(src, stage.at[j % 2], wsem.at[j % 2])
                  for j, (src, stage, _) in enumerate(chunks)]
        copies[0].start()
        copies[1].start()
        for j, (_, stage, dst) in enumerate(chunks):
            copies[j].wait()
            dst[...] = stage[j % 2].astype(BF16)
            if j + 2 < len(chunks):
                copies[j + 2].start()

    @pl.when(i == 0)
    def _():
        gather_start(0, 0)
        ybuf[1] = jnp.zeros(ybuf.shape[1:], F32)

    @pl.when((i == 0) | (bexp_ref[i] != bexp_ref[jnp.maximum(i - 1, 0)]))
    def _():
        load_expert(bexp_ref[i])

    gather_wait(slot)

    @pl.when(i >= 1)
    def _():
        scatter_wait(slot)

    gather_start(jnp.minimum(i + 1, n - 1), other)
    scatter_start(i - 1, other)
    x = xbuf[slot].astype(BF16)
    dot = functools.partial(jnp.dot, preferred_element_type=F32)
    g = dot(x, wg_s[...])
    u = dot(x, wu_s[...])
    act = (g * _sigmoid(g) * u).astype(BF16)
    ybuf[slot] = dot(act, wd_s[...])

    @pl.when(i == n - 1)
    def _():
        scatter_start(i, slot)
        gather_wait(other)
        scatter_wait(other)
        scatter_wait(slot)


def moe_experts(h_tok, w_gate, w_up, w_down, layer, block_expert, slot_tok, slot_dst):
    t, d = h_tok.shape
    n_blocks = block_expert.shape[0]
    f = w_gate.shape[-1]
    hbm = pl.BlockSpec(memory_space=pl.ANY)
    grid_spec = pltpu.PrefetchScalarGridSpec(
        num_scalar_prefetch=3,
        grid=(n_blocks,),
        in_specs=[hbm, hbm, hbm, hbm],
        out_specs=hbm,
        scratch_shapes=[pltpu.VMEM((2, MOE_BLOCK, d), F32), pltpu.VMEM((2, MOE_BLOCK, d), F32),
                        pltpu.VMEM((d, f), BF16), pltpu.VMEM((d, f), BF16), pltpu.VMEM((f, d), BF16),
                        pltpu.VMEM((2, W_STAGE_ELEMS // f, f), F32), pltpu.VMEM((2, W_STAGE_ELEMS // d, d), F32),
                        pltpu.SemaphoreType.DMA((2,)), pltpu.SemaphoreType.DMA((2,)),
                        pltpu.SemaphoreType.DMA((2,))],
    )
    return pl.pallas_call(
        functools.partial(_moe_kernel, layer=layer),
        out_shape=jax.ShapeDtypeStruct((2 * t + MOE_DUMP_ROWS, d), F32),
        grid_spec=grid_spec,
        compiler_params=_cparams(("arbitrary",), 56),
        name="moe_experts",
    )(block_expert, slot_tok, slot_dst, h_tok, w_gate, w_up, w_down)


def route_slots(idx):
    t = idx.shape[1]
    n = 2 * t
    flat_e = idx.reshape(-1)
    onehot = (flat_e[:, None] == jnp.arange(N_EXPERTS)[None, :]).astype(jnp.int32)
    csum = jnp.cumsum(onehot, axis=0)
    rank = jnp.take_along_axis(csum, flat_e[:, None], axis=1)[:, 0] - 1
    counts = csum[-1]
    padded = (counts + MOE_BLOCK - 1) // MOE_BLOCK * MOE_BLOCK
    pad_end = jnp.cumsum(padded)
    dest = (pad_end - padded)[flat_e] + rank
    n_blocks = n // MOE_BLOCK + N_EXPERTS
    cap = n_blocks * MOE_BLOCK
    slot_src = jnp.full((cap,), -1, jnp.int32).at[dest].set(jnp.arange(n, dtype=jnp.int32))
    starts = jnp.arange(n_blocks, dtype=jnp.int32) * MOE_BLOCK
    last_start = jnp.maximum(pad_end[-1] - MOE_BLOCK, 0)
    block_expert = jnp.minimum(jnp.searchsorted(pad_end, jnp.minimum(starts, last_start), side='right'),
                               N_EXPERTS - 1).astype(jnp.int32)
    slot_tok = jnp.where(slot_src >= 0, slot_src % t, 0).astype(jnp.int32)
    pos = jnp.arange(-MOE_BLOCK, cap, dtype=jnp.int32)
    src_ext = jnp.concatenate([jnp.full((MOE_BLOCK,), -1, jnp.int32), slot_src])
    slot_dst = jnp.where(src_ext >= 0, src_ext, n + pos % MOE_DUMP_ROWS).astype(jnp.int32)
    return block_expert, slot_tok, slot_dst


def _combine_kernel(x_ref, y0_ref, y1_ref, w_ref, g_ref, fg_ref, o_ref, *, final):
    w = w_ref[...]
    x = x_ref[0] + g_ref[0] * (y0_ref[...] * w[:, 0:1] + y1_ref[...] * w[:, 1:2])
    if final:
        x = x * lax.rsqrt(jnp.mean(x * x, axis=-1, keepdims=True) + EPS) * fg_ref[...]
    o_ref[0] = x


def moe_combine(xs, yk, wts, mods, gate_col, final_g, final):
    b, l, d = xs.shape
    nt = l // ROW_TILE
    t_all = b * l
    t0 = 1 if final else 0
    n_out = nt - t0
    return pl.pallas_call(
        functools.partial(_combine_kernel, final=final),
        out_shape=jax.ShapeDtypeStruct((b, n_out * ROW_TILE, d), F32),
        grid=(b, n_out),
        in_specs=[pl.BlockSpec((1, ROW_TILE, d), lambda i, t: (i, t + t0, 0)),
                  pl.BlockSpec((ROW_TILE, d), lambda i, t: (i * nt + t + t0, 0)),
                  pl.BlockSpec((ROW_TILE, d), lambda i, t: (t_all // ROW_TILE + i * nt + t + t0, 0)),
                  pl.BlockSpec((ROW_TILE, 2), lambda i, t: (i * nt + t + t0, 0)),
                  pl.BlockSpec((1, 1, d), lambda i, t: (_mod_row(i, t + t0), 0, gate_col)),
                  pl.BlockSpec((1, d), lambda i, t: (0, 0))],
        out_specs=pl.BlockSpec((1, ROW_TILE, d), lambda i, t: (i, t, 0)),
        compiler_params=_cparams(("parallel", "parallel"), 48),
        name="moe_combine",
    )(xs, yk, yk, wts, mods, final_g.reshape(1, d))


def _layer(xs, layer, mods, norm1_g, norm2_g, w_in, conv_qkv_w, a_log, dt_bias, dn_norm_g, cf_dw_w,
           cf_ln_g, cf_ln_b, w_out, router_w, router_bias, w_gate, w_up, w_down, final_g, final):
    b, l, d = xs.shape
    qkvz = 4 * DN_W
    gates0 = qkvz + 4 * DN_HEADS
    w_cf = w_in[layer, :, gates0:].astype(BF16)
    w_aux = jnp.pad(w_in[layer, :, qkvz:gates0], ((0, 0), (0, LANES - 4 * DN_HEADS))).astype(BF16)

    h = norm_modulate(xs, norm1_g, mods, 0, BF16).reshape(b * l, d)
    p_dn = matmul(h, w_in, qkvz, 1024, 512, layer).reshape(b, l, qkvz)
    p_cf = matmul(h, w_cf, 2 * CF_W, 1024, 512).reshape(b, l, 2 * CF_W)
    aux = delta_gates(h, w_aux, a_log, dt_bias).reshape(b, l, LANES)
    dn = deltanet_heads(p_dn, aux, conv_qkv_w, dn_norm_g)
    cfh = conformer_heads(p_cf, cf_dw_w, cf_ln_g, cf_ln_b, False)
    cfv = conformer_heads(p_cf, cf_dw_w, cf_ln_g, cf_ln_b, True)
    xs = out_proj_residual(dn, cfh, cfv, w_out, layer, xs, mods, 2)

    h2, idx, wts = norm_router(xs, norm2_g, mods, 3, router_w, router_bias)
    block_expert, slot_tok, slot_dst = route_slots(idx)
    yk = moe_experts(h2.reshape(b * l, d), w_gate, w_up, w_down, layer, block_expert, slot_tok, slot_dst)
    return moe_combine(xs, yk, wts.T, mods, 5, final_g, final)


def kernel(x, c, ctx, c_ctx, w_ada, b_ada, norm1_g, norm2_g, w_in, conv_qkv_w, a_log, dt_bias,
           dn_norm_g, cf_dw_w, cf_ln_g, cf_ln_b, w_out, router_w, router_bias, w_gate, w_up,
           w_down, final_g):
    depth = w_ada.shape[0]
    bsz = x.shape[0]
    xs = jnp.concatenate([ctx, x], axis=1)
    c_all = jnp.concatenate([c, c_ctx[None, :], jnp.zeros((8 - bsz - 1, c.shape[1]), F32)], axis=0)
    mods = ada_mods(c_all, w_ada, b_ada)
    for i in range(depth):
        mods_i = mods[i].reshape(8, 1, -1)
        xs = _layer(xs, i, mods_i, norm1_g[i], norm2_g[i], w_in, conv_qkv_w[i], a_log[i], dt_bias[i],
                    dn_norm_g[i], cf_dw_w[i], cf_ln_g[i], cf_ln_b[i], w_out, router_w, router_bias,
                    w_gate, w_up, w_down, final_g, i == depth - 1)
    return xs
```

```python
import functools
import math

import jax
import jax.numpy as jnp
from jax import lax
from jax.experimental import pallas as pl
from jax.experimental.pallas import tpu as pltpu

F32 = jnp.float32
BF16 = jnp.bfloat16

D_MODEL = 4096
CTX_LEN = 256
GRID_W = 64
EPS = 1e-6
DN_HEADS = 16
DN_D = 128
DN_W = DN_HEADS * DN_D
SHORT_CONV = 5
CHUNK = 64
CF_W = 2048
CF_KERNEL = 31
CF_PAD = (CF_KERNEL - 1) // 2
N_EXPERTS = 16
N_GROUPS = 4
EXPERTS_PER_GROUP = 4
D_FF = 1024
MOE_BLOCK = 128
MOE_DUMP_ROWS = 2 * MOE_BLOCK
W_STAGE_ELEMS = 512 * 1024
LANES = 128
ROW_TILE = 256
PREP_UNROLL = 12
CF_TILE_UNROLL = 4
MIB = 1024 * 1024


def _cparams(sem, vmem_mib):
    return pltpu.CompilerParams(dimension_semantics=sem, vmem_limit_bytes=vmem_mib * MIB)


def _sigmoid(x):
    return 1.0 / (1.0 + jnp.exp(-x))


def _ada_kernel(c_ref, w_ref, b_ref, o_ref):
    c = c_ref[...]
    act = (c * _sigmoid(c)).astype(BF16)
    o_ref[0] = jnp.dot(act, w_ref[0].astype(BF16), preferred_element_type=F32) + b_ref[0]


def ada_mods(c_all, w_ada, b_ada):
    depth, d, n = w_ada.shape
    tn = 512
    return pl.pallas_call(
        _ada_kernel,
        out_shape=jax.ShapeDtypeStruct((depth, 8, n), F32),
        grid=(depth, n // tn),
        in_specs=[pl.BlockSpec((8, d), lambda i, j: (0, 0)),
                  pl.BlockSpec((1, d, tn), lambda i, j: (i, 0, j)),
                  pl.BlockSpec((1, 1, tn), lambda i, j: (i, 0, j))],
        out_specs=pl.BlockSpec((1, 8, tn), lambda i, j: (i, 0, j)),
        compiler_params=_cparams(("parallel", "parallel"), 40),
        name="ada_mods",
    )(c_all, w_ada, b_ada.reshape(depth, 1, n))


def _mod_row(b, t):
    return jnp.where(t == 0, 4, b)


def _norm_mod_kernel(x_ref, g_ref, sh_ref, sc_ref, o_ref):
    x = x_ref[0]
    y = x * lax.rsqrt(jnp.mean(x * x, axis=-1, keepdims=True) + EPS) * g_ref[...]
    o_ref[0] = (y * (1.0 + sc_ref[0]) + sh_ref[0]).astype(o_ref.dtype)


def norm_modulate(xs, gain, mods, shift_col, out_dtype):
    b, l, d = xs.shape
    return pl.pallas_call(
        _norm_mod_kernel,
        out_shape=jax.ShapeDtypeStruct((b, l, d), out_dtype),
        grid=(b, l // ROW_TILE),
        in_specs=[pl.BlockSpec((1, ROW_TILE, d), lambda i, t: (i, t, 0)),
                  pl.BlockSpec((1, d), lambda i, t: (0, 0)),
                  pl.BlockSpec((1, 1, d), lambda i, t: (_mod_row(i, t), 0, shift_col)),
                  pl.BlockSpec((1, 1, d), lambda i, t: (_mod_row(i, t), 0, shift_col + 1))],
        out_specs=pl.BlockSpec((1, ROW_TILE, d), lambda i, t: (i, t, 0)),
        compiler_params=_cparams(("parallel", "parallel"), 40),
        name="norm_modulate",
    )(xs, gain.reshape(1, d), mods, mods)


def _mm_nt_kernel(a_ref, w_ref, o_ref):
    w = w_ref[...].astype(BF16)
    o_ref[...] = lax.dot_general(a_ref[...], w, (((1,), (1,)), ((), ())), preferred_element_type=F32)


def matmul_nt(a, w_t, layer, tm, tn):
    m, k = a.shape
    n = w_t.shape[1]
    return pl.pallas_call(
        _mm_nt_kernel,
        out_shape=jax.ShapeDtypeStruct((m, n), F32),
        grid=(m // tm, pl.cdiv(n, tn)),
        in_specs=[pl.BlockSpec((tm, k), lambda i, j: (i, 0)),
                  pl.BlockSpec((None, tn, k), lambda i, j: (layer, j, 0))],
        out_specs=pl.BlockSpec((tm, tn), lambda i, j: (i, j)),
        compiler_params=_cparams(("parallel", "parallel"), 48),
        name="in_proj",
    )(a, w_t)


def _dprep_kernel(p_ref, alog_ref, dtb_ref, o_ref):
    raw = p_ref[...]
    beta = _sigmoid(raw)
    g = -jnp.exp(alog_ref[...]) * jax.nn.softplus(raw + dtb_ref[...])
    r = lax.broadcasted_iota(jnp.int32, (CHUNK, CHUNK), 0)
    c = lax.broadcasted_iota(jnp.int32, (CHUNK, CHUNK), 1)
    tri_lo = (c <= r).astype(F32)
    tri_up = (c >= r).astype(F32)
    lane = lax.broadcasted_iota(jnp.int32, (CHUNK, LANES), 1)
    for ch in range(ROW_TILE // CHUNK):
        sl = slice(ch * CHUNK, (ch + 1) * CHUNK)
        gch = g[sl]
        cum_f = jnp.dot(tri_lo, gch, preferred_element_type=F32, precision=lax.Precision.HIGHEST)
        cum_b = jnp.dot(tri_up, gch, preferred_element_type=F32, precision=lax.Precision.HIGHEST)
        gc = jnp.where(lane < 3 * DN_HEADS, cum_f, cum_b)
        o_ref[sl, :] = jnp.where(lane < 2 * DN_HEADS, beta[sl], gc)


def delta_gates(p2d, a_log, dt_bias):
    m = p2d.shape[0]
    lanes_vec = lambda v: jnp.zeros((1, LANES), F32).at[0, 2 * DN_HEADS:4 * DN_HEADS].set(v.reshape(-1))
    return pl.pallas_call(
        _dprep_kernel,
        out_shape=jax.ShapeDtypeStruct((m, LANES), F32),
        grid=(m // ROW_TILE,),
        in_specs=[pl.BlockSpec((ROW_TILE, LANES), lambda i: (i, 4 * DN_W // LANES)),
                  pl.BlockSpec((1, LANES), lambda i: (0, 0)),
                  pl.BlockSpec((1, LANES), lambda i: (0, 0))],
        out_specs=pl.BlockSpec((ROW_TILE, LANES), lambda i: (i, 0)),
        compiler_params=_cparams(("parallel",), 32),
        name="delta_gates",
    )(p2d, lanes_vec(a_log), lanes_vec(dt_bias))


def _mm_pairs(xs, ys):
    xb = [x.astype(BF16) for x in xs]
    yb = [_block_diag2(y.astype(BF16)) for y in ys]
    return [jnp.dot(x, y, preferred_element_type=F32) for x, y in zip(xb, yb)]


def _block_diag2(y):
    yy = jnp.concatenate([y, y], axis=0)
    r = lax.broadcasted_iota(jnp.int32, (2 * CHUNK, LANES), 0)
    c = lax.broadcasted_iota(jnp.int32, (2 * CHUNK, LANES), 1)
    return jnp.where((r // CHUNK) == (c // CHUNK), yy, jnp.zeros_like(yy))


def _unit_tri_inverse2(a2s, row, col):
    eye2 = (row == col).astype(F32)
    same = lambda n: (row // n) == (col // n)
    a8 = [jnp.where(same(8), a2, 0.0) for a2 in a2s]
    p = _mm_pairs(a8, a8)
    t = [eye2 - a for a in a8]
    t = [x + y for x, y in zip(t, _mm_pairs(t, p))]
    p = _mm_pairs(p, p)
    t = [x + y for x, y in zip(t, _mm_pairs(t, p))]
    for n in (16, 32, 64):
        off = same(n) & jnp.logical_not(same(n // 2))
        tc = _mm_pairs(t, [jnp.where(off, a2, 0.0) for a2 in a2s])
        t = [x - y for x, y in zip(t, _mm_pairs(tc, t))]
    return t


def _delta_kernel(q_ref, k_ref, v_ref, z_ref, aux_ref, wq_ref, wk_ref, wv_ref, ng_ref, o_ref,
                  qs, ks, vs, bcs, kp_s, n_s, of_s, ob_s, *, seq_len):
    l = seq_len
    n_chunks = l // CHUNK
    ctx_chunks = CTX_LEN // CHUNK
    head = pl.program_id(1)
    trow = lax.broadcasted_iota(jnp.int32, (l, 1), 0)

    def conv_silu(x_ref, w_ref):
        x = x_ref[0]
        w = w_ref[...]
        pad = (SHORT_CONV - 1) // 2
        acc = x * w[pad:pad + 1]
        for j in range(SHORT_CONV):
            s = j - pad
            if s == 0:
                continue
            shifted = pltpu.roll(x, (-s) % l, 0)
            ts = trow + s
            seg_lo = jnp.where(trow < CTX_LEN, 0, CTX_LEN)
            seg_hi = jnp.where(trow < CTX_LEN, CTX_LEN, l)
            valid = (ts >= seg_lo) & (ts < seg_hi)
            acc = acc + jnp.where(valid, shifted, 0.0) * w[j:j + 1]
        return acc * _sigmoid(acc)

    def l2norm(x):
        return x * lax.rsqrt(jnp.sum(x * x, axis=-1, keepdims=True) + EPS)

    qs[...] = l2norm(conv_silu(q_ref, wq_ref)) * (DN_D ** -0.5)
    ks[...] = l2norm(conv_silu(k_ref, wk_ref))
    vs[...] = conv_silu(v_ref, wv_ref)

    aux = aux_ref[0]
    alane = lax.broadcasted_iota(jnp.int32, (l, LANES), 1)
    for i in range(4):
        colv = jnp.sum(jnp.where(alane == head + i * DN_HEADS, aux, 0.0), axis=-1, keepdims=True)
        bcs[i] = jnp.broadcast_to(colv, (l, LANES))

    row = lax.broadcasted_iota(jnp.int32, (CHUNK, LANES), 0)
    lane = lax.broadcasted_iota(jnp.int32, (CHUNK, LANES), 1)
    fwd = lane < CHUNK
    col = jnp.where(fwd, lane, lane - CHUNK)
    sgn = jnp.where(fwd, 1, -1)
    incl = (row - col) * sgn >= 0
    strict = (row - col) * sgn > 0
    dot = functools.partial(jnp.dot, preferred_element_type=F32)
    nt = (((1,), (1,)), ((), ()))
    tn = (((0,), (0,)), ((), ()))
    acc_refs = (of_s, ob_s)

    def g_last_row(c, d):
        return bcs[2 + d, pl.ds(c * CHUNK + (CHUNK - 1 if d == 0 else 0), 1), :]

    def prep(i, carry):
        chunks = [i * PREP_UNROLL + j for j in range(PREP_UNROLL)]
        sls = [pl.ds(pl.multiple_of(c * CHUNK, CHUNK), CHUNK) for c in chunks]
        qkv = [(qs[sl, :], ks[sl, :], vs[sl, :]) for sl in sls]
        gates = [(bcs[0, sl, :], bcs[1, sl, :], bcs[2, sl, :], bcs[3, sl, :]) for sl in sls]
        a2s, qks = [], []
        for (q, k, v), (bf, bb, gcf, gcb) in zip(qkv, gates):
            b2 = jnp.where(fwd, bf, bb)
            c2 = jnp.where(fwd, gcf, gcb)
            r2 = jnp.sum(jnp.where(row == col, c2, 0.0), axis=0, keepdims=True)
            decay = jnp.where(incl, jnp.exp(jnp.where(incl, c2 - r2, 0.0)), 0.0)
            kk = jnp.concatenate([k, k], axis=0).astype(BF16)
            gram = lax.dot_general(k.astype(BF16), kk, nt, preferred_element_type=F32)
            qks.append(lax.dot_general(q.astype(BF16), kk, nt, preferred_element_type=F32) * decay)
            a2s.append(jnp.where(strict, b2 * gram * decay, 0.0))
        t2s = _unit_tri_inverse2(a2s, row, col)
        uws = []
        for (q, k, v), (bf, bb, gcf, gcb), t2 in zip(qkv, gates, t2s):
            rhs = jnp.concatenate(
                [jnp.concatenate([v * bf, k * (bf * jnp.exp(gcf))], axis=1),
                 jnp.concatenate([v * bb, k * (bb * jnp.exp(gcb))], axis=1)], axis=0)
            uws.append(dot(_block_diag2(t2.astype(BF16)), rhs.astype(BF16)))
        for c, sl, (q, k, v), g4, qk, uw in zip(chunks, sls, qkv, gates, qks, uws):
            for d in range(2):
                gc = g4[2 + d]
                uw_d = uw[d * CHUNK:(d + 1) * CHUNK].astype(BF16)
                kd = k * jnp.exp(g_last_row(c, d) - gc)
                kn = lax.dot_general(kd.astype(BF16), uw_d, tn, preferred_element_type=F32)
                qk_d = jnp.where(fwd == (d == 0), qk, 0.0).astype(BF16)
                po = dot(qk_d, jnp.concatenate([uw_d, uw_d], axis=0))
                p_mat = q * jnp.exp(gc) - po[:, DN_D:]
                kp_s[c, d] = jnp.concatenate([kn[:, DN_D:], p_mat], axis=0).astype(BF16)
                n_s[c, d] = kn[:, :DN_D]
                acc_refs[d][sl, :] = po[:, :DN_D]
        return carry

    lax.fori_loop(0, n_chunks // PREP_UNROLL, prep, 0)

    def one_dir(c, s, d):
        sl = pl.ds(pl.multiple_of(c * CHUNK, CHUNK), CHUNK)
        r = dot(kp_s[c, d], s.astype(BF16))
        acc_refs[d][sl, :] += r[DN_D:]
        return s * jnp.exp(g_last_row(c, d)) - r[:DN_D] + n_s[c, d]

    def scan(t, carry):
        s_f, s_b = carry
        cb = jnp.where(t < ctx_chunks, ctx_chunks - 1 - t, n_chunks + ctx_chunks - 1 - t)
        return one_dir(t, s_f, 0), one_dir(cb, s_b, 1)

    zero = jnp.zeros((DN_D, DN_D), F32)
    lax.fori_loop(0, n_chunks, scan, (zero, zero))

    o = of_s[...] + ob_s[...]
    o = o * lax.rsqrt(jnp.mean(o * o, axis=-1, keepdims=True) + EPS) * ng_ref[...]
    z = z_ref[0]
    o_ref[0] = (o * (z * _sigmoid(z))).astype(o_ref.dtype)


def deltanet_heads(p_main, aux, conv_w, norm_g):
    b, l, _ = p_main.shape
    n_chunks = l // CHUNK
    blk = lambda off: pl.BlockSpec((1, l, DN_D), lambda i, h, off=off: (i, 0, off + h))
    wblk = lambda off: pl.BlockSpec((SHORT_CONV, DN_D), lambda i, h, off=off: (0, off + h))
    seq = pltpu.VMEM((l, DN_D), F32)
    return pl.pallas_call(
        functools.partial(_delta_kernel, seq_len=l),
        out_shape=jax.ShapeDtypeStruct((b, l, DN_W), BF16),
        grid=(b, DN_HEADS),
        in_specs=[blk(0), blk(DN_HEADS), blk(2 * DN_HEADS), blk(3 * DN_HEADS),
                  pl.BlockSpec((1, l, LANES), lambda i, h: (i, 0, 0)),
                  wblk(0), wblk(DN_HEADS), wblk(2 * DN_HEADS),
                  pl.BlockSpec((1, DN_D), lambda i, h: (0, 0))],
        out_specs=pl.BlockSpec((1, l, DN_D), lambda i, h: (i, 0, h)),
        scratch_shapes=[seq, seq, seq,
                        pltpu.VMEM((4, l, LANES), F32),
                        pltpu.VMEM((n_chunks, 2, DN_D + CHUNK, DN_D), BF16),
                        pltpu.VMEM((n_chunks, 2, DN_D, DN_D), F32),
                        seq, seq],
        compiler_params=_cparams(("parallel", "parallel"), 48),
        name="deltanet_heads",
    )(p_main, p_main, p_main, p_main, aux, conv_w, conv_w, conv_w, norm_g.reshape(1, DN_D))


def _cf_kernel(val_lo_ref, val_hi_ref, gate_lo_ref, gate_hi_ref, w_ref, g_ref, b_ref, o_ref, cbuf, lbuf,
               *, seq_len, vertical):
    lat = seq_len - CTX_LEN
    half_lanes = LANES // 2
    low_half = lax.broadcasted_iota(jnp.int32, (seq_len, LANES), 1) < half_lanes

    def straddle(lo_ref, hi_ref):
        return jnp.where(low_half, pltpu.roll(lo_ref[0], half_lanes, 1), pltpu.roll(hi_ref[0], half_lanes, 1))

    gate = straddle(gate_lo_ref, gate_hi_ref)
    u = straddle(val_lo_ref, val_hi_ref) * _sigmoid(gate)
    w = w_ref[...]
    gain, bias = g_ref[...], b_ref[...]
    hpad = 16
    vpad = CF_PAD * GRID_W

    def finish(acc, row0):
        xc = acc - jnp.mean(acc, axis=-1, keepdims=True)
        y = xc * lax.rsqrt(jnp.mean(xc * xc, axis=-1, keepdims=True) + EPS) * gain + bias
        o_ref[0, pl.ds(row0, GRID_W), :] = (y * _sigmoid(y)).astype(o_ref.dtype)

    def window_conv(win, masked):
        r = lax.broadcasted_iota(jnp.int32, (GRID_W, 1), 0)
        acc = jnp.zeros((GRID_W, LANES), F32)
        for k in range(CF_KERNEL):
            off = hpad - CF_PAD + k
            piece = win[off:off + GRID_W]
            if masked:
                src = r + (k - CF_PAD)
                piece = jnp.where((src >= 0) & (src < GRID_W), piece, 0.0)
            acc = acc + piece * w[k:k + 1]
        return acc

    zeros_h = jnp.zeros((hpad, LANES), F32)
    cbuf[0:hpad, :] = zeros_h
    cbuf[hpad + CTX_LEN:, :] = zeros_h
    cbuf[hpad:hpad + CTX_LEN, :] = u[:CTX_LEN]

    def ctx_tile(g, carry):
        base = pl.multiple_of(g * GRID_W, GRID_W)
        finish(window_conv(cbuf[pl.ds(base, GRID_W + 2 * hpad), :], False), base)
        return carry

    lax.fori_loop(0, CTX_LEN // GRID_W, ctx_tile, 0, unroll=CF_TILE_UNROLL)

    if vertical:
        zeros_v = jnp.zeros((vpad, LANES), F32)
        lbuf[0:vpad, :] = zeros_v
        lbuf[vpad + lat:, :] = zeros_v
        lbuf[vpad:vpad + lat, :] = u[CTX_LEN:]

        def lat_tile(g, carry):
            acc = jnp.zeros((GRID_W, LANES), F32)
            for k in range(CF_KERNEL):
                base = pl.multiple_of((g + k) * GRID_W, GRID_W)
                acc = acc + lbuf[pl.ds(base, GRID_W), :] * w[k:k + 1]
            finish(acc, pl.multiple_of(CTX_LEN + g * GRID_W, GRID_W))
            return carry
    else:
        lbuf[0:hpad, :] = zeros_h
        lbuf[hpad + lat:, :] = zeros_h
        lbuf[hpad:hpad + lat, :] = u[CTX_LEN:]

        def lat_tile(g, carry):
            base = pl.multiple_of(g * GRID_W, GRID_W)
            finish(window_conv(lbuf[pl.ds(base, GRID_W + 2 * hpad), :], True),
                   pl.multiple_of(CTX_LEN + base, GRID_W))
            return carry

    lax.fori_loop(0, lat // GRID_W, lat_tile, 0, unroll=CF_TILE_UNROLL)


def conformer_heads(p, dw_w, ln_g, ln_b, vertical):
    b, l, _ = p.shape
    half = CF_W // 2
    lat = l - CTX_LEN
    first = half // LANES if vertical else 0
    cf_off = 4 * DN_W + 4 * DN_HEADS
    assert cf_off % LANES == LANES // 2 and CF_W % LANES == 0
    val0 = cf_off // LANES + first
    gate0 = (cf_off + CF_W) // LANES + first
    lrows = lat + (2 * CF_PAD * GRID_W if vertical else 32)
    chan = lambda rows: pl.BlockSpec((rows, LANES), lambda i, j: (0, first + j))
    col = lambda c0: pl.BlockSpec((1, l, LANES), lambda i, j: (i, 0, c0 + j))
    return pl.pallas_call(
        functools.partial(_cf_kernel, seq_len=l, vertical=vertical),
        out_shape=jax.ShapeDtypeStruct((b, l, half), BF16),
        grid=(b, half // LANES),
        in_specs=[col(val0), col(val0 + 1), col(gate0), col(gate0 + 1),
                  chan(CF_KERNEL), chan(1), chan(1)],
        out_specs=pl.BlockSpec((1, l, LANES), lambda i, j: (i, 0, j)),
        scratch_shapes=[pltpu.VMEM((CTX_LEN + 32, LANES), F32), pltpu.VMEM((lrows, LANES), F32)],
        compiler_params=_cparams(("parallel", "parallel"), 32),
        name="conformer_v" if vertical else "conformer_h",
    )(p, p, p, p, dw_w, ln_g.reshape(1, CF_W), ln_b.reshape(1, CF_W))


def _out_proj_kernel(dn_ref, ch_ref, cv_ref, w0_ref, w1_ref, w2_ref, x_ref, gl_ref, gc_ref, o_ref, *, tm):
    dot = functools.partial(jnp.dot, preferred_element_type=F32)
    acc = (dot(dn_ref[0], w0_ref[...].astype(BF16)) + dot(ch_ref[0], w1_ref[...].astype(BF16))
           + dot(cv_ref[0], w2_ref[...].astype(BF16)))
    row = pl.program_id(1) * tm + lax.broadcasted_iota(jnp.int32, (tm, 1), 0)
    gate = jnp.where(row < CTX_LEN, gc_ref[0], gl_ref[0])
    o_ref[0] = x_ref[0] + gate * acc


def out_proj_residual(dn, cfh, cfv, w_out, layer, xs, mods, gate_col):
    b, l, d = xs.shape
    tm, tn = l // 2, 512
    k0, k1 = dn.shape[-1], cfh.shape[-1]
    act = lambda kk: pl.BlockSpec((1, tm, kk), lambda i, t, j: (i, t, 0))
    gate = lambda r: pl.BlockSpec((1, 1, tn), lambda i, t, j, r=r: (i if r is None else r, 0, gate_col * (d // tn) + j))
    return pl.pallas_call(
        functools.partial(_out_proj_kernel, tm=tm),
        out_shape=jax.ShapeDtypeStruct((b, l, d), F32),
        grid=(b, l // tm, d // tn),
        in_specs=[act(k0), act(k1), act(k1),
                  pl.BlockSpec((None, k0, tn), lambda i, t, j: (layer, 0, j)),
                  pl.BlockSpec((None, k1, tn), lambda i, t, j: (layer, k0 // k1, j)),
                  pl.BlockSpec((None, k1, tn), lambda i, t, j: (layer, k0 // k1 + 1, j)),
                  pl.BlockSpec((1, tm, tn), lambda i, t, j: (i, t, j)),
                  gate(None), gate(4)],
        out_specs=pl.BlockSpec((1, tm, tn), lambda i, t, j: (i, t, j)),
        compiler_params=_cparams(("parallel", "parallel", "parallel"), 56),
        name="out_proj",
    )(dn, cfh, cfv, w_out, w_out, w_out, xs, mods, mods)


def _norm_router_kernel(x_ref, g_ref, sh_ref, sc_ref, rw_ref, rb_ref, h_ref, idx_ref, wt_ref):
    x = x_ref[0]
    y = x * lax.rsqrt(jnp.mean(x * x, axis=-1, keepdims=True) + EPS) * g_ref[...]
    h = y * (1.0 + sc_ref[0]) + sh_ref[0]
    h_ref[0] = h
    logits = lax.dot_general(rw_ref[...], h, (((1,), (1,)), ((), ())),
                             preferred_element_type=F32, precision=lax.Precision.HIGHEST)
    scores = _sigmoid(logits)
    biased = scores + rb_ref[...]
    rows = [biased[e:e + 1] for e in range(N_EXPERTS)]
    srows = [scores[e:e + 1] for e in range(N_EXPERTS)]

    def pair_max(v):
        best = v[0] + v[1]
        for a in range(len(v)):
            for bb in range(a + 1, len(v)):
                if (a, bb) != (0, 1):
                    best = jnp.maximum(best, v[a] + v[bb])
        return best

    gscore = [pair_max(rows[g * 4:(g + 1) * 4]) for g in range(N_GROUPS)]
    best_g = jnp.zeros_like(gscore[0], dtype=jnp.int32)
    best_v = gscore[0]
    for g in range(1, N_GROUPS):
        better = gscore[g] > best_v
        best_g = jnp.where(better, g, best_g)
        best_v = jnp.where(better, gscore[g], best_v)

    def pick(vals):
        out = []
        for j in range(EXPERTS_PER_GROUP):
            m = vals[j]
            for g in range(1, N_GROUPS):
                m = jnp.where(best_g == g, vals[g * 4 + j], m)
            out.append(m)
        return out

    m, s = pick(rows), pick(srows)
    i1 = jnp.zeros_like(best_g)
    v1 = m[0]
    for j in range(1, 4):
        better = m[j] > v1
        i1 = jnp.where(better, j, i1)
        v1 = jnp.where(better, m[j], v1)
    i2 = jnp.full_like(best_g, -1)
    v2 = jnp.full_like(v1, -jnp.inf)
    for j in range(4):
        better = (i1 != j) & ((m[j] > v2) | (i2 < 0))
        i2 = jnp.where(better, j, i2)
        v2 = jnp.where(better, m[j], v2)

    def sel(vals, i):
        out = vals[0]
        for j in range(1, 4):
            out = jnp.where(i == j, vals[j], out)
        return out

    w1, w2 = sel(s, i1), sel(s, i2)
    tot = w1 + w2
    idx_ref[0:1, :] = best_g * EXPERTS_PER_GROUP + i1
    idx_ref[1:2, :] = best_g * EXPERTS_PER_GROUP + i2
    wt_ref[0:1, :] = w1 / tot
    wt_ref[1:2, :] = w2 / tot


def norm_router(xs, gain, mods, shift_col, router_w, router_bias):
    b, l, d = xs.shape
    nt = l // ROW_TILE
    tok = lambda i, t: (0, i * nt + t)
    return pl.pallas_call(
        _norm_router_kernel,
        out_shape=(jax.ShapeDtypeStruct((b, l, d), F32),
                   jax.ShapeDtypeStruct((2, b * l), jnp.int32),
                   jax.ShapeDtypeStruct((2, b * l), F32)),
        grid=(b, nt),
        in_specs=[pl.BlockSpec((1, ROW_TILE, d), lambda i, t: (i, t, 0)),
                  pl.BlockSpec((1, d), lambda i, t: (0, 0)),
                  pl.BlockSpec((1, 1, d), lambda i, t: (_mod_row(i, t), 0, shift_col)),
                  pl.BlockSpec((1, 1, d), lambda i, t: (_mod_row(i, t), 0, shift_col + 1)),
                  pl.BlockSpec((N_EXPERTS, d), lambda i, t: (0, 0)),
                  pl.BlockSpec((N_EXPERTS, 1), lambda i, t: (0, 0))],
        out_specs=(pl.BlockSpec((1, ROW_TILE, d), lambda i, t: (i, t, 0)),
                   pl.BlockSpec((2, ROW_TILE), tok),
                   pl.BlockSpec((2, ROW_TILE), tok)),
        compiler_params=_cparams(("parallel", "parallel"), 48),
        name="norm_router",
    )(xs, gain.reshape(1, d), mods, mods, router_w.T, router_bias.reshape(N_EXPERTS, 1))


def _moe_kernel(bexp_ref, stok_ref, sdst_ref, h_hbm, wg_hbm, wu_hbm, wd_hbm, y_hbm,
                xbuf, ybuf, wg_s, wu_s, wd_s, stage_a, stage_b, gsem, ssem, wsem, *, layer):
    i = pl.program_id(0)
    n = pl.num_programs(0)
    slot = lax.rem(i, 2)
    other = 1 - slot
    d, f = wg_s.shape

    def gather_start(block, buf):
        base = block * MOE_BLOCK
        for r in range(MOE_BLOCK):
            pltpu.make_async_copy(h_hbm.at[pl.ds(stok_ref[base + r], 1)], xbuf.at[buf, pl.ds(r, 1)],
                                  gsem.at[buf]).start()

    def gather_wait(buf):
        pltpu.make_async_copy(h_hbm.at[pl.ds(0, MOE_BLOCK)], xbuf.at[buf], gsem.at[buf]).wait()

    def scatter_start(block, buf):
        base = (block + 1) * MOE_BLOCK
        for r in range(MOE_BLOCK):
            pltpu.make_async_copy(ybuf.at[buf, pl.ds(r, 1)], y_hbm.at[pl.ds(sdst_ref[base + r], 1)],
                                  ssem.at[buf]).start()

    def scatter_wait(buf):
        pltpu.make_async_copy(ybuf.at[buf], y_hbm.at[pl.ds(0, MOE_BLOCK)], ssem.at[buf]).wait()

    def load_expert(e):
        rows_a, rows_b = stage_a.shape[1], stage_b.shape[1]
        chunks = []
        for w_hbm, w_s in ((wg_hbm, wg_s), (wu_hbm, wu_s)):
            for c in range(d // rows_a):
                rows = pl.ds(c * rows_a, rows_a)
                chunks.append((w_hbm.at[layer, e, rows], stage_a, w_s.at[rows]))
        for c in range(f // rows_b):
            rows = pl.ds(c * rows_b, rows_b)
            chunks.append((wd_hbm.at[layer, e, rows], stage_b, wd_s.at[rows]))
        copies = [pltpu.make_async_copy(src, stage.at[j % 2], wsem.at[j % 2])
                  for j, (src, stage, _) in enumerate(chunks)]
        copies[0].start()
        copies[1].start()
        for j, (_, stage, dst) in enumerate(chunks):
            copies[j].wait()
            dst[...] = stage[j % 2].astype(BF16)
            if j + 2 < len(chunks):
                copies[j + 2].start()

    @pl.when(i == 0)
    def _():
        gather_start(0, 0)
        ybuf[1] = jnp.zeros(ybuf.shape[1:], F32)

    @pl.when((i == 0) | (bexp_ref[i] != bexp_ref[jnp.maximum(i - 1, 0)]))
    def _():
        load_expert(bexp_ref[i])

    gather_wait(slot)

    @pl.when(i >= 1)
    def _():
        scatter_wait(slot)

    gather_start(jnp.minimum(i + 1, n - 1), other)
    scatter_start(i - 1, other)
    x = xbuf[slot].astype(BF16)
    dot = functools.partial(jnp.dot, preferred_element_type=F32)
    g = dot(x, wg_s[...])
    u = dot(x, wu_s[...])
    act = (g * _sigmoid(g) * u).astype(BF16)
    ybuf[slot] = dot(act, wd_s[...])

    @pl.when(i == n - 1)
    def _():
        scatter_start(i, slot)
        gather_wait(other)
        scatter_wait(other)
        scatter_wait(slot)


def moe_experts(h_tok, w_gate, w_up, w_down, layer, block_expert, slot_tok, slot_dst):
    t, d = h_tok.shape
    n_blocks = block_expert.shape[0]
    f = w_gate.shape[-1]
    hbm = pl.BlockSpec(memory_space=pl.ANY)
    grid_spec = pltpu.PrefetchScalarGridSpec(
        num_scalar_prefetch=3,
        grid=(n_blocks,),
        in_specs=[hbm, hbm, hbm, hbm],
        out_specs=hbm,
        scratch_shapes=[pltpu.VMEM((2, MOE_BLOCK, d), F32), pltpu.VMEM((2, MOE_BLOCK, d), F32),
                        pltpu.VMEM((d, f), BF16), pltpu.VMEM((d, f), BF16), pltpu.VMEM((f, d), BF16),
                        pltpu.VMEM((2, W_STAGE_ELEMS // f, f), F32), pltpu.VMEM((2, W_STAGE_ELEMS // d, d), F32),
                        pltpu.SemaphoreType.DMA((2,)), pltpu.SemaphoreType.DMA((2,)),
                        pltpu.SemaphoreType.DMA((2,))],
    )
    return pl.pallas_call(
        functools.partial(_moe_kernel, layer=layer),
        out_shape=jax.ShapeDtypeStruct((2 * t + MOE_DUMP_ROWS, d), F32),
        grid_spec=grid_spec,
        compiler_params=_cparams(("arbitrary",), 56),
        name="moe_experts",
    )(block_expert, slot_tok, slot_dst, h_tok, w_gate, w_up, w_down)


def route_slots(idx):
    t = idx.shape[1]
    n = 2 * t
    flat_e = idx.reshape(-1)
    onehot = (flat_e[:, None] == jnp.arange(N_EXPERTS)[None, :]).astype(jnp.int32)
    csum = jnp.cumsum(onehot, axis=0)
    rank = jnp.take_along_axis(csum, flat_e[:, None], axis=1)[:, 0] - 1
    counts = csum[-1]
    padded = (counts + MOE_BLOCK - 1) // MOE_BLOCK * MOE_BLOCK
    pad_end = jnp.cumsum(padded)
    dest = (pad_end - padded)[flat_e] + rank
    n_blocks = n // MOE_BLOCK + N_EXPERTS
    cap = n_blocks * MOE_BLOCK
    slot_src = jnp.full((cap,), -1, jnp.int32).at[dest].set(jnp.arange(n, dtype=jnp.int32))
    starts = jnp.arange(n_blocks, dtype=jnp.int32) * MOE_BLOCK
    last_start = jnp.maximum(pad_end[-1] - MOE_BLOCK, 0)
    block_expert = jnp.minimum(jnp.searchsorted(pad_end, jnp.minimum(starts, last_start), side='right'),
                               N_EXPERTS - 1).astype(jnp.int32)
    slot_tok = jnp.where(slot_src >= 0, slot_src % t, 0).astype(jnp.int32)
    pos = jnp.arange(-MOE_BLOCK, cap, dtype=jnp.int32)
    src_ext = jnp.concatenate([jnp.full((MOE_BLOCK,), -1, jnp.int32), slot_src])
    slot_dst = jnp.where(src_ext >= 0, src_ext, n + pos % MOE_DUMP_ROWS).astype(jnp.int32)
    return block_expert, slot_tok, slot_dst


def _combine_kernel(x_ref, y0_ref, y1_ref, w_ref, g_ref, fg_ref, o_ref, *, final):
    w = w_ref[...]
    x = x_ref[0] + g_ref[0] * (y0_ref[...] * w[:, 0:1] + y1_ref[...] * w[:, 1:2])
    if final:
        x = x * lax.rsqrt(jnp.mean(x * x, axis=-1, keepdims=True) + EPS) * fg_ref[...]
    o_ref[0] = x


def moe_combine(xs, yk, wts, mods, gate_col, final_g, final):
    b, l, d = xs.shape
    nt = l // ROW_TILE
    t_all = b * l
    t0 = 1 if final else 0
    n_out = nt - t0
    return pl.pallas_call(
        functools.partial(_combine_kernel, final=final),
        out_shape=jax.ShapeDtypeStruct((b, n_out * ROW_TILE, d), F32),
        grid=(b, n_out),
        in_specs=[pl.BlockSpec((1, ROW_TILE, d), lambda i, t: (i, t + t0, 0)),
                  pl.BlockSpec((ROW_TILE, d), lambda i, t: (i * nt + t + t0, 0)),
                  pl.BlockSpec((ROW_TILE, d), lambda i, t: (t_all // ROW_TILE + i * nt + t + t0, 0)),
                  pl.BlockSpec((ROW_TILE, 2), lambda i, t: (i * nt + t + t0, 0)),
                  pl.BlockSpec((1, 1, d), lambda i, t: (_mod_row(i, t + t0), 0, gate_col)),
                  pl.BlockSpec((1, d), lambda i, t: (0, 0))],
        out_specs=pl.BlockSpec((1, ROW_TILE, d), lambda i, t: (i, t, 0)),
        compiler_params=_cparams(("parallel", "parallel"), 48),
        name="moe_combine",
    )(xs, yk, yk, wts, mods, final_g.reshape(1, d))


def _layer(xs, layer, mods, norm1_g, norm2_g, w_in, conv_qkv_w, a_log, dt_bias, dn_norm_g, cf_dw_w,
           cf_ln_g, cf_ln_b, w_out, router_w, router_bias, w_gate, w_up, w_down, final_g, final):
    b, l, d = xs.shape
    in_cols = w_in.shape[-1]

    h = norm_modulate(xs, norm1_g, mods, 0, BF16).reshape(b * l, d)
    p2d = matmul_nt(h, jnp.swapaxes(w_in, 1, 2), layer, 1024, 512)
    p = p2d.reshape(b, l, in_cols)
    aux = delta_gates(p2d, a_log, dt_bias).reshape(b, l, LANES)
    dn = deltanet_heads(p, aux, conv_qkv_w, dn_norm_g)
    cfh = conformer_heads(p, cf_dw_w, cf_ln_g, cf_ln_b, False)
    cfv = conformer_heads(p, cf_dw_w, cf_ln_g, cf_ln_b, True)
    xs = out_proj_residual(dn, cfh, cfv, w_out, layer, xs, mods, 2)

    h2, idx, wts = norm_router(xs, norm2_g, mods, 3, router_w, router_bias)
    block_expert, slot_tok, slot_dst = route_slots(idx)
    yk = moe_experts(h2.reshape(b * l, d), w_gate, w_up, w_down, layer, block_expert, slot_tok, slot_dst)
    return moe_combine(xs, yk, wts.T, mods, 5, final_g, final)


def kernel(x, c, ctx, c_ctx, w_ada, b_ada, norm1_g, norm2_g, w_in, conv_qkv_w, a_log, dt_bias,
           dn_norm_g, cf_dw_w, cf_ln_g, cf_ln_b, w_out, router_w, router_bias, w_gate, w_up,
           w_down, final_g):
    depth = w_ada.shape[0]
    bsz = x.shape[0]
    xs = jnp.concatenate([ctx, x], axis=1)
    c_all = jnp.concatenate([c, c_ctx[None, :], jnp.zeros((8 - bsz - 1, c.shape[1]), F32)], axis=0)
    mods = ada_mods(c_all, w_ada, b_ada)
    for i in range(depth):
        mods_i = mods[i].reshape(8, 1, -1)
        xs = _layer(xs, i, mods_i, norm1_g[i], norm2_g[i], w_in, conv_qkv_w[i], a_log[i], dt_bias[i],
                    dn_norm_g[i], cf_dw_w[i], cf_ln_g[i], cf_ln_b[i], w_out, router_w, router_bias,
                    w_gate, w_up, w_down, final_g, i == depth - 1)
    return xs
```

```python
import functools
import math

import jax
import jax.numpy as jnp
from jax import lax
from jax.experimental import pallas as pl
from jax.experimental.pallas import tpu as pltpu

F32 = jnp.float32
BF16 = jnp.bfloat16

D_MODEL = 4096
CTX_LEN = 256
GRID_W = 64
EPS = 1e-6
DN_HEADS = 16
DN_D = 128
DN_W = DN_HEADS * DN_D
SHORT_CONV = 5
CHUNK = 64
CF_W = 2048
CF_KERNEL = 31
CF_PAD = (CF_KERNEL - 1) // 2
N_EXPERTS = 16
N_GROUPS = 4
EXPERTS_PER_GROUP = 4
D_FF = 1024
MOE_BLOCK = 128
MOE_DUMP_ROWS = 2 * MOE_BLOCK
W_STAGE_ELEMS = 512 * 1024
LANES = 128
ROW_TILE = 256
PREP_UNROLL = 12
CF_TILE_UNROLL = 4
MIB = 1024 * 1024


def _cparams(sem, vmem_mib):
    return pltpu.CompilerParams(dimension_semantics=sem, vmem_limit_bytes=vmem_mib * MIB)


def _sigmoid(x):
    return 1.0 / (1.0 + jnp.exp(-x))


def _ada_kernel(c_ref, w_ref, b_ref, o_ref):
    c = c_ref[...]
    act = (c * _sigmoid(c)).astype(BF16)
    o_ref[0] = jnp.dot(act, w_ref[0].astype(BF16), preferred_element_type=F32) + b_ref[0]


def ada_mods(c_all, w_ada, b_ada):
    depth, d, n = w_ada.shape
    tn = 512
    return pl.pallas_call(
        _ada_kernel,
        out_shape=jax.ShapeDtypeStruct((depth, 8, n), F32),
        grid=(depth, n // tn),
        in_specs=[pl.BlockSpec((8, d), lambda i, j: (0, 0)),
                  pl.BlockSpec((1, d, tn), lambda i, j: (i, 0, j)),
                  pl.BlockSpec((1, 1, tn), lambda i, j: (i, 0, j))],
        out_specs=pl.BlockSpec((1, 8, tn), lambda i, j: (i, 0, j)),
        compiler_params=_cparams(("parallel", "parallel"), 40),
        name="ada_mods",
    )(c_all, w_ada, b_ada.reshape(depth, 1, n))


def _mod_row(b, t):
    return jnp.where(t == 0, 4, b)


def _norm_mod_kernel(x_ref, g_ref, sh_ref, sc_ref, o_ref):
    x = x_ref[0]
    y = x * lax.rsqrt(jnp.mean(x * x, axis=-1, keepdims=True) + EPS) * g_ref[...]
    o_ref[0] = (y * (1.0 + sc_ref[0]) + sh_ref[0]).astype(o_ref.dtype)


def norm_modulate(xs, gain, mods, shift_col, out_dtype):
    b, l, d = xs.shape
    return pl.pallas_call(
        _norm_mod_kernel,
        out_shape=jax.ShapeDtypeStruct((b, l, d), out_dtype),
        grid=(b, l // ROW_TILE),
        in_specs=[pl.BlockSpec((1, ROW_TILE, d), lambda i, t: (i, t, 0)),
                  pl.BlockSpec((1, d), lambda i, t: (0, 0)),
                  pl.BlockSpec((1, 1, d), lambda i, t: (_mod_row(i, t), 0, shift_col)),
                  pl.BlockSpec((1, 1, d), lambda i, t: (_mod_row(i, t), 0, shift_col + 1))],
        out_specs=pl.BlockSpec((1, ROW_TILE, d), lambda i, t: (i, t, 0)),
        compiler_params=_cparams(("parallel", "parallel"), 40),
        name="norm_modulate",
    )(xs, gain.reshape(1, d), mods, mods)


def _mm_nt_kernel(a_ref, w_ref, o_ref):
    w = w_ref[...].astype(BF16)
    o_ref[...] = lax.dot_general(a_ref[...], w, (((1,), (1,)), ((), ())), preferred_element_type=F32)


def matmul_nt(a, w_t, layer, tm, tn):
    m, k = a.shape
    n = w_t.shape[1]
    return pl.pallas_call(
        _mm_nt_kernel,
        out_shape=jax.ShapeDtypeStruct((m, n), F32),
        grid=(m // tm, pl.cdiv(n, tn)),
        in_specs=[pl.BlockSpec((tm, k), lambda i, j: (i, 0)),
                  pl.BlockSpec((None, tn, k), lambda i, j: (layer, j, 0))],
        out_specs=pl.BlockSpec((tm, tn), lambda i, j: (i, j)),
        compiler_params=_cparams(("parallel", "parallel"), 48),
        name="in_proj",
    )(a, w_t)


def _dprep_kernel(p_ref, alog_ref, dtb_ref, o_ref):
    raw = p_ref[...]
    beta = _sigmoid(raw)
    g = -jnp.exp(alog_ref[...]) * jax.nn.softplus(raw + dtb_ref[...])
    r = lax.broadcasted_iota(jnp.int32, (CHUNK, CHUNK), 0)
    c = lax.broadcasted_iota(jnp.int32, (CHUNK, CHUNK), 1)
    tri_lo = (c <= r).astype(F32)
    tri_up = (c >= r).astype(F32)
    lane = lax.broadcasted_iota(jnp.int32, (CHUNK, LANES), 1)
    for ch in range(ROW_TILE // CHUNK):
        sl = slice(ch * CHUNK, (ch + 1) * CHUNK)
        gch = g[sl]
        cum_f = jnp.dot(tri_lo, gch, preferred_element_type=F32, precision=lax.Precision.HIGHEST)
        cum_b = jnp.dot(tri_up, gch, preferred_element_type=F32, precision=lax.Precision.HIGHEST)
        gc = jnp.where(lane < 3 * DN_HEADS, cum_f, cum_b)
        o_ref[sl, :] = jnp.where(lane < 2 * DN_HEADS, beta[sl], gc)


def delta_gates(p2d, a_log, dt_bias):
    m = p2d.shape[0]
    lanes_vec = lambda v: jnp.zeros((1, LANES), F32).at[0, 2 * DN_HEADS:4 * DN_HEADS].set(v.reshape(-1))
    return pl.pallas_call(
        _dprep_kernel,
        out_shape=jax.ShapeDtypeStruct((m, LANES), F32),
        grid=(m // ROW_TILE,),
        in_specs=[pl.BlockSpec((ROW_TILE, LANES), lambda i: (i, 4 * DN_W // LANES)),
                  pl.BlockSpec((1, LANES), lambda i: (0, 0)),
                  pl.BlockSpec((1, LANES), lambda i: (0, 0))],
        out_specs=pl.BlockSpec((ROW_TILE, LANES), lambda i: (i, 0)),
        compiler_params=_cparams(("parallel",), 32),
        name="delta_gates",
    )(p2d, lanes_vec(a_log), lanes_vec(dt_bias))


def _mm_pairs(xs, ys):
    xb = [x.astype(BF16) for x in xs]
    yb = [_block_diag2(y.astype(BF16)) for y in ys]
    return [jnp.dot(x, y, preferred_element_type=F32) for x, y in zip(xb, yb)]


def _block_diag2(y):
    yy = jnp.concatenate([y, y], axis=0)
    r = lax.broadcasted_iota(jnp.int32, (2 * CHUNK, LANES), 0)
    c = lax.broadcasted_iota(jnp.int32, (2 * CHUNK, LANES), 1)
    return jnp.where((r // CHUNK) == (c // CHUNK), yy, jnp.zeros_like(yy))


def _unit_tri_inverse2(a2s, row, col):
    eye2 = (row == col).astype(F32)
    same = lambda n: (row // n) == (col // n)
    a8 = [jnp.where(same(8), a2, 0.0) for a2 in a2s]
    p = _mm_pairs(a8, a8)
    yield
    t = [eye2 - a for a in a8]
    t = [x + y for x, y in zip(t, _mm_pairs(t, p))]
    yield
    p = _mm_pairs(p, p)
    yield
    t = [x + y for x, y in zip(t, _mm_pairs(t, p))]
    yield
    for n in (16, 32, 64):
        off = same(n) & jnp.logical_not(same(n // 2))
        tc = _mm_pairs(t, [jnp.where(off, a2, 0.0) for a2 in a2s])
        yield
        t = [x - y for x, y in zip(t, _mm_pairs(tc, t))]
        yield
    return t


def _delta_kernel(q_ref, k_ref, v_ref, z_ref, aux_ref, wq_ref, wk_ref, wv_ref, ng_ref, o_ref,
                  qs, ks, vs, bcs, kp_s, n_s, of_s, ob_s, *, seq_len):
    l = seq_len
    n_chunks = l // CHUNK
    ctx_chunks = CTX_LEN // CHUNK
    head = pl.program_id(1)
    trow = lax.broadcasted_iota(jnp.int32, (l, 1), 0)

    def conv_silu(x_ref, w_ref):
        x = x_ref[0]
        w = w_ref[...]
        pad = (SHORT_CONV - 1) // 2
        acc = x * w[pad:pad + 1]
        for j in range(SHORT_CONV):
            s = j - pad
            if s == 0:
                continue
            shifted = pltpu.roll(x, (-s) % l, 0)
            ts = trow + s
            seg_lo = jnp.where(trow < CTX_LEN, 0, CTX_LEN)
            seg_hi = jnp.where(trow < CTX_LEN, CTX_LEN, l)
            valid = (ts >= seg_lo) & (ts < seg_hi)
            acc = acc + jnp.where(valid, shifted, 0.0) * w[j:j + 1]
        return acc * _sigmoid(acc)

    def l2norm(x):
        return x * lax.rsqrt(jnp.sum(x * x, axis=-1, keepdims=True) + EPS)

    qs[...] = l2norm(conv_silu(q_ref, wq_ref)) * (DN_D ** -0.5)
    ks[...] = l2norm(conv_silu(k_ref, wk_ref))
    vs[...] = conv_silu(v_ref, wv_ref)

    aux = aux_ref[0]
    alane = lax.broadcasted_iota(jnp.int32, (l, LANES), 1)
    for i in range(4):
        colv = jnp.sum(jnp.where(alane == head + i * DN_HEADS, aux, 0.0), axis=-1, keepdims=True)
        bcs[i] = jnp.broadcast_to(colv, (l, LANES))

    row = lax.broadcasted_iota(jnp.int32, (CHUNK, LANES), 0)
    lane = lax.broadcasted_iota(jnp.int32, (CHUNK, LANES), 1)
    fwd = lane < CHUNK
    col = jnp.where(fwd, lane, lane - CHUNK)
    sgn = jnp.where(fwd, 1, -1)
    incl = (row - col) * sgn >= 0
    strict = (row - col) * sgn > 0
    dot = functools.partial(jnp.dot, preferred_element_type=F32)
    nt = (((1,), (1,)), ((), ()))
    tn = (((0,), (0,)), ((), ()))
    acc_refs = (of_s, ob_s)

    def chunk_of(t, d):
        if d == 0:
            return t
        return ctx_chunks - 1 - t if t < ctx_chunks else n_chunks + ctx_chunks - 1 - t

    rows_of = lambda c: slice(c * CHUNK, (c + 1) * CHUNK)

    def g_last_row(c, d):
        r = c * CHUNK + (CHUNK - 1 if d == 0 else 0)
        return bcs[2 + d, r:r + 1, :]

    def prep(steps):
        data, a2s, qks = [], [], []
        for t in steps:
            sf, sb = rows_of(chunk_of(t, 0)), rows_of(chunk_of(t, 1))
            qkv = ((qs[sf, :], ks[sf, :], vs[sf, :]), (qs[sb, :], ks[sb, :], vs[sb, :]))
            beta = (bcs[0, sf, :], bcs[1, sb, :])
            gc = (bcs[2, sf, :], bcs[3, sb, :])
            data.append((qkv, beta, gc))
            b2 = jnp.where(fwd, beta[0], beta[1])
            c2 = jnp.where(fwd, gc[0], gc[1])
            r2 = jnp.sum(jnp.where(row == col, c2, 0.0), axis=0, keepdims=True)
            decay = jnp.where(incl, jnp.exp(jnp.where(incl, c2 - r2, 0.0)), 0.0)
            kst = jnp.concatenate([qkv[0][1], qkv[1][1]], axis=0).astype(BF16)
            qst = jnp.concatenate([qkv[0][0], qkv[1][0]], axis=0).astype(BF16)
            gram = lax.dot_general(kst, kst, nt, preferred_element_type=F32)
            qk = lax.dot_general(qst, kst, nt, preferred_element_type=F32)
            gram = jnp.where(fwd, gram[:CHUNK], gram[CHUNK:])
            qks.append(jnp.where(fwd, qk[:CHUNK], qk[CHUNK:]) * decay)
            a2s.append(jnp.where(strict, b2 * gram * decay, 0.0))
        yield
        t2s = yield from _unit_tri_inverse2(a2s, row, col)
        uws = []
        for (qkv, beta, gc), t2 in zip(data, t2s):
            rhs = jnp.concatenate(
                [jnp.concatenate([qkv[d][2] * beta[d], qkv[d][1] * (beta[d] * jnp.exp(gc[d]))], axis=1)
                 for d in range(2)], axis=0)
            uws.append(dot(_block_diag2(t2.astype(BF16)), rhs.astype(BF16)))
        yield
        for t, (qkv, beta, gc), qk, uw in zip(steps, data, qks, uws):
            for d in range(2):
                c = chunk_of(t, d)
                q, k, _ = qkv[d]
                uw_d = uw[d * CHUNK:(d + 1) * CHUNK].astype(BF16)
                kd = k * jnp.exp(g_last_row(c, d) - gc[d])
                kn = lax.dot_general(kd.astype(BF16), uw_d, tn, preferred_element_type=F32)
                qk_d = jnp.where(fwd == (d == 0), qk, 0.0).astype(BF16)
                po = dot(qk_d, jnp.concatenate([uw_d, uw_d], axis=0))
                p_mat = q * jnp.exp(gc[d]) - po[:, DN_D:]
                kp_s[t, d] = jnp.concatenate([kn[:, DN_D:], p_mat], axis=0).astype(BF16)
                n_s[t, d] = kn[:, :DN_D]
                acc_refs[d][rows_of(c), :] = po[:, :DN_D]
            yield

    state = [jnp.zeros((DN_D, DN_D), F32), jnp.zeros((DN_D, DN_D), F32)]

    def scan(steps):
        for t in steps:
            for d in range(2):
                c = chunk_of(t, d)
                r = dot(kp_s[t, d], state[d].astype(BF16))
                acc_refs[d][rows_of(c), :] += r[DN_D:]
                state[d] = state[d] * jnp.exp(g_last_row(c, d)) - r[:DN_D] + n_s[t, d]
            yield

    def interleave(gens):
        gens = list(gens)
        while gens:
            for g in list(gens):
                try:
                    next(g)
                except StopIteration:
                    gens.remove(g)

    groups = [list(range(i, i + PREP_UNROLL)) for i in range(0, n_chunks, PREP_UNROLL)]
    interleave([prep(groups[0])])
    for gi, steps in enumerate(groups):
        nxt = [prep(groups[gi + 1])] if gi + 1 < len(groups) else []
        interleave([scan(steps)] + nxt)

    o = of_s[...] + ob_s[...]
    o = o * lax.rsqrt(jnp.mean(o * o, axis=-1, keepdims=True) + EPS) * ng_ref[...]
    z = z_ref[0]
    o_ref[0] = (o * (z * _sigmoid(z))).astype(o_ref.dtype)


def deltanet_heads(p_main, aux, conv_w, norm_g):
    b, l, _ = p_main.shape
    n_chunks = l // CHUNK
    blk = lambda off: pl.BlockSpec((1, l, DN_D), lambda i, h, off=off: (i, 0, off + h))
    wblk = lambda off: pl.BlockSpec((SHORT_CONV, DN_D), lambda i, h, off=off: (0, off + h))
    seq = pltpu.VMEM((l, DN_D), F32)
    return pl.pallas_call(
        functools.partial(_delta_kernel, seq_len=l),
        out_shape=jax.ShapeDtypeStruct((b, l, DN_W), BF16),
        grid=(b, DN_HEADS),
        in_specs=[blk(0), blk(DN_HEADS), blk(2 * DN_HEADS), blk(3 * DN_HEADS),
                  pl.BlockSpec((1, l, LANES), lambda i, h: (i, 0, 0)),
                  wblk(0), wblk(DN_HEADS), wblk(2 * DN_HEADS),
                  pl.BlockSpec((1, DN_D), lambda i, h: (0, 0))],
        out_specs=pl.BlockSpec((1, l, DN_D), lambda i, h: (i, 0, h)),
        scratch_shapes=[seq, seq, seq,
                        pltpu.VMEM((4, l, LANES), F32),
                        pltpu.VMEM((n_chunks, 2, DN_D + CHUNK, DN_D), BF16),
                        pltpu.VMEM((n_chunks, 2, DN_D, DN_D), F32),
                        seq, seq],
        compiler_params=_cparams(("parallel", "parallel"), 48),
        name="deltanet_heads",
    )(p_main, p_main, p_main, p_main, aux, conv_w, conv_w, conv_w, norm_g.reshape(1, DN_D))


def _cf_kernel(val_lo_ref, val_hi_ref, gate_lo_ref, gate_hi_ref, w_ref, g_ref, b_ref, o_ref, cbuf, lbuf,
               *, seq_len, vertical):
    lat = seq_len - CTX_LEN
    half_lanes = LANES // 2
    low_half = lax.broadcasted_iota(jnp.int32, (seq_len, LANES), 1) < half_lanes

    def straddle(lo_ref, hi_ref):
        return jnp.where(low_half, pltpu.roll(lo_ref[0], half_lanes, 1), pltpu.roll(hi_ref[0], half_lanes, 1))

    gate = straddle(gate_lo_ref, gate_hi_ref)
    u = straddle(val_lo_ref, val_hi_ref) * _sigmoid(gate)
    w = w_ref[...]
    gain, bias = g_ref[...], b_ref[...]
    hpad = 16
    vpad = CF_PAD * GRID_W

    def finish(acc, row0):
        xc = acc - jnp.mean(acc, axis=-1, keepdims=True)
        y = xc * lax.rsqrt(jnp.mean(xc * xc, axis=-1, keepdims=True) + EPS) * gain + bias
        o_ref[0, pl.ds(row0, GRID_W), :] = (y * _sigmoid(y)).astype(o_ref.dtype)

    def window_conv(win, masked):
        r = lax.broadcasted_iota(jnp.int32, (GRID_W, 1), 0)
        acc = jnp.zeros((GRID_W, LANES), F32)
        for k in range(CF_KERNEL):
            off = hpad - CF_PAD + k
            piece = win[off:off + GRID_W]
            if masked:
                src = r + (k - CF_PAD)
                piece = jnp.where((src >= 0) & (src < GRID_W), piece, 0.0)
            acc = acc + piece * w[k:k + 1]
        return acc

    zeros_h = jnp.zeros((hpad, LANES), F32)
    cbuf[0:hpad, :] = zeros_h
    cbuf[hpad + CTX_LEN:, :] = zeros_h
    cbuf[hpad:hpad + CTX_LEN, :] = u[:CTX_LEN]

    def ctx_tile(g, carry):
        base = pl.multiple_of(g * GRID_W, GRID_W)
        finish(window_conv(cbuf[pl.ds(base, GRID_W + 2 * hpad), :], False), base)
        return carry

    lax.fori_loop(0, CTX_LEN // GRID_W, ctx_tile, 0, unroll=CF_TILE_UNROLL)

    if vertical:
        zeros_v = jnp.zeros((vpad, LANES), F32)
        lbuf[0:vpad, :] = zeros_v
        lbuf[vpad + lat:, :] = zeros_v
        lbuf[vpad:vpad + lat, :] = u[CTX_LEN:]

        def lat_tile(g, carry):
            acc = jnp.zeros((GRID_W, LANES), F32)
            for k in range(CF_KERNEL):
                base = pl.multiple_of((g + k) * GRID_W, GRID_W)
                acc = acc + lbuf[pl.ds(base, GRID_W), :] * w[k:k + 1]
            finish(acc, pl.multiple_of(CTX_LEN + g * GRID_W, GRID_W))
            return carry
    else:
        lbuf[0:hpad, :] = zeros_h
        lbuf[hpad + lat:, :] = zeros_h
        lbuf[hpad:hpad + lat, :] = u[CTX_LEN:]

        def lat_tile(g, carry):
            base = pl.multiple_of(g * GRID_W, GRID_W)
            finish(window_conv(lbuf[pl.ds(base, GRID_W + 2 * hpad), :], True),
                   pl.multiple_of(CTX_LEN + base, GRID_W))
            return carry

    lax.fori_loop(0, lat // GRID_W, lat_tile, 0, unroll=CF_TILE_UNROLL)


def conformer_heads(p, dw_w, ln_g, ln_b, vertical):
    b, l, _ = p.shape
    half = CF_W // 2
    lat = l - CTX_LEN
    first = half // LANES if vertical else 0
    cf_off = 4 * DN_W + 4 * DN_HEADS
    assert cf_off % LANES == LANES // 2 and CF_W % LANES == 0
    val0 = cf_off // LANES + first
    gate0 = (cf_off + CF_W) // LANES + first
    lrows = lat + (2 * CF_PAD * GRID_W if vertical else 32)
    chan = lambda rows: pl.BlockSpec((rows, LANES), lambda i, j: (0, first + j))
    col = lambda c0: pl.BlockSpec((1, l, LANES), lambda i, j: (i, 0, c0 + j))
    return pl.pallas_call(
        functools.partial(_cf_kernel, seq_len=l, vertical=vertical),
        out_shape=jax.ShapeDtypeStruct((b, l, half), BF16),
        grid=(b, half // LANES),
        in_specs=[col(val0), col(val0 + 1), col(gate0), col(gate0 + 1),
                  chan(CF_KERNEL), chan(1), chan(1)],
        out_specs=pl.BlockSpec((1, l, LANES), lambda i, j: (i, 0, j)),
        scratch_shapes=[pltpu.VMEM((CTX_LEN + 32, LANES), F32), pltpu.VMEM((lrows, LANES), F32)],
        compiler_params=_cparams(("parallel", "parallel"), 32),
        name="conformer_v" if vertical else "conformer_h",
    )(p, p, p, p, dw_w, ln_g.reshape(1, CF_W), ln_b.reshape(1, CF_W))


def _out_proj_kernel(dn_ref, ch_ref, cv_ref, w0_ref, w1_ref, w2_ref, x_ref, gl_ref, gc_ref, o_ref, *, tm):
    dot = functools.partial(jnp.dot, preferred_element_type=F32)
    acc = (dot(dn_ref[0], w0_ref[...].astype(BF16)) + dot(ch_ref[0], w1_ref[...].astype(BF16))
           + dot(cv_ref[0], w2_ref[...].astype(BF16)))
    row = pl.program_id(1) * tm + lax.broadcasted_iota(jnp.int32, (tm, 1), 0)
    gate = jnp.where(row < CTX_LEN, gc_ref[0], gl_ref[0])
    o_ref[0] = x_ref[0] + gate * acc


def out_proj_residual(dn, cfh, cfv, w_out, layer, xs, mods, gate_col):
    b, l, d = xs.shape
    tm, tn = l // 2, 512
    k0, k1 = dn.shape[-1], cfh.shape[-1]
    act = lambda kk: pl.BlockSpec((1, tm, kk), lambda i, t, j: (i, t, 0))
    gate = lambda r: pl.BlockSpec((1, 1, tn), lambda i, t, j, r=r: (i if r is None else r, 0, gate_col * (d // tn) + j))
    return pl.pallas_call(
        functools.partial(_out_proj_kernel, tm=tm),
        out_shape=jax.ShapeDtypeStruct((b, l, d), F32),
        grid=(b, l // tm, d // tn),
        in_specs=[act(k0), act(k1), act(k1),
                  pl.BlockSpec((None, k0, tn), lambda i, t, j: (layer, 0, j)),
                  pl.BlockSpec((None, k1, tn), lambda i, t, j: (layer, k0 // k1, j)),
                  pl.BlockSpec((None, k1, tn), lambda i, t, j: (layer, k0 // k1 + 1, j)),
                  pl.BlockSpec((1, tm, tn), lambda i, t, j: (i, t, j)),
                  gate(None), gate(4)],
        out_specs=pl.BlockSpec((1, tm, tn), lambda i, t, j: (i, t, j)),
        compiler_params=_cparams(("parallel", "parallel", "parallel"), 56),
        name="out_proj",
    )(dn, cfh, cfv, w_out, w_out, w_out, xs, mods, mods)


def _norm_router_kernel(x_ref, g_ref, sh_ref, sc_ref, rw_ref, rb_ref, h_ref, idx_ref, wt_ref):
    x = x_ref[0]
    y = x * lax.rsqrt(jnp.mean(x * x, axis=-1, keepdims=True) + EPS) * g_ref[...]
    h = y * (1.0 + sc_ref[0]) + sh_ref[0]
    h_ref[0] = h
    logits = lax.dot_general(rw_ref[...], h, (((1,), (1,)), ((), ())),
                             preferred_element_type=F32, precision=lax.Precision.HIGHEST)
    scores = _sigmoid(logits)
    biased = scores + rb_ref[...]
    rows = [biased[e:e + 1] for e in range(N_EXPERTS)]
    srows = [scores[e:e + 1] for e in range(N_EXPERTS)]

    def pair_max(v):
        best = v[0] + v[1]
        for a in range(len(v)):
            for bb in range(a + 1, len(v)):
                if (a, bb) != (0, 1):
                    best = jnp.maximum(best, v[a] + v[bb])
        return best

    gscore = [pair_max(rows[g * 4:(g + 1) * 4]) for g in range(N_GROUPS)]
    best_g = jnp.zeros_like(gscore[0], dtype=jnp.int32)
    best_v = gscore[0]
    for g in range(1, N_GROUPS):
        better = gscore[g] > best_v
        best_g = jnp.where(better, g, best_g)
        best_v = jnp.where(better, gscore[g], best_v)

    def pick(vals):
        out = []
        for j in range(EXPERTS_PER_GROUP):
            m = vals[j]
            for g in range(1, N_GROUPS):
                m = jnp.where(best_g == g, vals[g * 4 + j], m)
            out.append(m)
        return out

    m, s = pick(rows), pick(srows)
    i1 = jnp.zeros_like(best_g)
    v1 = m[0]
    for j in range(1, 4):
        better = m[j] > v1
        i1 = jnp.where(better, j, i1)
        v1 = jnp.where(better, m[j], v1)
    i2 = jnp.full_like(best_g, -1)
    v2 = jnp.full_like(v1, -jnp.inf)
    for j in range(4):
        better = (i1 != j) & ((m[j] > v2) | (i2 < 0))
        i2 = jnp.where(better, j, i2)
        v2 = jnp.where(better, m[j], v2)

    def sel(vals, i):
        out = vals[0]
        for j in range(1, 4):
            out = jnp.where(i == j, vals[j], out)
        return out

    w1, w2 = sel(s, i1), sel(s, i2)
    tot = w1 + w2
    idx_ref[0:1, :] = best_g * EXPERTS_PER_GROUP + i1
    idx_ref[1:2, :] = best_g * EXPERTS_PER_GROUP + i2
    wt_ref[0:1, :] = w1 / tot
    wt_ref[1:2, :] = w2 / tot


def norm_router(xs, gain, mods, shift_col, router_w, router_bias):
    b, l, d = xs.shape
    nt = l // ROW_TILE
    tok = lambda i, t: (0, i * nt + t)
    return pl.pallas_call(
        _norm_router_kernel,
        out_shape=(jax.ShapeDtypeStruct((b, l, d), F32),
                   jax.ShapeDtypeStruct((2, b * l), jnp.int32),
                   jax.ShapeDtypeStruct((2, b * l), F32)),
        grid=(b, nt),
        in_specs=[pl.BlockSpec((1, ROW_TILE, d), lambda i, t: (i, t, 0)),
                  pl.BlockSpec((1, d), lambda i, t: (0, 0)),
                  pl.BlockSpec((1, 1, d), lambda i, t: (_mod_row(i, t), 0, shift_col)),
                  pl.BlockSpec((1, 1, d), lambda i, t: (_mod_row(i, t), 0, shift_col + 1)),
                  pl.BlockSpec((N_EXPERTS, d), lambda i, t: (0, 0)),
                  pl.BlockSpec((N_EXPERTS, 1), lambda i, t: (0, 0))],
        out_specs=(pl.BlockSpec((1, ROW_TILE, d), lambda i, t: (i, t, 0)),
                   pl.BlockSpec((2, ROW_TILE), tok),
                   pl.BlockSpec((2, ROW_TILE), tok)),
        compiler_params=_cparams(("parallel", "parallel"), 48),
        name="norm_router",
    )(xs, gain.reshape(1, d), mods, mods, router_w.T, router_bias.reshape(N_EXPERTS, 1))


def _moe_kernel(bexp_ref, stok_ref, sdst_ref, h_hbm, wg_hbm, wu_hbm, wd_hbm, y_hbm,
                xbuf, ybuf, wg_s, wu_s, wd_s, stage_a, stage_b, gsem, ssem, wsem, *, layer):
    i = pl.program_id(0)
    n = pl.num_programs(0)
    slot = lax.rem(i, 2)
    other = 1 - slot
    d, f = wg_s.shape

    def gather_start(block, buf):
        base = block * MOE_BLOCK
        for r in range(MOE_BLOCK):
            pltpu.make_async_copy(h_hbm.at[pl.ds(stok_ref[base + r], 1)], xbuf.at[buf, pl.ds(r, 1)],
                                  gsem.at[buf]).start()

    def gather_wait(buf):
        pltpu.make_async_copy(h_hbm.at[pl.ds(0, MOE_BLOCK)], xbuf.at[buf], gsem.at[buf]).wait()

    def scatter_start(block, buf):
        base = (block + 1) * MOE_BLOCK
        for r in range(MOE_BLOCK):
            pltpu.make_async_copy(ybuf.at[buf, pl.ds(r, 1)], y_hbm.at[pl.ds(sdst_ref[base + r], 1)],
                                  ssem.at[buf]).start()

    def scatter_wait(buf):
        pltpu.make_async_copy(ybuf.at[buf], y_hbm.at[pl.ds(0, MOE_BLOCK)], ssem.at[buf]).wait()

    def load_expert(e):
        rows_a, rows_b = stage_a.shape[1], stage_b.shape[1]
        chunks = []
        for w_hbm, w_s in ((wg_hbm, wg_s), (wu_hbm, wu_s)):
            for c in range(d // rows_a):
                rows = pl.ds(c * rows_a, rows_a)
                chunks.append((w_hbm.at[layer, e, rows], stage_a, w_s.at[rows]))
        for c in range(f // rows_b):
            rows = pl.ds(c * rows_b, rows_b)
            chunks.append((wd_hbm.at[layer, e, rows], stage_b, wd_s.at[rows]))
        copies = [pltpu.make_async_copy(src, stage.at[j % 2], wsem.at[j % 2])
                  for j, (src, stage, _) in enumerate(chunks)]
        copies[0].start()
        copies[1].start()
        for j, (_, stage, dst) in enumerate(chunks):
            copies[j].wait()
            dst[...] = stage[j % 2].astype(BF16)
            if j + 2 < len(chunks):
                copies[j + 2].start()

    @pl.when(i == 0)
    def _():
        gather_start(0, 0)
        ybuf[1] = jnp.zeros(ybuf.shape[1:], F32)

    @pl.when((i == 0) | (bexp_ref[i] != bexp_ref[jnp.maximum(i - 1, 0)]))
    def _():
        load_expert(bexp_ref[i])

    gather_wait(slot)

    @pl.when(i >= 1)
    def _():
        scatter_wait(slot)

    gather_start(jnp.minimum(i + 1, n - 1), other)
    scatter_start(i - 1, other)
    x = xbuf[slot].astype(BF16)
    dot = functools.partial(jnp.dot, preferred_element_type=F32)
    g = dot(x, wg_s[...])
    u = dot(x, wu_s[...])
    act = (g * _sigmoid(g) * u).astype(BF16)
    ybuf[slot] = dot(act, wd_s[...])

    @pl.when(i == n - 1)
    def _():
        scatter_start(i, slot)
        gather_wait(other)
        scatter_wait(other)
        scatter_wait(slot)


def moe_experts(h_tok, w_gate, w_up, w_down, layer, block_expert, slot_tok, slot_dst):
    t, d = h_tok.shape
    n_blocks = block_expert.shape[0]
    f = w_gate.shape[-1]
    hbm = pl.BlockSpec(memory_space=pl.ANY)
    grid_spec = pltpu.PrefetchScalarGridSpec(
        num_scalar_prefetch=3,
        grid=(n_blocks,),
        in_specs=[hbm, hbm, hbm, hbm],
        out_specs=hbm,
        scratch_shapes=[pltpu.VMEM((2, MOE_BLOCK, d), F32), pltpu.VMEM((2, MOE_BLOCK, d), F32),
                        pltpu.VMEM((d, f), BF16), pltpu.VMEM((d, f), BF16), pltpu.VMEM((f, d), BF16),
                        pltpu.VMEM((2, W_STAGE_ELEMS // f, f), F32), pltpu.VMEM((2, W_STAGE_ELEMS // d, d), F32),
                        pltpu.SemaphoreType.DMA((2,)), pltpu.SemaphoreType.DMA((2,)),
                        pltpu.SemaphoreType.DMA((2,))],
    )
    return pl.pallas_call(
        functools.partial(_moe_kernel, layer=layer),
        out_shape=jax.ShapeDtypeStruct((2 * t + MOE_DUMP_ROWS, d), F32),
        grid_spec=grid_spec,
        compiler_params=_cparams(("arbitrary",), 56),
        name="moe_experts",
    )(block_expert, slot_tok, slot_dst, h_tok, w_gate, w_up, w_down)


def route_slots(idx):
    t = idx.shape[1]
    n = 2 * t
    flat_e = idx.reshape(-1)
    onehot = (flat_e[:, None] == jnp.arange(N_EXPERTS)[None, :]).astype(jnp.int32)
    csum = jnp.cumsum(onehot, axis=0)
    rank = jnp.take_along_axis(csum, flat_e[:, None], axis=1)[:, 0] - 1
    counts = csum[-1]
    padded = (counts + MOE_BLOCK - 1) // MOE_BLOCK * MOE_BLOCK
    pad_end = jnp.cumsum(padded)
    dest = (pad_end - padded)[flat_e] + rank
    n_blocks = n // MOE_BLOCK + N_EXPERTS
    cap = n_blocks * MOE_BLOCK
    slot_src = jnp.full((cap,), -1, jnp.int32).at[dest].set(jnp.arange(n, dtype=jnp.int32))
    starts = jnp.arange(n_blocks, dtype=jnp.int32) * MOE_BLOCK
    last_start = jnp.maximum(pad_end[-1] - MOE_BLOCK, 0)
    block_expert = jnp.minimum(jnp.searchsorted(pad_end, jnp.minimum(starts, last_start), side='right'),
                               N_EXPERTS - 1).astype(jnp.int32)
    slot_tok = jnp.where(slot_src >= 0, slot_src % t, 0).astype(jnp.int32)
    pos = jnp.arange(-MOE_BLOCK, cap, dtype=jnp.int32)
    src_ext = jnp.concatenate([jnp.full((MOE_BLOCK,), -1, jnp.int32), slot_src])
    slot_dst = jnp.where(src_ext >= 0, src_ext, n + pos % MOE_DUMP_ROWS).astype(jnp.int32)
    return block_expert, slot_tok, slot_dst


def _combine_kernel(x_ref, y0_ref, y1_ref, w_ref, g_ref, fg_ref, o_ref, *, final):
    w = w_ref[...]
    x = x_ref[0] + g_ref[0] * (y0_ref[...] * w[:, 0:1] + y1_ref[...] * w[:, 1:2])
    if final:
        x = x * lax.rsqrt(jnp.mean(x * x, axis=-1, keepdims=True) + EPS) * fg_ref[...]
    o_ref[0] = x


def moe_combine(xs, yk, wts, mods, gate_col, final_g, final):
    b, l, d = xs.shape
    nt = l // ROW_TILE
    t_all = b * l
    t0 = 1 if final else 0
    n_out = nt - t0
    return pl.pallas_call(
        functools.partial(_combine_kernel, final=final),
        out_shape=jax.ShapeDtypeStruct((b, n_out * ROW_TILE, d), F32),
        grid=(b, n_out),
        in_specs=[pl.BlockSpec((1, ROW_TILE, d), lambda i, t: (i, t + t0, 0)),
                  pl.BlockSpec((ROW_TILE, d), lambda i, t: (i * nt + t + t0, 0)),
                  pl.BlockSpec((ROW_TILE, d), lambda i, t: (t_all // ROW_TILE + i * nt + t + t0, 0)),
                  pl.BlockSpec((ROW_TILE, 2), lambda i, t: (i * nt + t + t0, 0)),
                  pl.BlockSpec((1, 1, d), lambda i, t: (_mod_row(i, t + t0), 0, gate_col)),
                  pl.BlockSpec((1, d), lambda i, t: (0, 0))],
        out_specs=pl.BlockSpec((1, ROW_TILE, d), lambda i, t: (i, t, 0)),
        compiler_params=_cparams(("parallel", "parallel"), 48),
        name="moe_combine",
    )(xs, yk, yk, wts, mods, final_g.reshape(1, d))


def _layer(xs, layer, mods, norm1_g, norm2_g, w_in, conv_qkv_w, a_log, dt_bias, dn_norm_g, cf_dw_w,
           cf_ln_g, cf_ln_b, w_out, router_w, router_bias, w_gate, w_up, w_down, final_g, final):
    b, l, d = xs.shape
    in_cols = w_in.shape[-1]

    h = norm_modulate(xs, norm1_g, mods, 0, BF16).reshape(b * l, d)
    p2d = matmul_nt(h, jnp.swapaxes(w_in, 1, 2), layer, 1024, 512)
    p = p2d.reshape(b, l, in_cols)
    aux = delta_gates(p2d, a_log, dt_bias).reshape(b, l, LANES)
    dn = deltanet_heads(p, aux, conv_qkv_w, dn_norm_g)
    cfh = conformer_heads(p, cf_dw_w, cf_ln_g, cf_ln_b, False)
    cfv = conformer_heads(p, cf_dw_w, cf_ln_g, cf_ln_b, True)
    xs = out_proj_residual(dn, cfh, cfv, w_out, layer, xs, mods, 2)

    h2, idx, wts = norm_router(xs, norm2_g, mods, 3, router_w, router_bias)
    block_expert, slot_tok, slot_dst = route_slots(idx)
    yk = moe_experts(h2.reshape(b * l, d), w_gate, w_up, w_down, layer, block_expert, slot_tok, slot_dst)
    return moe_combine(xs, yk, wts.T, mods, 5, final_g, final)


def kernel(x, c, ctx, c_ctx, w_ada, b_ada, norm1_g, norm2_g, w_in, conv_qkv_w, a_log, dt_bias,
           dn_norm_g, cf_dw_w, cf_ln_g, cf_ln_b, w_out, router_w, router_bias, w_gate, w_up,
           w_down, final_g):
    depth = w_ada.shape[0]
    bsz = x.shape[0]
    xs = jnp.concatenate([ctx, x], axis=1)
    c_all = jnp.concatenate([c, c_ctx[None, :], jnp.zeros((8 - bsz - 1, c.shape[1]), F32)], axis=0)
    mods = ada_mods(c_all, w_ada, b_ada)
    for i in range(depth):
        mods_i = mods[i].reshape(8, 1, -1)
        xs = _layer(xs, i, mods_i, norm1_g[i], norm2_g[i], w_in, conv_qkv_w[i], a_log[i], dt_bias[i],
                    dn_norm_g[i], cf_dw_w[i], cf_ln_g[i], cf_ln_b[i], w_out, router_w, router_bias,
                    w_gate, w_up, w_down, final_g, i == depth - 1)
    return xs
```

```python
import functools
import math

import jax
import jax.numpy as jnp
from jax import lax
from jax.experimental import pallas as pl
from jax.experimental.pallas import tpu as pltpu

F32 = jnp.float32
BF16 = jnp.bfloat16

D_MODEL = 4096
CTX_LEN = 256
GRID_W = 64
EPS = 1e-6
DN_HEADS = 16
DN_D = 128
DN_W = DN_HEADS * DN_D
SHORT_CONV = 5
CHUNK = 64
CF_W = 2048
CF_KERNEL = 31
CF_PAD = (CF_KERNEL - 1) // 2
N_EXPERTS = 16
N_GROUPS = 4
EXPERTS_PER_GROUP = 4
D_FF = 1024
MOE_BLOCK = 128
MOE_DUMP_ROWS = 2 * MOE_BLOCK
W_STAGE_ELEMS = 512 * 1024
LANES = 128
ROW_TILE = 256
STEP_GROUPS = (12, 12, 12)
CF_TILE_UNROLL = 4
MIB = 1024 * 1024


def _cparams(sem, vmem_mib):
    return pltpu.CompilerParams(dimension_semantics=sem, vmem_limit_bytes=vmem_mib * MIB)


def _sigmoid(x):
    return 1.0 / (1.0 + jnp.exp(-x))


def _ada_kernel(c_ref, w_ref, b_ref, o_ref):
    c = c_ref[...]
    act = (c * _sigmoid(c)).astype(BF16)
    o_ref[0] = jnp.dot(act, w_ref[0].astype(BF16), preferred_element_type=F32) + b_ref[0]


def ada_mods(c_all, w_ada, b_ada):
    depth, d, n = w_ada.shape
    tn = 512
    return pl.pallas_call(
        _ada_kernel,
        out_shape=jax.ShapeDtypeStruct((depth, 8, n), F32),
        grid=(depth, n // tn),
        in_specs=[pl.BlockSpec((8, d), lambda i, j: (0, 0)),
                  pl.BlockSpec((1, d, tn), lambda i, j: (i, 0, j)),
                  pl.BlockSpec((1, 1, tn), lambda i, j: (i, 0, j))],
        out_specs=pl.BlockSpec((1, 8, tn), lambda i, j: (i, 0, j)),
        compiler_params=_cparams(("parallel", "parallel"), 40),
        name="ada_mods",
    )(c_all, w_ada, b_ada.reshape(depth, 1, n))


def _mod_row(b, t):
    return jnp.where(t == 0, 4, b)


def _norm_mod_kernel(x_ref, g_ref, sh_ref, sc_ref, o_ref):
    x = x_ref[0]
    y = x * lax.rsqrt(jnp.mean(x * x, axis=-1, keepdims=True) + EPS) * g_ref[...]
    o_ref[0] = (y * (1.0 + sc_ref[0]) + sh_ref[0]).astype(o_ref.dtype)


def norm_modulate(xs, gain, mods, shift_col, out_dtype):
    b, l, d = xs.shape
    return pl.pallas_call(
        _norm_mod_kernel,
        out_shape=jax.ShapeDtypeStruct((b, l, d), out_dtype),
        grid=(b, l // ROW_TILE),
        in_specs=[pl.BlockSpec((1, ROW_TILE, d), lambda i, t: (i, t, 0)),
                  pl.BlockSpec((1, d), lambda i, t: (0, 0)),
                  pl.BlockSpec((1, 1, d), lambda i, t: (_mod_row(i, t), 0, shift_col)),
                  pl.BlockSpec((1, 1, d), lambda i, t: (_mod_row(i, t), 0, shift_col + 1))],
        out_specs=pl.BlockSpec((1, ROW_TILE, d), lambda i, t: (i, t, 0)),
        compiler_params=_cparams(("parallel", "parallel"), 40),
        name="norm_modulate",
    )(xs, gain.reshape(1, d), mods, mods)


def _mm_nt_kernel(a_ref, w_ref, o_ref):
    w = w_ref[...].astype(BF16)
    o_ref[...] = lax.dot_general(a_ref[...], w, (((1,), (1,)), ((), ())), preferred_element_type=F32)


def matmul_nt(a, w_t, layer, tm, tn):
    m, k = a.shape
    n = w_t.shape[1]
    return pl.pallas_call(
        _mm_nt_kernel,
        out_shape=jax.ShapeDtypeStruct((m, n), F32),
        grid=(m // tm, pl.cdiv(n, tn)),
        in_specs=[pl.BlockSpec((tm, k), lambda i, j: (i, 0)),
                  pl.BlockSpec((None, tn, k), lambda i, j: (layer, j, 0))],
        out_specs=pl.BlockSpec((tm, tn), lambda i, j: (i, j)),
        compiler_params=_cparams(("parallel", "parallel"), 48),
        name="in_proj",
    )(a, w_t)


def _dprep_kernel(p_ref, alog_ref, dtb_ref, o_ref):
    raw = p_ref[...]
    beta = _sigmoid(raw)
    g = -jnp.exp(alog_ref[...]) * jax.nn.softplus(raw + dtb_ref[...])
    r = lax.broadcasted_iota(jnp.int32, (CHUNK, CHUNK), 0)
    c = lax.broadcasted_iota(jnp.int32, (CHUNK, CHUNK), 1)
    tri_lo = (c <= r).astype(F32)
    tri_up = (c >= r).astype(F32)
    lane = lax.broadcasted_iota(jnp.int32, (CHUNK, LANES), 1)
    for ch in range(ROW_TILE // CHUNK):
        sl = slice(ch * CHUNK, (ch + 1) * CHUNK)
        gch = g[sl]
        cum_f = jnp.dot(tri_lo, gch, preferred_element_type=F32, precision=lax.Precision.HIGHEST)
        cum_b = jnp.dot(tri_up, gch, preferred_element_type=F32, precision=lax.Precision.HIGHEST)
        gc = jnp.where(lane < 3 * DN_HEADS, cum_f, cum_b)
        o_ref[sl, :] = jnp.where(lane < 2 * DN_HEADS, beta[sl], gc)


def delta_gates(p2d, a_log, dt_bias):
    m = p2d.shape[0]
    lanes_vec = lambda v: jnp.zeros((1, LANES), F32).at[0, 2 * DN_HEADS:4 * DN_HEADS].set(v.reshape(-1))
    return pl.pallas_call(
        _dprep_kernel,
        out_shape=jax.ShapeDtypeStruct((m, LANES), F32),
        grid=(m // ROW_TILE,),
        in_specs=[pl.BlockSpec((ROW_TILE, LANES), lambda i: (i, 4 * DN_W // LANES)),
                  pl.BlockSpec((1, LANES), lambda i: (0, 0)),
                  pl.BlockSpec((1, LANES), lambda i: (0, 0))],
        out_specs=pl.BlockSpec((ROW_TILE, LANES), lambda i: (i, 0)),
        compiler_params=_cparams(("parallel",), 32),
        name="delta_gates",
    )(p2d, lanes_vec(a_log), lanes_vec(dt_bias))


def _mm_pairs(xs, ys):
    xb = [x.astype(BF16) for x in xs]
    yb = [_block_diag2(y.astype(BF16)) for y in ys]
    return [jnp.dot(x, y, preferred_element_type=F32) for x, y in zip(xb, yb)]


def _block_diag2(y):
    yy = jnp.concatenate([y, y], axis=0)
    r = lax.broadcasted_iota(jnp.int32, (2 * CHUNK, LANES), 0)
    c = lax.broadcasted_iota(jnp.int32, (2 * CHUNK, LANES), 1)
    return jnp.where((r // CHUNK) == (c // CHUNK), yy, jnp.zeros_like(yy))


def _unit_tri_inverse2(a2s, row, col):
    eye2 = (row == col).astype(F32)
    same = lambda n: (row // n) == (col // n)
    a8 = [jnp.where(same(8), a2, 0.0) for a2 in a2s]
    p = _mm_pairs(a8, a8)
    yield
    t = [eye2 - a for a in a8]
    t = [x + y for x, y in zip(t, _mm_pairs(t, p))]
    yield
    p = _mm_pairs(p, p)
    yield
    t = [x + y for x, y in zip(t, _mm_pairs(t, p))]
    yield
    for n in (16, 32, 64):
        off = same(n) & jnp.logical_not(same(n // 2))
        tc = _mm_pairs(t, [jnp.where(off, a2, 0.0) for a2 in a2s])
        yield
        t = [x - y for x, y in zip(t, _mm_pairs(tc, t))]
        yield
    return t


def _delta_kernel(q_ref, k_ref, v_ref, z_ref, aux_ref, wq_ref, wk_ref, wv_ref, ng_ref, o_ref,
                  qs, ks, vs, bcs, kp_s, n_s, of_s, ob_s, *, seq_len):
    l = seq_len
    n_chunks = l // CHUNK
    ctx_chunks = CTX_LEN // CHUNK
    head = pl.program_id(1)
    halo = 8
    pad = (SHORT_CONV - 1) // 2
    crow = lax.broadcasted_iota(jnp.int32, (CHUNK, 1), 0)

    def conv_silu(x_ref, w_ref, c):
        r0 = c * CHUNK
        seg_lo, seg_hi = (0, CTX_LEN) if r0 < CTX_LEN else (CTX_LEN, l)
        lo, hi = max(r0 - halo, 0), min(r0 + CHUNK + halo, l)
        win = x_ref[0, lo:hi, :]
        zeros = jnp.zeros((halo, LANES), F32)
        if lo > r0 - halo:
            win = jnp.concatenate([zeros, win], axis=0)
        if hi < r0 + CHUNK + halo:
            win = jnp.concatenate([win, zeros], axis=0)
        w = w_ref[...]
        acc = None
        for j in range(SHORT_CONV):
            s = j - pad
            if 0 <= r0 + s and r0 + s + CHUNK <= l:
                piece = x_ref[0, r0 + s:r0 + s + CHUNK, :]
            else:
                piece = win[halo + s:halo + s + CHUNK]
            if r0 + s < seg_lo or r0 + CHUNK - 1 + s >= seg_hi:
                src = crow + (r0 + s)
                piece = jnp.where((src >= seg_lo) & (src < seg_hi), piece, 0.0)
            term = piece * w[j:j + 1]
            acc = term if acc is None else acc + term
        return acc * _sigmoid(acc)

    def l2norm(x):
        return x * lax.rsqrt(jnp.sum(x * x, axis=-1, keepdims=True) + EPS)

    alane = lax.broadcasted_iota(jnp.int32, (CHUNK, LANES), 1)
    prepared = set()

    def ensure_inputs(c):
        if c in prepared:
            return
        prepared.add(c)
        sl = slice(c * CHUNK, (c + 1) * CHUNK)
        qs[sl, :] = l2norm(conv_silu(q_ref, wq_ref, c)) * (DN_D ** -0.5)
        ks[sl, :] = l2norm(conv_silu(k_ref, wk_ref, c))
        vs[sl, :] = conv_silu(v_ref, wv_ref, c)
        aux = aux_ref[0, sl, :]
        for i in range(4):
            colv = jnp.sum(jnp.where(alane == head + i * DN_HEADS, aux, 0.0), axis=-1, keepdims=True)
            bcs[i, sl, :] = jnp.broadcast_to(colv, (CHUNK, LANES))

    def finish_chunk(c):
        sl = slice(c * CHUNK, (c + 1) * CHUNK)
        o = of_s[sl, :] + ob_s[sl, :]
        o = o * lax.rsqrt(jnp.mean(o * o, axis=-1, keepdims=True) + EPS) * ng_ref[...]
        z = z_ref[0, sl, :]
        o_ref[0, sl, :] = (o * (z * _sigmoid(z))).astype(o_ref.dtype)

    row = lax.broadcasted_iota(jnp.int32, (CHUNK, LANES), 0)
    lane = lax.broadcasted_iota(jnp.int32, (CHUNK, LANES), 1)
    fwd = lane < CHUNK
    col = jnp.where(fwd, lane, lane - CHUNK)
    sgn = jnp.where(fwd, 1, -1)
    incl = (row - col) * sgn >= 0
    strict = (row - col) * sgn > 0
    dot = functools.partial(jnp.dot, preferred_element_type=F32)
    nt = (((1,), (1,)), ((), ()))
    tn = (((0,), (0,)), ((), ()))
    acc_refs = (of_s, ob_s)

    def chunk_of(t, d):
        if d == 0:
            return t
        return ctx_chunks - 1 - t if t < ctx_chunks else n_chunks + ctx_chunks - 1 - t

    rows_of = lambda c: slice(c * CHUNK, (c + 1) * CHUNK)

    def g_last_row(c, d):
        r = c * CHUNK + (CHUNK - 1 if d == 0 else 0)
        return bcs[2 + d, r:r + 1, :]

    def prep(steps):
        data, a2s, qks = [], [], []
        for t in steps:
            sf, sb = rows_of(chunk_of(t, 0)), rows_of(chunk_of(t, 1))
            qkv = ((qs[sf, :], ks[sf, :], vs[sf, :]), (qs[sb, :], ks[sb, :], vs[sb, :]))
            beta = (bcs[0, sf, :], bcs[1, sb, :])
            gc = (bcs[2, sf, :], bcs[3, sb, :])
            data.append((qkv, beta, gc))
            b2 = jnp.where(fwd, beta[0], beta[1])
            c2 = jnp.where(fwd, gc[0], gc[1])
            r2 = jnp.sum(jnp.where(row == col, c2, 0.0), axis=0, keepdims=True)
            decay = jnp.where(incl, jnp.exp(jnp.where(incl, c2 - r2, 0.0)), 0.0)
            kst = jnp.concatenate([qkv[0][1], qkv[1][1]], axis=0).astype(BF16)
            qst = jnp.concatenate([qkv[0][0], qkv[1][0]], axis=0).astype(BF16)
            gram = lax.dot_general(kst, kst, nt, preferred_element_type=F32)
            qk = lax.dot_general(qst, kst, nt, preferred_element_type=F32)
            gram = jnp.where(fwd, gram[:CHUNK], gram[CHUNK:])
            qks.append(jnp.where(fwd, qk[:CHUNK], qk[CHUNK:]) * decay)
            a2s.append(jnp.where(strict, b2 * gram * decay, 0.0))
        yield
        t2s = yield from _unit_tri_inverse2(a2s, row, col)
        uws = []
        for (qkv, beta, gc), t2 in zip(data, t2s):
            rhs = jnp.concatenate(
                [jnp.concatenate([qkv[d][2] * beta[d], qkv[d][1] * (beta[d] * jnp.exp(gc[d]))], axis=1)
                 for d in range(2)], axis=0)
            uws.append(dot(_block_diag2(t2.astype(BF16)), rhs.astype(BF16)))
        yield
        for t, (qkv, beta, gc), qk, uw in zip(steps, data, qks, uws):
            for d in range(2):
                c = chunk_of(t, d)
                q, k, _ = qkv[d]
                uw_d = uw[d * CHUNK:(d + 1) * CHUNK].astype(BF16)
                kd = k * jnp.exp(g_last_row(c, d) - gc[d])
                kn = lax.dot_general(kd.astype(BF16), uw_d, tn, preferred_element_type=F32)
                qk_d = jnp.where(fwd == (d == 0), qk, 0.0).astype(BF16)
                po = dot(qk_d, jnp.concatenate([uw_d, uw_d], axis=0))
                p_mat = q * jnp.exp(gc[d]) - po[:, DN_D:]
                kp_s[t, d] = jnp.concatenate([kn[:, DN_D:], p_mat], axis=0).astype(BF16)
                n_s[t, d] = kn[:, :DN_D]
                acc_refs[d][rows_of(c), :] = po[:, :DN_D]
            yield

    state = [jnp.zeros((DN_D, DN_D), F32), jnp.zeros((DN_D, DN_D), F32)]
    visits = {}

    def scan(steps):
        for t in steps:
            for d in range(2):
                c = chunk_of(t, d)
                r = dot(kp_s[t, d], state[d].astype(BF16))
                acc_refs[d][rows_of(c), :] += r[DN_D:]
                state[d] = state[d] * jnp.exp(g_last_row(c, d)) - r[:DN_D] + n_s[t, d]
                visits[c] = visits.get(c, 0) + 1
                if visits[c] == 2:
                    finish_chunk(c)
            yield

    def interleave(gens):
        gens = list(gens)
        while gens:
            for g in list(gens):
                try:
                    next(g)
                except StopIteration:
                    gens.remove(g)

    def inputs(steps):
        for t in steps:
            for d in range(2):
                if chunk_of(t, d) not in prepared:
                    ensure_inputs(chunk_of(t, d))
                    yield

    assert sum(STEP_GROUPS) == n_chunks
    bounds = [sum(STEP_GROUPS[:i]) for i in range(len(STEP_GROUPS) + 1)]
    groups = [list(range(a, b)) for a, b in zip(bounds[:-1], bounds[1:])]
    stages = (inputs, prep, scan)
    for tick in range(len(groups) + len(stages) - 1):
        interleave([stage(groups[tick - k]) for k, stage in enumerate(stages) if 0 <= tick - k < len(groups)])
    assert all(v == 2 for v in visits.values()) and len(visits) == n_chunks


def deltanet_heads(p_main, aux, conv_w, norm_g):
    b, l, _ = p_main.shape
    n_chunks = l // CHUNK
    blk = lambda off: pl.BlockSpec((1, l, DN_D), lambda i, h, off=off: (i, 0, off + h))
    wblk = lambda off: pl.BlockSpec((SHORT_CONV, DN_D), lambda i, h, off=off: (0, off + h))
    seq = pltpu.VMEM((l, DN_D), F32)
    return pl.pallas_call(
        functools.partial(_delta_kernel, seq_len=l),
        out_shape=jax.ShapeDtypeStruct((b, l, DN_W), BF16),
        grid=(b, DN_HEADS),
        in_specs=[blk(0), blk(DN_HEADS), blk(2 * DN_HEADS), blk(3 * DN_HEADS),
                  pl.BlockSpec((1, l, LANES), lambda i, h: (i, 0, 0)),
                  wblk(0), wblk(DN_HEADS), wblk(2 * DN_HEADS),
                  pl.BlockSpec((1, DN_D), lambda i, h: (0, 0))],
        out_specs=pl.BlockSpec((1, l, DN_D), lambda i, h: (i, 0, h)),
        scratch_shapes=[seq, seq, seq,
                        pltpu.VMEM((4, l, LANES), F32),
                        pltpu.VMEM((n_chunks, 2, DN_D + CHUNK, DN_D), BF16),
                        pltpu.VMEM((n_chunks, 2, DN_D, DN_D), F32),
                        seq, seq],
        compiler_params=_cparams(("parallel", "parallel"), 48),
        name="deltanet_heads",
    )(p_main, p_main, p_main, p_main, aux, conv_w, conv_w, conv_w, norm_g.reshape(1, DN_D))


def _cf_kernel(val_lo_ref, val_hi_ref, gate_lo_ref, gate_hi_ref, w_ref, g_ref, b_ref, o_ref, cbuf, lbuf,
               *, seq_len, vertical):
    lat = seq_len - CTX_LEN
    half_lanes = LANES // 2
    low_half = lax.broadcasted_iota(jnp.int32, (seq_len, LANES), 1) < half_lanes

    def straddle(lo_ref, hi_ref):
        return jnp.where(low_half, pltpu.roll(lo_ref[0], half_lanes, 1), pltpu.roll(hi_ref[0], half_lanes, 1))

    gate = straddle(gate_lo_ref, gate_hi_ref)
    u = straddle(val_lo_ref, val_hi_ref) * _sigmoid(gate)
    w = w_ref[...]
    gain, bias = g_ref[...], b_ref[...]
    hpad = 16
    vpad = CF_PAD * GRID_W

    def finish(acc, row0):
        xc = acc - jnp.mean(acc, axis=-1, keepdims=True)
        y = xc * lax.rsqrt(jnp.mean(xc * xc, axis=-1, keepdims=True) + EPS) * gain + bias
        o_ref[0, pl.ds(row0, GRID_W), :] = (y * _sigmoid(y)).astype(o_ref.dtype)

    def window_conv(win, masked):
        r = lax.broadcasted_iota(jnp.int32, (GRID_W, 1), 0)
        acc = jnp.zeros((GRID_W, LANES), F32)
        for k in range(CF_KERNEL):
            off = hpad - CF_PAD + k
            piece = win[off:off + GRID_W]
            if masked:
                src = r + (k - CF_PAD)
                piece = jnp.where((src >= 0) & (src < GRID_W), piece, 0.0)
            acc = acc + piece * w[k:k + 1]
        return acc

    zeros_h = jnp.zeros((hpad, LANES), F32)
    cbuf[0:hpad, :] = zeros_h
    cbuf[hpad + CTX_LEN:, :] = zeros_h
    cbuf[hpad:hpad + CTX_LEN, :] = u[:CTX_LEN]

    def ctx_tile(g, carry):
        base = pl.multiple_of(g * GRID_W, GRID_W)
        finish(window_conv(cbuf[pl.ds(base, GRID_W + 2 * hpad), :], False), base)
        return carry

    lax.fori_loop(0, CTX_LEN // GRID_W, ctx_tile, 0, unroll=CF_TILE_UNROLL)

    if vertical:
        zeros_v = jnp.zeros((vpad, LANES), F32)
        lbuf[0:vpad, :] = zeros_v
        lbuf[vpad + lat:, :] = zeros_v
        lbuf[vpad:vpad + lat, :] = u[CTX_LEN:]

        def lat_tile(g, carry):
            acc = jnp.zeros((GRID_W, LANES), F32)
            for k in range(CF_KERNEL):
                base = pl.multiple_of((g + k) * GRID_W, GRID_W)
                acc = acc + lbuf[pl.ds(base, GRID_W), :] * w[k:k + 1]
            finish(acc, pl.multiple_of(CTX_LEN + g * GRID_W, GRID_W))
            return carry
    else:
        lbuf[0:hpad, :] = zeros_h
        lbuf[hpad + lat:, :] = zeros_h
        lbuf[hpad:hpad + lat, :] = u[CTX_LEN:]

        def lat_tile(g, carry):
            base = pl.multiple_of(g * GRID_W, GRID_W)
            finish(window_conv(lbuf[pl.ds(base, GRID_W + 2 * hpad), :], True),
                   pl.multiple_of(CTX_LEN + base, GRID_W))
            return carry

    lax.fori_loop(0, lat // GRID_W, lat_tile, 0, unroll=CF_TILE_UNROLL)


def conformer_heads(p, dw_w, ln_g, ln_b, vertical):
    b, l, _ = p.shape
    half = CF_W // 2
    lat = l - CTX_LEN
    first = half // LANES if vertical else 0
    cf_off = 4 * DN_W + 4 * DN_HEADS
    assert cf_off % LANES == LANES // 2 and CF_W % LANES == 0
    val0 = cf_off // LANES + first
    gate0 = (cf_off + CF_W) // LANES + first
    lrows = lat + (2 * CF_PAD * GRID_W if vertical else 32)
    chan = lambda rows: pl.BlockSpec((rows, LANES), lambda i, j: (0, first + j))
    col = lambda c0: pl.BlockSpec((1, l, LANES), lambda i, j: (i, 0, c0 + j))
    return pl.pallas_call(
        functools.partial(_cf_kernel, seq_len=l, vertical=vertical),
        out_shape=jax.ShapeDtypeStruct((b, l, half), BF16),
        grid=(b, half // LANES),
        in_specs=[col(val0), col(val0 + 1), col(gate0), col(gate0 + 1),
                  chan(CF_KERNEL), chan(1), chan(1)],
        out_specs=pl.BlockSpec((1, l, LANES), lambda i, j: (i, 0, j)),
        scratch_shapes=[pltpu.VMEM((CTX_LEN + 32, LANES), F32), pltpu.VMEM((lrows, LANES), F32)],
        compiler_params=_cparams(("parallel", "parallel"), 32),
        name="conformer_v" if vertical else "conformer_h",
    )(p, p, p, p, dw_w, ln_g.reshape(1, CF_W), ln_b.reshape(1, CF_W))


def _out_proj_kernel(dn_ref, ch_ref, cv_ref, w0_ref, w1_ref, w2_ref, x_ref, gl_ref, gc_ref, o_ref, *, tm):
    dot = functools.partial(jnp.dot, preferred_element_type=F32)
    acc = (dot(dn_ref[0], w0_ref[...].astype(BF16)) + dot(ch_ref[0], w1_ref[...].astype(BF16))
           + dot(cv_ref[0], w2_ref[...].astype(BF16)))
    row = pl.program_id(1) * tm + lax.broadcasted_iota(jnp.int32, (tm, 1), 0)
    gate = jnp.where(row < CTX_LEN, gc_ref[0], gl_ref[0])
    o_ref[0] = x_ref[0] + gate * acc


def out_proj_residual(dn, cfh, cfv, w_out, layer, xs, mods, gate_col):
    b, l, d = xs.shape
    tm, tn = l // 2, 512
    k0, k1 = dn.shape[-1], cfh.shape[-1]
    act = lambda kk: pl.BlockSpec((1, tm, kk), lambda i, t, j: (i, t, 0))
    gate = lambda r: pl.BlockSpec((1, 1, tn), lambda i, t, j, r=r: (i if r is None else r, 0, gate_col * (d // tn) + j))
    return pl.pallas_call(
        functools.partial(_out_proj_kernel, tm=tm),
        out_shape=jax.ShapeDtypeStruct((b, l, d), F32),
        grid=(b, l // tm, d // tn),
        in_specs=[act(k0), act(k1), act(k1),
                  pl.BlockSpec((None, k0, tn), lambda i, t, j: (layer, 0, j)),
                  pl.BlockSpec((None, k1, tn), lambda i, t, j: (layer, k0 // k1, j)),
                  pl.BlockSpec((None, k1, tn), lambda i, t, j: (layer, k0 // k1 + 1, j)),
                  pl.BlockSpec((1, tm, tn), lambda i, t, j: (i, t, j)),
                  gate(None), gate(4)],
        out_specs=pl.BlockSpec((1, tm, tn), lambda i, t, j: (i, t, j)),
        compiler_params=_cparams(("parallel", "parallel", "parallel"), 56),
        name="out_proj",
    )(dn, cfh, cfv, w_out, w_out, w_out, xs, mods, mods)


def _norm_router_kernel(x_ref, g_ref, sh_ref, sc_ref, rw_ref, rb_ref, h_ref, idx_ref, wt_ref):
    x = x_ref[0]
    y = x * lax.rsqrt(jnp.mean(x * x, axis=-1, keepdims=True) + EPS) * g_ref[...]
    h = y * (1.0 + sc_ref[0]) + sh_ref[0]
    h_ref[0] = h
    logits = lax.dot_general(rw_ref[...], h, (((1,), (1,)), ((), ())),
                             preferred_element_type=F32, precision=lax.Precision.HIGHEST)
    scores = _sigmoid(logits)
    biased = scores + rb_ref[...]
    rows = [biased[e:e + 1] for e in range(N_EXPERTS)]
    srows = [scores[e:e + 1] for e in range(N_EXPERTS)]

    def pair_max(v):
        best = v[0] + v[1]
        for a in range(len(v)):
            for bb in range(a + 1, len(v)):
                if (a, bb) != (0, 1):
                    best = jnp.maximum(best, v[a] + v[bb])
        return best

    gscore = [pair_max(rows[g * 4:(g + 1) * 4]) for g in range(N_GROUPS)]
    best_g = jnp.zeros_like(gscore[0], dtype=jnp.int32)
    best_v = gscore[0]
    for g in range(1, N_GROUPS):
        better = gscore[g] > best_v
        best_g = jnp.where(better, g, best_g)
        best_v = jnp.where(better, gscore[g], best_v)

    def pick(vals):
        out = []
        for j in range(EXPERTS_PER_GROUP):
            m = vals[j]
            for g in range(1, N_GROUPS):
                m = jnp.where(best_g == g, vals[g * 4 + j], m)
            out.append(m)
        return out

    m, s = pick(rows), pick(srows)
    i1 = jnp.zeros_like(best_g)
    v1 = m[0]
    for j in range(1, 4):
        better = m[j] > v1
        i1 = jnp.where(better, j, i1)
        v1 = jnp.where(better, m[j], v1)
    i2 = jnp.full_like(best_g, -1)
    v2 = jnp.full_like(v1, -jnp.inf)
    for j in range(4):
        better = (i1 != j) & ((m[j] > v2) | (i2 < 0))
        i2 = jnp.where(better, j, i2)
        v2 = jnp.where(better, m[j], v2)

    def sel(vals, i):
        out = vals[0]
        for j in range(1, 4):
            out = jnp.where(i == j, vals[j], out)
        return out

    w1, w2 = sel(s, i1), sel(s, i2)
    tot = w1 + w2
    idx_ref[0:1, :] = best_g * EXPERTS_PER_GROUP + i1
    idx_ref[1:2, :] = best_g * EXPERTS_PER_GROUP + i2
    wt_ref[0:1, :] = w1 / tot
    wt_ref[1:2, :] = w2 / tot


def norm_router(xs, gain, mods, shift_col, router_w, router_bias):
    b, l, d = xs.shape
    nt = l // ROW_TILE
    tok = lambda i, t: (0, i * nt + t)
    return pl.pallas_call(
        _norm_router_kernel,
        out_shape=(jax.ShapeDtypeStruct((b, l, d), F32),
                   jax.ShapeDtypeStruct((2, b * l), jnp.int32),
                   jax.ShapeDtypeStruct((2, b * l), F32)),
        grid=(b, nt),
        in_specs=[pl.BlockSpec((1, ROW_TILE, d), lambda i, t: (i, t, 0)),
                  pl.BlockSpec((1, d), lambda i, t: (0, 0)),
                  pl.BlockSpec((1, 1, d), lambda i, t: (_mod_row(i, t), 0, shift_col)),
                  pl.BlockSpec((1, 1, d), lambda i, t: (_mod_row(i, t), 0, shift_col + 1)),
                  pl.BlockSpec((N_EXPERTS, d), lambda i, t: (0, 0)),
                  pl.BlockSpec((N_EXPERTS, 1), lambda i, t: (0, 0))],
        out_specs=(pl.BlockSpec((1, ROW_TILE, d), lambda i, t: (i, t, 0)),
                   pl.BlockSpec((2, ROW_TILE), tok),
                   pl.BlockSpec((2, ROW_TILE), tok)),
        compiler_params=_cparams(("parallel", "parallel"), 48),
        name="norm_router",
    )(xs, gain.reshape(1, d), mods, mods, router_w.T, router_bias.reshape(N_EXPERTS, 1))


def _moe_kernel(bexp_ref, stok_ref, sdst_ref, h_hbm, wg_hbm, wu_hbm, wd_hbm, y_hbm,
                xbuf, ybuf, wg_s, wu_s, wd_s, stage_a, stage_b, gsem, ssem, wsem, *, layer):
    i = pl.program_id(0)
    n = pl.num_programs(0)
    slot = lax.rem(i, 2)
    other = 1 - slot
    d, f = wg_s.shape

    def gather_start(block, buf):
        base = block * MOE_BLOCK
        for r in range(MOE_BLOCK):
            pltpu.make_async_copy(h_hbm.at[pl.ds(stok_ref[base + r], 1)], xbuf.at[buf, pl.ds(r, 1)],
                                  gsem.at[buf]).start()

    def gather_wait(buf):
        pltpu.make_async_copy(h_hbm.at[pl.ds(0, MOE_BLOCK)], xbuf.at[buf], gsem.at[buf]).wait()

    def scatter_start(block, buf):
        base = (block + 1) * MOE_BLOCK
        for r in range(MOE_BLOCK):
            pltpu.make_async_copy(ybuf.at[buf, pl.ds(r, 1)], y_hbm.at[pl.ds(sdst_ref[base + r], 1)],
                                  ssem.at[buf]).start()

    def scatter_wait(buf):
        pltpu.make_async_copy(ybuf.at[buf], y_hbm.at[pl.ds(0, MOE_BLOCK)], ssem.at[buf]).wait()

    def load_expert(e):
        rows_a, rows_b = stage_a.shape[1], stage_b.shape[1]
        chunks = []
        for w_hbm, w_s in ((wg_hbm, wg_s), (wu_hbm, wu_s)):
            for c in range(d // rows_a):
                rows = pl.ds(c * rows_a, rows_a)
                chunks.append((w_hbm.at[layer, e, rows], stage_a, w_s.at[rows]))
        for c in range(f // rows_b):
            rows = pl.ds(c * rows_b, rows_b)
            chunks.append((wd_hbm.at[layer, e, rows], stage_b, wd_s.at[rows]))
        copies = [pltpu.make_async_copy(src, stage.at[j % 2], wsem.at[j % 2])
                  for j, (src, stage, _) in enumerate(chunks)]
        copies[0].start()
        copies[1].start()
        for j, (_, stage, dst) in enumerate(chunks):
            copies[j].wait()
            dst[...] = stage[j % 2].astype(BF16)
            if j + 2 < len(chunks):
                copies[j + 2].start()

    @pl.when(i == 0)
    def _():
        gather_start(0, 0)
        ybuf[1] = jnp.zeros(ybuf.shape[1:], F32)

    @pl.when((i == 0) | (bexp_ref[i] != bexp_ref[jnp.maximum(i - 1, 0)]))
    def _():
        load_expert(bexp_ref[i])

    gather_wait(slot)

    @pl.when(i >= 1)
    def _():
        scatter_wait(slot)

    gather_start(jnp.minimum(i + 1, n - 1), other)
    scatter_start(i - 1, other)
    x = xbuf[slot].astype(BF16)
    dot = functools.partial(jnp.dot, preferred_element_type=F32)
    g = dot(x, wg_s[...])
    u = dot(x, wu_s[...])
    act = (g * _sigmoid(g) * u).astype(BF16)
    ybuf[slot] = dot(act, wd_s[...])

    @pl.when(i == n - 1)
    def _():
        scatter_start(i, slot)
        gather_wait(other)
        scatter_wait(other)
        scatter_wait(slot)


def moe_experts(h_tok, w_gate, w_up, w_down, layer, n_routed, block_expert, slot_tok, slot_dst):
    t, d = h_tok.shape
    n_blocks = block_expert.shape[0]
    f = w_gate.shape[-1]
    hbm = pl.BlockSpec(memory_space=pl.ANY)
    grid_spec = pltpu.PrefetchScalarGridSpec(
        num_scalar_prefetch=3,
        grid=(n_blocks,),
        in_specs=[hbm, hbm, hbm, hbm],
        out_specs=hbm,
        scratch_shapes=[pltpu.VMEM((2, MOE_BLOCK, d), F32), pltpu.VMEM((2, MOE_BLOCK, d), F32),
                        pltpu.VMEM((d, f), BF16), pltpu.VMEM((d, f), BF16), pltpu.VMEM((f, d), BF16),
                        pltpu.VMEM((2, W_STAGE_ELEMS // f, f), F32), pltpu.VMEM((2, W_STAGE_ELEMS // d, d), F32),
                        pltpu.SemaphoreType.DMA((2,)), pltpu.SemaphoreType.DMA((2,)),
                        pltpu.SemaphoreType.DMA((2,))],
    )
    return pl.pallas_call(
        functools.partial(_moe_kernel, layer=layer),
        out_shape=jax.ShapeDtypeStruct((2 * n_routed + MOE_DUMP_ROWS, d), F32),
        grid_spec=grid_spec,
        compiler_params=_cparams(("arbitrary",), 56),
        name="moe_experts",
    )(block_expert, slot_tok, slot_dst, h_tok, w_gate, w_up, w_down)


def route_slots(idx, routed, n_routed):
    t = idx.shape[1]
    n = 2 * t
    flat_e = idx.reshape(-1)
    flat_on = jnp.concatenate([routed, routed])
    onehot = ((flat_e[:, None] == jnp.arange(N_EXPERTS)[None, :]) & flat_on[:, None]).astype(jnp.int32)
    csum = jnp.cumsum(onehot, axis=0)
    rank = jnp.take_along_axis(csum, flat_e[:, None], axis=1)[:, 0] - 1
    counts = csum[-1]
    padded = (counts + MOE_BLOCK - 1) // MOE_BLOCK * MOE_BLOCK
    pad_end = jnp.cumsum(padded)
    n_blocks = 2 * n_routed // MOE_BLOCK + N_EXPERTS
    cap = n_blocks * MOE_BLOCK
    dest = jnp.where(flat_on, (pad_end - padded)[flat_e] + rank, cap)
    slot_src = jnp.full((cap,), -1, jnp.int32).at[dest].set(jnp.arange(n, dtype=jnp.int32), mode='drop')
    starts = jnp.arange(n_blocks, dtype=jnp.int32) * MOE_BLOCK
    last_start = jnp.maximum(pad_end[-1] - MOE_BLOCK, 0)
    block_expert = jnp.minimum(jnp.searchsorted(pad_end, jnp.minimum(starts, last_start), side='right'),
                               N_EXPERTS - 1).astype(jnp.int32)
    slot_tok = jnp.where(slot_src >= 0, slot_src % t, 0).astype(jnp.int32)
    compact = jnp.cumsum(routed.astype(jnp.int32)) - 1
    out_row = (slot_src // t) * n_routed + compact[slot_tok]
    pos = jnp.arange(-MOE_BLOCK, cap, dtype=jnp.int32)
    keep = jnp.concatenate([jnp.zeros((MOE_BLOCK,), bool), slot_src >= 0])
    out_row = jnp.concatenate([jnp.zeros((MOE_BLOCK,), jnp.int32), out_row])
    slot_dst = jnp.where(keep, out_row, 2 * n_routed + pos % MOE_DUMP_ROWS).astype(jnp.int32)
    return block_expert, slot_tok, slot_dst


def _combine_kernel(x_ref, y0_ref, y1_ref, w_ref, g_ref, fg_ref, o_ref, *, final):
    w = w_ref[...]
    x = x_ref[0] + g_ref[0] * (y0_ref[...] * w[:, 0:1] + y1_ref[...] * w[:, 1:2])
    if final:
        x = x * lax.rsqrt(jnp.mean(x * x, axis=-1, keepdims=True) + EPS) * fg_ref[...]
    o_ref[0] = x


def moe_combine(xs, yk, wts, mods, gate_col, final_g, final):
    b, l, d = xs.shape
    nt = l // ROW_TILE
    t0 = 1 if final else 0
    n_out = nt - t0
    return pl.pallas_call(
        functools.partial(_combine_kernel, final=final),
        out_shape=jax.ShapeDtypeStruct((b, n_out * ROW_TILE, d), F32),
        grid=(b, n_out),
        in_specs=[pl.BlockSpec((1, ROW_TILE, d), lambda i, t: (i, t + t0, 0)),
                  pl.BlockSpec((ROW_TILE, d), lambda i, t: (i * n_out + t, 0)),
                  pl.BlockSpec((ROW_TILE, d), lambda i, t: (b * n_out + i * n_out + t, 0)),
                  pl.BlockSpec((ROW_TILE, 2), lambda i, t: (i * nt + t + t0, 0)),
                  pl.BlockSpec((1, 1, d), lambda i, t: (_mod_row(i, t + t0), 0, gate_col)),
                  pl.BlockSpec((1, d), lambda i, t: (0, 0))],
        out_specs=pl.BlockSpec((1, ROW_TILE, d), lambda i, t: (i, t, 0)),
        compiler_params=_cparams(("parallel", "parallel"), 48),
        name="moe_combine",
    )(xs, yk, yk, wts, mods, final_g.reshape(1, d))


def _layer(xs, layer, mods, norm1_g, norm2_g, w_in, conv_qkv_w, a_log, dt_bias, dn_norm_g, cf_dw_w,
           cf_ln_g, cf_ln_b, w_out, router_w, router_bias, w_gate, w_up, w_down, final_g, final):
    b, l, d = xs.shape
    in_cols = w_in.shape[-1]

    h = norm_modulate(xs, norm1_g, mods, 0, BF16).reshape(b * l, d)
    p2d = matmul_nt(h, jnp.swapaxes(w_in, 1, 2), layer, 1024, 512)
    p = p2d.reshape(b, l, in_cols)
    aux = delta_gates(p2d, a_log, dt_bias).reshape(b, l, LANES)
    dn = deltanet_heads(p, aux, conv_qkv_w, dn_norm_g)
    cfh = conformer_heads(p, cf_dw_w, cf_ln_g, cf_ln_b, False)
    cfv = conformer_heads(p, cf_dw_w, cf_ln_g, cf_ln_b, True)
    xs = out_proj_residual(dn, cfh, cfv, w_out, layer, xs, mods, 2)

    h2, idx, wts = norm_router(xs, norm2_g, mods, 3, router_w, router_bias)
    first_row = CTX_LEN if final else 0
    routed = jnp.tile(jnp.arange(l) >= first_row, b)
    n_routed = b * (l - first_row)
    block_expert, slot_tok, slot_dst = route_slots(idx, routed, n_routed)
    yk = moe_experts(h2.reshape(b * l, d), w_gate, w_up, w_down, layer, n_routed, block_expert, slot_tok,
                     slot_dst)
    return moe_combine(xs, yk, wts.T, mods, 5, final_g, final)


def kernel(x, c, ctx, c_ctx, w_ada, b_ada, norm1_g, norm2_g, w_in, conv_qkv_w, a_log, dt_bias,
           dn_norm_g, cf_dw_w, cf_ln_g, cf_ln_b, w_out, router_w, router_bias, w_gate, w_up,
           w_down, final_g):
    depth = w_ada.shape[0]
    bsz = x.shape[0]
    xs = jnp.concatenate([ctx, x], axis=1)
    c_all = jnp.concatenate([c, c_ctx[None, :], jnp.zeros((8 - bsz - 1, c.shape[1]), F32)], axis=0)
    mods = ada_mods(c_all, w_ada, b_ada)
    for i in range(depth):
        mods_i = mods[i].reshape(8, 1, -1)
        xs = _layer(xs, i, mods_i, norm1_g[i], norm2_g[i], w_in, conv_qkv_w[i], a_log[i], dt_bias[i],
                    dn_norm_g[i], cf_dw_w[i], cf_ln_g[i], cf_ln_b[i], w_out, router_w, router_bias,
                    w_gate, w_up, w_down, final_g, i == depth - 1)
    return xs
```

```python
import functools
import math

import jax
import jax.numpy as jnp
from jax import lax
from jax.experimental import pallas as pl
from jax.experimental.pallas import tpu as pltpu

F32 = jnp.float32
BF16 = jnp.bfloat16

D_MODEL = 4096
CTX_LEN = 256
GRID_W = 64
EPS = 1e-6
DN_HEADS = 16
DN_D = 128
DN_W = DN_HEADS * DN_D
SHORT_CONV = 5
CHUNK = 64
CF_W = 2048
CF_KERNEL = 31
CF_PAD = (CF_KERNEL - 1) // 2
N_EXPERTS = 16
N_GROUPS = 4
EXPERTS_PER_GROUP = 4
D_FF = 1024
MOE_BLOCK = 128
MOE_DUMP_ROWS = 2 * MOE_BLOCK
W_STAGE_ELEMS = 512 * 1024
LANES = 128
ROW_TILE = 256
STEP_GROUPS = (12, 12, 12)
CF_TILE_UNROLL = 4
MIB = 1024 * 1024


def _cparams(sem, vmem_mib):
    return pltpu.CompilerParams(dimension_semantics=sem, vmem_limit_bytes=vmem_mib * MIB)


def _sigmoid(x):
    return 1.0 / (1.0 + jnp.exp(-x))


def _ada_kernel(c_ref, w_ref, b_ref, o_ref):
    c = c_ref[...]
    act = (c * _sigmoid(c)).astype(BF16)
    o_ref[0] = jnp.dot(act, w_ref[0].astype(BF16), preferred_element_type=F32) + b_ref[0]


def ada_mods(c_all, w_ada, b_ada):
    depth, d, n = w_ada.shape
    tn = 512
    return pl.pallas_call(
        _ada_kernel,
        out_shape=jax.ShapeDtypeStruct((depth, 8, n), F32),
        grid=(depth, n // tn),
        in_specs=[pl.BlockSpec((8, d), lambda i, j: (0, 0)),
                  pl.BlockSpec((1, d, tn), lambda i, j: (i, 0, j)),
                  pl.BlockSpec((1, 1, tn), lambda i, j: (i, 0, j))],
        out_specs=pl.BlockSpec((1, 8, tn), lambda i, j: (i, 0, j)),
        compiler_params=_cparams(("parallel", "parallel"), 40),
        name="ada_mods",
    )(c_all, w_ada, b_ada.reshape(depth, 1, n))


def _mod_row(b, t):
    return jnp.where(t == 0, 4, b)


def _norm_mod_kernel(x_ref, g_ref, sh_ref, sc_ref, o_ref):
    x = x_ref[0]
    y = x * lax.rsqrt(jnp.mean(x * x, axis=-1, keepdims=True) + EPS) * g_ref[...]
    o_ref[0] = (y * (1.0 + sc_ref[0]) + sh_ref[0]).astype(o_ref.dtype)


def norm_modulate(xs, gain, mods, shift_col, out_dtype):
    b, l, d = xs.shape
    return pl.pallas_call(
        _norm_mod_kernel,
        out_shape=jax.ShapeDtypeStruct((b, l, d), out_dtype),
        grid=(b, l // ROW_TILE),
        in_specs=[pl.BlockSpec((1, ROW_TILE, d), lambda i, t: (i, t, 0)),
                  pl.BlockSpec((1, d), lambda i, t: (0, 0)),
                  pl.BlockSpec((1, 1, d), lambda i, t: (_mod_row(i, t), 0, shift_col)),
                  pl.BlockSpec((1, 1, d), lambda i, t: (_mod_row(i, t), 0, shift_col + 1))],
        out_specs=pl.BlockSpec((1, ROW_TILE, d), lambda i, t: (i, t, 0)),
        compiler_params=_cparams(("parallel", "parallel"), 40),
        name="norm_modulate",
    )(xs, gain.reshape(1, d), mods, mods)


def _mm_nt_kernel(a_ref, w_ref, o_ref):
    w = w_ref[...].astype(BF16)
    o_ref[...] = lax.dot_general(a_ref[...], w, (((1,), (1,)), ((), ())), preferred_element_type=F32)


def matmul_nt(a, w_t, layer, tm, tn):
    m, k = a.shape
    n = w_t.shape[1]
    return pl.pallas_call(
        _mm_nt_kernel,
        out_shape=jax.ShapeDtypeStruct((m, n), F32),
        grid=(m // tm, pl.cdiv(n, tn)),
        in_specs=[pl.BlockSpec((tm, k), lambda i, j: (i, 0)),
                  pl.BlockSpec((None, tn, k), lambda i, j: (layer, j, 0))],
        out_specs=pl.BlockSpec((tm, tn), lambda i, j: (i, j)),
        compiler_params=_cparams(("parallel", "parallel"), 48),
        name="in_proj",
    )(a, w_t)


def _dprep_kernel(p_ref, alog_ref, dtb_ref, o_ref):
    raw = p_ref[...]
    beta = _sigmoid(raw)
    g = -jnp.exp(alog_ref[...]) * jax.nn.softplus(raw + dtb_ref[...])
    r = lax.broadcasted_iota(jnp.int32, (CHUNK, CHUNK), 0)
    c = lax.broadcasted_iota(jnp.int32, (CHUNK, CHUNK), 1)
    tri_lo = (c <= r).astype(F32)
    tri_up = (c >= r).astype(F32)
    lane = lax.broadcasted_iota(jnp.int32, (CHUNK, LANES), 1)
    for ch in range(ROW_TILE // CHUNK):
        sl = slice(ch * CHUNK, (ch + 1) * CHUNK)
        gch = g[sl]
        cum_f = jnp.dot(tri_lo, gch, preferred_element_type=F32, precision=lax.Precision.HIGHEST)
        cum_b = jnp.dot(tri_up, gch, preferred_element_type=F32, precision=lax.Precision.HIGHEST)
        gc = jnp.where(lane < 3 * DN_HEADS, cum_f, cum_b)
        o_ref[sl, :] = jnp.where(lane < 2 * DN_HEADS, beta[sl], gc)


def delta_gates(p2d, a_log, dt_bias):
    m = p2d.shape[0]
    lanes_vec = lambda v: jnp.zeros((1, LANES), F32).at[0, 2 * DN_HEADS:4 * DN_HEADS].set(v.reshape(-1))
    return pl.pallas_call(
        _dprep_kernel,
        out_shape=jax.ShapeDtypeStruct((m, LANES), F32),
        grid=(m // ROW_TILE,),
        in_specs=[pl.BlockSpec((ROW_TILE, LANES), lambda i: (i, 4 * DN_W // LANES)),
                  pl.BlockSpec((1, LANES), lambda i: (0, 0)),
                  pl.BlockSpec((1, LANES), lambda i: (0, 0))],
        out_specs=pl.BlockSpec((ROW_TILE, LANES), lambda i: (i, 0)),
        compiler_params=_cparams(("parallel",), 32),
        name="delta_gates",
    )(p2d, lanes_vec(a_log), lanes_vec(dt_bias))


def _mm_pairs(xs, ys):
    xb = [x.astype(BF16) for x in xs]
    yb = [_block_diag2(y.astype(BF16)) for y in ys]
    return [jnp.dot(x, y, preferred_element_type=F32) for x, y in zip(xb, yb)]


def _block_diag2(y):
    yy = jnp.concatenate([y, y], axis=0)
    r = lax.broadcasted_iota(jnp.int32, (2 * CHUNK, LANES), 0)
    c = lax.broadcasted_iota(jnp.int32, (2 * CHUNK, LANES), 1)
    return jnp.where((r // CHUNK) == (c // CHUNK), yy, jnp.zeros_like(yy))


def _unit_tri_inverse2(a2s, row, col):
    eye2 = (row == col).astype(F32)
    same = lambda n: (row // n) == (col // n)
    a8 = [jnp.where(same(8), a2, 0.0) for a2 in a2s]
    p = _mm_pairs(a8, a8)
    yield
    t = [eye2 - a for a in a8]
    t = [x + y for x, y in zip(t, _mm_pairs(t, p))]
    yield
    p = _mm_pairs(p, p)
    yield
    t = [x + y for x, y in zip(t, _mm_pairs(t, p))]
    yield
    for n in (16, 32, 64):
        off = same(n) & jnp.logical_not(same(n // 2))
        tc = _mm_pairs(t, [jnp.where(off, a2, 0.0) for a2 in a2s])
        yield
        t = [x - y for x, y in zip(t, _mm_pairs(tc, t))]
        yield
    return t


def _delta_kernel(q_ref, k_ref, v_ref, z_ref, aux_ref, wq_ref, wk_ref, wv_ref, ng_ref, o_ref,
                  qs, ks, vs, bcs, kp_s, n_s, of_s, ob_s, *, seq_len):
    l = seq_len
    n_chunks = l // CHUNK
    ctx_chunks = CTX_LEN // CHUNK
    head = pl.program_id(1)
    halo = 8
    pad = (SHORT_CONV - 1) // 2
    crow = lax.broadcasted_iota(jnp.int32, (CHUNK, 1), 0)

    def conv_silu(x_ref, w_ref, c):
        r0 = c * CHUNK
        seg_lo, seg_hi = (0, CTX_LEN) if r0 < CTX_LEN else (CTX_LEN, l)
        lo, hi = max(r0 - halo, 0), min(r0 + CHUNK + halo, l)
        win = x_ref[0, lo:hi, :]
        zeros = jnp.zeros((halo, LANES), F32)
        if lo > r0 - halo:
            win = jnp.concatenate([zeros, win], axis=0)
        if hi < r0 + CHUNK + halo:
            win = jnp.concatenate([win, zeros], axis=0)
        w = w_ref[...]
        acc = None
        for j in range(SHORT_CONV):
            s = j - pad
            if 0 <= r0 + s and r0 + s + CHUNK <= l:
                piece = x_ref[0, r0 + s:r0 + s + CHUNK, :]
            else:
                piece = win[halo + s:halo + s + CHUNK]
            if r0 + s < seg_lo or r0 + CHUNK - 1 + s >= seg_hi:
                src = crow + (r0 + s)
                piece = jnp.where((src >= seg_lo) & (src < seg_hi), piece, 0.0)
            term = piece * w[j:j + 1]
            acc = term if acc is None else acc + term
        return acc * _sigmoid(acc)

    def l2norm(x):
        return x * lax.rsqrt(jnp.sum(x * x, axis=-1, keepdims=True) + EPS)

    alane = lax.broadcasted_iota(jnp.int32, (CHUNK, LANES), 1)
    prepared = set()

    def ensure_inputs(c):
        if c in prepared:
            return
        prepared.add(c)
        sl = slice(c * CHUNK, (c + 1) * CHUNK)
        qs[sl, :] = l2norm(conv_silu(q_ref, wq_ref, c)) * (DN_D ** -0.5)
        ks[sl, :] = l2norm(conv_silu(k_ref, wk_ref, c))
        vs[sl, :] = conv_silu(v_ref, wv_ref, c)
        aux = aux_ref[0, sl, :]
        for i in range(4):
            colv = jnp.sum(jnp.where(alane == head + i * DN_HEADS, aux, 0.0), axis=-1, keepdims=True)
            bcs[i, sl, :] = jnp.broadcast_to(colv, (CHUNK, LANES))

    def finish_chunk(c):
        sl = slice(c * CHUNK, (c + 1) * CHUNK)
        o = of_s[sl, :] + ob_s[sl, :]
        o = o * lax.rsqrt(jnp.mean(o * o, axis=-1, keepdims=True) + EPS) * ng_ref[...]
        z = z_ref[0, sl, :]
        o_ref[0, sl, :] = (o * (z * _sigmoid(z))).astype(o_ref.dtype)

    row = lax.broadcasted_iota(jnp.int32, (CHUNK, LANES), 0)
    lane = lax.broadcasted_iota(jnp.int32, (CHUNK, LANES), 1)
    fwd = lane < CHUNK
    col = jnp.where(fwd, lane, lane - CHUNK)
    sgn = jnp.where(fwd, 1, -1)
    incl = (row - col) * sgn >= 0
    strict = (row - col) * sgn > 0
    dot = functools.partial(jnp.dot, preferred_element_type=F32)
    nt = (((1,), (1,)), ((), ()))
    tn = (((0,), (0,)), ((), ()))
    acc_refs = (of_s, ob_s)

    def chunk_of(t, d):
        if d == 0:
            return t
        return ctx_chunks - 1 - t if t < ctx_chunks else n_chunks + ctx_chunks - 1 - t

    rows_of = lambda c: slice(c * CHUNK, (c + 1) * CHUNK)

    def g_last_row(c, d):
        r = c * CHUNK + (CHUNK - 1 if d == 0 else 0)
        return bcs[2 + d, r:r + 1, :]

    def prep(steps):
        data, a2s, qks = [], [], []
        for t in steps:
            sf, sb = rows_of(chunk_of(t, 0)), rows_of(chunk_of(t, 1))
            qkv = ((qs[sf, :], ks[sf, :], vs[sf, :]), (qs[sb, :], ks[sb, :], vs[sb, :]))
            beta = (bcs[0, sf, :], bcs[1, sb, :])
            gc = (bcs[2, sf, :], bcs[3, sb, :])
            data.append((qkv, beta, gc))
            b2 = jnp.where(fwd, beta[0], beta[1])
            c2 = jnp.where(fwd, gc[0], gc[1])
            r2 = jnp.sum(jnp.where(row == col, c2, 0.0), axis=0, keepdims=True)
            decay = jnp.where(incl, jnp.exp(jnp.where(incl, c2 - r2, 0.0)), 0.0)
            kst = jnp.concatenate([qkv[0][1], qkv[1][1]], axis=0).astype(BF16)
            qst = jnp.concatenate([qkv[0][0], qkv[1][0]], axis=0).astype(BF16)
            gram = lax.dot_general(kst, kst, nt, preferred_element_type=F32)
            qk = lax.dot_general(qst, kst, nt, preferred_element_type=F32)
            gram = jnp.where(fwd, gram[:CHUNK], gram[CHUNK:])
            qks.append(jnp.where(fwd, qk[:CHUNK], qk[CHUNK:]) * decay)
            a2s.append(jnp.where(strict, b2 * gram * decay, 0.0))
        yield
        t2s = yield from _unit_tri_inverse2(a2s, row, col)
        uws = []
        for (qkv, beta, gc), t2 in zip(data, t2s):
            rhs = jnp.concatenate(
                [jnp.concatenate([qkv[d][2] * beta[d], qkv[d][1] * (beta[d] * jnp.exp(gc[d]))], axis=1)
                 for d in range(2)], axis=0)
            uws.append(dot(_block_diag2(t2.astype(BF16)), rhs.astype(BF16)))
        yield
        for t, (qkv, beta, gc), qk, uw in zip(steps, data, qks, uws):
            for d in range(2):
                c = chunk_of(t, d)
                q, k, _ = qkv[d]
                uw_d = uw[d * CHUNK:(d + 1) * CHUNK].astype(BF16)
                kd = k * jnp.exp(g_last_row(c, d) - gc[d])
                kn = lax.dot_general(kd.astype(BF16), uw_d, tn, preferred_element_type=F32)
                qk_d = jnp.where(fwd == (d == 0), qk, 0.0).astype(BF16)
                po = dot(qk_d, jnp.concatenate([uw_d, uw_d], axis=0))
                p_mat = q * jnp.exp(gc[d]) - po[:, DN_D:]
                kp_s[t, d] = jnp.concatenate([kn[:, DN_D:], p_mat], axis=0).astype(BF16)
                n_s[t, d] = kn[:, :DN_D]
                acc_refs[d][rows_of(c), :] = po[:, :DN_D]
            yield

    state = [jnp.zeros((DN_D, DN_D), F32), jnp.zeros((DN_D, DN_D), F32)]
    visits = {}

    def scan(steps):
        for t in steps:
            for d in range(2):
                c = chunk_of(t, d)
                r = dot(kp_s[t, d], state[d].astype(BF16))
                acc_refs[d][rows_of(c), :] += r[DN_D:]
                state[d] = state[d] * jnp.exp(g_last_row(c, d)) - r[:DN_D] + n_s[t, d]
                visits[c] = visits.get(c, 0) + 1
                if visits[c] == 2:
                    finish_chunk(c)
            yield

    def interleave(gens):
        gens = list(gens)
        while gens:
            for g in list(gens):
                try:
                    next(g)
                except StopIteration:
                    gens.remove(g)

    def inputs(steps):
        for t in steps:
            for d in range(2):
                if chunk_of(t, d) not in prepared:
                    ensure_inputs(chunk_of(t, d))
                    yield

    assert sum(STEP_GROUPS) == n_chunks
    bounds = [sum(STEP_GROUPS[:i]) for i in range(len(STEP_GROUPS) + 1)]
    groups = [list(range(a, b)) for a, b in zip(bounds[:-1], bounds[1:])]
    stages = (inputs, prep, scan)
    for tick in range(len(groups) + len(stages) - 1):
        interleave([stage(groups[tick - k]) for k, stage in enumerate(stages) if 0 <= tick - k < len(groups)])
    assert all(v == 2 for v in visits.values()) and len(visits) == n_chunks


def deltanet_heads(p_main, aux, conv_w, norm_g):
    b, l, _ = p_main.shape
    n_chunks = l // CHUNK
    blk = lambda off: pl.BlockSpec((1, l, DN_D), lambda i, h, off=off: (i, 0, off + h))
    wblk = lambda off: pl.BlockSpec((SHORT_CONV, DN_D), lambda i, h, off=off: (0, off + h))
    seq = pltpu.VMEM((l, DN_D), F32)
    return pl.pallas_call(
        functools.partial(_delta_kernel, seq_len=l),
        out_shape=jax.ShapeDtypeStruct((b, l, DN_W), BF16),
        grid=(b, DN_HEADS),
        in_specs=[blk(0), blk(DN_HEADS), blk(2 * DN_HEADS), blk(3 * DN_HEADS),
                  pl.BlockSpec((1, l, LANES), lambda i, h: (i, 0, 0)),
                  wblk(0), wblk(DN_HEADS), wblk(2 * DN_HEADS),
                  pl.BlockSpec((1, DN_D), lambda i, h: (0, 0))],
        out_specs=pl.BlockSpec((1, l, DN_D), lambda i, h: (i, 0, h)),
        scratch_shapes=[seq, seq, seq,
                        pltpu.VMEM((4, l, LANES), F32),
                        pltpu.VMEM((n_chunks, 2, DN_D + CHUNK, DN_D), BF16),
                        pltpu.VMEM((n_chunks, 2, DN_D, DN_D), F32),
                        seq, seq],
        compiler_params=_cparams(("parallel", "parallel"), 48),
        name="deltanet_heads",
    )(p_main, p_main, p_main, p_main, aux, conv_w, conv_w, conv_w, norm_g.reshape(1, DN_D))


def _cf_kernel(val_lo_ref, val_hi_ref, gate_lo_ref, gate_hi_ref, w_ref, g_ref, b_ref, o_ref, cbuf, lbuf,
               *, seq_len, vertical):
    lat = seq_len - CTX_LEN
    half_lanes = LANES // 2
    low_half = lax.broadcasted_iota(jnp.int32, (seq_len, LANES), 1) < half_lanes

    def straddle(lo_ref, hi_ref):
        return jnp.where(low_half, pltpu.roll(lo_ref[0], half_lanes, 1), pltpu.roll(hi_ref[0], half_lanes, 1))

    gate = straddle(gate_lo_ref, gate_hi_ref)
    u = straddle(val_lo_ref, val_hi_ref) * _sigmoid(gate)
    w = w_ref[...]
    gain, bias = g_ref[...], b_ref[...]
    hpad = 16
    vpad = CF_PAD * GRID_W

    def finish(acc, row0):
        xc = acc - jnp.mean(acc, axis=-1, keepdims=True)
        y = xc * lax.rsqrt(jnp.mean(xc * xc, axis=-1, keepdims=True) + EPS) * gain + bias
        o_ref[0, pl.ds(row0, GRID_W), :] = (y * _sigmoid(y)).astype(o_ref.dtype)

    def window_conv(win, masked):
        r = lax.broadcasted_iota(jnp.int32, (GRID_W, 1), 0)
        acc = jnp.zeros((GRID_W, LANES), F32)
        for k in range(CF_KERNEL):
            off = hpad - CF_PAD + k
            piece = win[off:off + GRID_W]
            if masked:
                src = r + (k - CF_PAD)
                piece = jnp.where((src >= 0) & (src < GRID_W), piece, 0.0)
            acc = acc + piece * w[k:k + 1]
        return acc

    zeros_h = jnp.zeros((hpad, LANES), F32)
    cbuf[0:hpad, :] = zeros_h
    cbuf[hpad + CTX_LEN:, :] = zeros_h
    cbuf[hpad:hpad + CTX_LEN, :] = u[:CTX_LEN]

    def ctx_tile(g, carry):
        base = pl.multiple_of(g * GRID_W, GRID_W)
        finish(window_conv(cbuf[pl.ds(base, GRID_W + 2 * hpad), :], False), base)
        return carry

    lax.fori_loop(0, CTX_LEN // GRID_W, ctx_tile, 0, unroll=CF_TILE_UNROLL)

    if vertical:
        zeros_v = jnp.zeros((vpad, LANES), F32)
        lbuf[0:vpad, :] = zeros_v
        lbuf[vpad + lat:, :] = zeros_v
        lbuf[vpad:vpad + lat, :] = u[CTX_LEN:]

        def lat_tile(g, carry):
            acc = jnp.zeros((GRID_W, LANES), F32)
            for k in range(CF_KERNEL):
                base = pl.multiple_of((g + k) * GRID_W, GRID_W)
                acc = acc + lbuf[pl.ds(base, GRID_W), :] * w[k:k + 1]
            finish(acc, pl.multiple_of(CTX_LEN + g * GRID_W, GRID_W))
            return carry
    else:
        lbuf[0:hpad, :] = zeros_h
        lbuf[hpad + lat:, :] = zeros_h
        lbuf[hpad:hpad + lat, :] = u[CTX_LEN:]

        def lat_tile(g, carry):
            base = pl.multiple_of(g * GRID_W, GRID_W)
            finish(window_conv(lbuf[pl.ds(base, GRID_W + 2 * hpad), :], True),
                   pl.multiple_of(CTX_LEN + base, GRID_W))
            return carry

    lax.fori_loop(0, lat // GRID_W, lat_tile, 0, unroll=CF_TILE_UNROLL)


def conformer_heads(p, dw_w, ln_g, ln_b, vertical):
    b, l, _ = p.shape
    half = CF_W // 2
    lat = l - CTX_LEN
    first = half // LANES if vertical else 0
    cf_off = 4 * DN_W + 4 * DN_HEADS
    assert cf_off % LANES == LANES // 2 and CF_W % LANES == 0
    val0 = cf_off // LANES + first
    gate0 = (cf_off + CF_W) // LANES + first
    lrows = lat + (2 * CF_PAD * GRID_W if vertical else 32)
    chan = lambda rows: pl.BlockSpec((rows, LANES), lambda i, j: (0, first + j))
    col = lambda c0: pl.BlockSpec((1, l, LANES), lambda i, j: (i, 0, c0 + j))
    return pl.pallas_call(
        functools.partial(_cf_kernel, seq_len=l, vertical=vertical),
        out_shape=jax.ShapeDtypeStruct((b, l, half), BF16),
        grid=(b, half // LANES),
        in_specs=[col(val0), col(val0 + 1), col(gate0), col(gate0 + 1),
                  chan(CF_KERNEL), chan(1), chan(1)],
        out_specs=pl.BlockSpec((1, l, LANES), lambda i, j: (i, 0, j)),
        scratch_shapes=[pltpu.VMEM((CTX_LEN + 32, LANES), F32), pltpu.VMEM((lrows, LANES), F32)],
        compiler_params=_cparams(("parallel", "parallel"), 32),
        name="conformer_v" if vertical else "conformer_h",
    )(p, p, p, p, dw_w, ln_g.reshape(1, CF_W), ln_b.reshape(1, CF_W))


def _out_proj_kernel(dn_ref, ch_ref, cv_ref, w0_ref, w1_ref, w2_ref, x_ref, gl_ref, gc_ref, o_ref, *, tm):
    dot = functools.partial(jnp.dot, preferred_element_type=F32)
    acc = (dot(dn_ref[0], w0_ref[...].astype(BF16)) + dot(ch_ref[0], w1_ref[...].astype(BF16))
           + dot(cv_ref[0], w2_ref[...].astype(BF16)))
    row = pl.program_id(1) * tm + lax.broadcasted_iota(jnp.int32, (tm, 1), 0)
    gate = jnp.where(row < CTX_LEN, gc_ref[0], gl_ref[0])
    o_ref[0] = x_ref[0] + gate * acc


def out_proj_residual(dn, cfh, cfv, w_out, layer, xs, mods, gate_col):
    b, l, d = xs.shape
    tm, tn = l // 2, 512
    k0, k1 = dn.shape[-1], cfh.shape[-1]
    act = lambda kk: pl.BlockSpec((1, tm, kk), lambda i, t, j: (i, t, 0))
    gate = lambda r: pl.BlockSpec((1, 1, tn), lambda i, t, j, r=r: (i if r is None else r, 0, gate_col * (d // tn) + j))
    return pl.pallas_call(
        functools.partial(_out_proj_kernel, tm=tm),
        out_shape=jax.ShapeDtypeStruct((b, l, d), F32),
        grid=(b, l // tm, d // tn),
        in_specs=[act(k0), act(k1), act(k1),
                  pl.BlockSpec((None, k0, tn), lambda i, t, j: (layer, 0, j)),
                  pl.BlockSpec((None, k1, tn), lambda i, t, j: (layer, k0 // k1, j)),
                  pl.BlockSpec((None, k1, tn), lambda i, t, j: (layer, k0 // k1 + 1, j)),
                  pl.BlockSpec((1, tm, tn), lambda i, t, j: (i, t, j)),
                  gate(None), gate(4)],
        out_specs=pl.BlockSpec((1, tm, tn), lambda i, t, j: (i, t, j)),
        compiler_params=_cparams(("parallel", "parallel", "parallel"), 56),
        name="out_proj",
    )(dn, cfh, cfv, w_out, w_out, w_out, xs, mods, mods)


def _norm_router_kernel(x_ref, g_ref, sh_ref, sc_ref, rw_ref, rb_ref, h_ref, idx_ref, wt_ref):
    x = x_ref[0]
    y = x * lax.rsqrt(jnp.mean(x * x, axis=-1, keepdims=True) + EPS) * g_ref[...]
    h = y * (1.0 + sc_ref[0]) + sh_ref[0]
    h_ref[0] = h
    logits = lax.dot_general(rw_ref[...], h, (((1,), (1,)), ((), ())),
                             preferred_element_type=F32, precision=lax.Precision.HIGHEST)
    scores = _sigmoid(logits)
    biased = scores + rb_ref[...]
    rows = [biased[e:e + 1] for e in range(N_EXPERTS)]
    srows = [scores[e:e + 1] for e in range(N_EXPERTS)]

    def pair_max(v):
        best = v[0] + v[1]
        for a in range(len(v)):
            for bb in range(a + 1, len(v)):
                if (a, bb) != (0, 1):
                    best = jnp.maximum(best, v[a] + v[bb])
        return best

    gscore = [pair_max(rows[g * 4:(g + 1) * 4]) for g in range(N_GROUPS)]
    best_g = jnp.zeros_like(gscore[0], dtype=jnp.int32)
    best_v = gscore[0]
    for g in range(1, N_GROUPS):
        better = gscore[g] > best_v
        best_g = jnp.where(better, g, best_g)
        best_v = jnp.where(better, gscore[g], best_v)

    def pick(vals):
        out = []
        for j in range(EXPERTS_PER_GROUP):
            m = vals[j]
            for g in range(1, N_GROUPS):
                m = jnp.where(best_g == g, vals[g * 4 + j], m)
            out.append(m)
        return out

    m, s = pick(rows), pick(srows)
    i1 = jnp.zeros_like(best_g)
    v1 = m[0]
    for j in range(1, 4):
        better = m[j] > v1
        i1 = jnp.where(better, j, i1)
        v1 = jnp.where(better, m[j], v1)
    i2 = jnp.full_like(best_g, -1)
    v2 = jnp.full_like(v1, -jnp.inf)
    for j in range(4):
        better = (i1 != j) & ((m[j] > v2) | (i2 < 0))
        i2 = jnp.where(better, j, i2)
        v2 = jnp.where(better, m[j], v2)

    def sel(vals, i):
        out = vals[0]
        for j in range(1, 4):
            out = jnp.where(i == j, vals[j], out)
        return out

    w1, w2 = sel(s, i1), sel(s, i2)
    tot = w1 + w2
    idx_ref[0:1, :] = best_g * EXPERTS_PER_GROUP + i1
    idx_ref[1:2, :] = best_g * EXPERTS_PER_GROUP + i2
    wt_ref[0:1, :] = w1 / tot
    wt_ref[1:2, :] = w2 / tot


def norm_router(xs, gain, mods, shift_col, router_w, router_bias):
    b, l, d = xs.shape
    nt = l // ROW_TILE
    tok = lambda i, t: (0, i * nt + t)
    return pl.pallas_call(
        _norm_router_kernel,
        out_shape=(jax.ShapeDtypeStruct((b, l, d), F32),
                   jax.ShapeDtypeStruct((2, b * l), jnp.int32),
                   jax.ShapeDtypeStruct((2, b * l), F32)),
        grid=(b, nt),
        in_specs=[pl.BlockSpec((1, ROW_TILE, d), lambda i, t: (i, t, 0)),
                  pl.BlockSpec((1, d), lambda i, t: (0, 0)),
                  pl.BlockSpec((1, 1, d), lambda i, t: (_mod_row(i, t), 0, shift_col)),
                  pl.BlockSpec((1, 1, d), lambda i, t: (_mod_row(i, t), 0, shift_col + 1)),
                  pl.BlockSpec((N_EXPERTS, d), lambda i, t: (0, 0)),
                  pl.BlockSpec((N_EXPERTS, 1), lambda i, t: (0, 0))],
        out_specs=(pl.BlockSpec((1, ROW_TILE, d), lambda i, t: (i, t, 0)),
                   pl.BlockSpec((2, ROW_TILE), tok),
                   pl.BlockSpec((2, ROW_TILE), tok)),
        compiler_params=_cparams(("parallel", "parallel"), 48),
        name="norm_router",
    )(xs, gain.reshape(1, d), mods, mods, router_w.T, router_bias.reshape(N_EXPERTS, 1))


def _moe_kernel(bexp_ref, stok_ref, sdst_ref, h_hbm, wg_hbm, wu_hbm, wd_hbm, y_hbm,
                xbuf, ybuf, wg_s, wu_s, wd_s, stage_a, stage_b, gsem, ssem, wsem, *, layer):
    i = pl.program_id(0)
    n = pl.num_programs(0)
    slot = lax.rem(i, 2)
    other = 1 - slot
    d, f = wg_s.shape

    def gather_start(block, buf):
        base = block * MOE_BLOCK
        for r in range(MOE_BLOCK):
            pltpu.make_async_copy(h_hbm.at[pl.ds(stok_ref[base + r], 1)], xbuf.at[buf, pl.ds(r, 1)],
                                  gsem.at[buf]).start()

    def gather_wait(buf):
        pltpu.make_async_copy(h_hbm.at[pl.ds(0, MOE_BLOCK)], xbuf.at[buf], gsem.at[buf]).wait()

    def scatter_start(block, buf):
        base = (block + 1) * MOE_BLOCK
        for r in range(MOE_BLOCK):
            pltpu.make_async_copy(ybuf.at[buf, pl.ds(r, 1)], y_hbm.at[pl.ds(sdst_ref[base + r], 1)],
                                  ssem.at[buf]).start()

    def scatter_wait(buf):
        pltpu.make_async_copy(ybuf.at[buf], y_hbm.at[pl.ds(0, MOE_BLOCK)], ssem.at[buf]).wait()

    def load_expert(e):
        rows_a, rows_b = stage_a.shape[1], stage_b.shape[1]
        chunks = []
        for w_hbm, w_s in ((wg_hbm, wg_s), (wu_hbm, wu_s)):
            for c in range(d // rows_a):
                rows = pl.ds(c * rows_a, rows_a)
                chunks.append((w_hbm.at[layer, e, rows], stage_a, w_s.at[rows]))
        for c in range(f // rows_b):
            rows = pl.ds(c * rows_b, rows_b)
            chunks.append((wd_hbm.at[layer, e, rows], stage_b, wd_s.at[rows]))
        copies = [pltpu.make_async_copy(src, stage.at[j % 2], wsem.at[j % 2])
                  for j, (src, stage, _) in enumerate(chunks)]
        copies[0].start()
        copies[1].start()
        for j, (_, stage, dst) in enumerate(chunks):
            copies[j].wait()
            dst[...] = stage[j % 2].astype(BF16)
            if j + 2 < len(chunks):
                copies[j + 2].start()

    @pl.when(i == 0)
    def _():
        gather_start(0, 0)
        ybuf[1] = jnp.zeros(ybuf.shape[1:], F32)

    @pl.when((i == 0) | (bexp_ref[i] != bexp_ref[jnp.maximum(i - 1, 0)]))
    def _():
        load_expert(bexp_ref[i])

    gather_wait(slot)

    @pl.when(i >= 1)
    def _():
        scatter_wait(slot)

    gather_start(jnp.minimum(i + 1, n - 1), other)
    scatter_start(i - 1, other)
    x = xbuf[slot].astype(BF16)
    dot = functools.partial(jnp.dot, preferred_element_type=F32)
    g = dot(x, wg_s[...])
    u = dot(x, wu_s[...])
    act = (g * _sigmoid(g) * u).astype(BF16)
    ybuf[slot] = dot(act, wd_s[...])

    @pl.when(i == n - 1)
    def _():
        scatter_start(i, slot)
        gather_wait(other)
        scatter_wait(other)
        scatter_wait(slot)


def moe_experts(h_tok, w_gate, w_up, w_down, layer, n_routed, block_expert, slot_tok, slot_dst):
    t, d = h_tok.shape
    n_blocks = block_expert.shape[0]
    f = w_gate.shape[-1]
    hbm = pl.BlockSpec(memory_space=pl.ANY)
    grid_spec = pltpu.PrefetchScalarGridSpec(
        num_scalar_prefetch=3,
        grid=(n_blocks,),
        in_specs=[hbm, hbm, hbm, hbm],
        out_specs=hbm,
        scratch_shapes=[pltpu.VMEM((2, MOE_BLOCK, d), F32), pltpu.VMEM((2, MOE_BLOCK, d), F32),
                        pltpu.VMEM((d, f), BF16), pltpu.VMEM((d, f), BF16), pltpu.VMEM((f, d), BF16),
                        pltpu.VMEM((2, W_STAGE_ELEMS // f, f), F32), pltpu.VMEM((2, W_STAGE_ELEMS // d, d), F32),
                        pltpu.SemaphoreType.DMA((2,)), pltpu.SemaphoreType.DMA((2,)),
                        pltpu.SemaphoreType.DMA((2,))],
    )
    return pl.pallas_call(
        functools.partial(_moe_kernel, layer=layer),
        out_shape=jax.ShapeDtypeStruct((2 * n_routed + MOE_DUMP_ROWS, d), F32),
        grid_spec=grid_spec,
        compiler_params=_cparams(("arbitrary",), 56),
        name="moe_experts",
    )(block_expert, slot_tok, slot_dst, h_tok, w_gate, w_up, w_down)


def route_slots(idx, seq_len, first_row):
    t = idx.shape[1]
    n = 2 * t
    kept = seq_len - first_row
    n_routed = t // seq_len * kept
    flat_e = idx.reshape(-1)
    flat_on = jnp.arange(n, dtype=jnp.int32) % seq_len >= first_row
    is_e = flat_e[:, None] == jnp.arange(N_EXPERTS)[None, :]
    onehot = (is_e & flat_on[:, None]).astype(jnp.int32)
    pieces = onehot.reshape(n // MOE_BLOCK, MOE_BLOCK, N_EXPERTS)
    inner = jnp.cumsum(pieces, axis=1)
    totals = inner[:, -1, :]
    csum = (inner + (jnp.cumsum(totals, axis=0) - totals)[:, None, :]).reshape(n, N_EXPERTS)
    rank = jnp.sum(jnp.where(is_e, csum, 0), axis=1) - 1
    counts = csum[-1]
    padded = (counts + MOE_BLOCK - 1) // MOE_BLOCK * MOE_BLOCK
    pad_end = jnp.cumsum(padded)
    n_blocks = 2 * n_routed // MOE_BLOCK + N_EXPERTS
    cap = n_blocks * MOE_BLOCK
    first_slot = jnp.sum(jnp.where(is_e, (pad_end - padded)[None, :], 0), axis=1)
    dest = jnp.where(flat_on, first_slot + rank, cap)
    slot_src = jnp.full((cap,), -1, jnp.int32).at[dest].set(jnp.arange(n, dtype=jnp.int32), mode='drop')
    starts = jnp.arange(n_blocks, dtype=jnp.int32) * MOE_BLOCK
    last_start = jnp.maximum(pad_end[-1] - MOE_BLOCK, 0)
    ended = pad_end[None, :] <= jnp.minimum(starts, last_start)[:, None]
    block_expert = jnp.minimum(jnp.sum(ended, axis=1), N_EXPERTS - 1).astype(jnp.int32)
    slot_tok = jnp.where(slot_src >= 0, slot_src % t, 0).astype(jnp.int32)
    compact = slot_tok // seq_len * kept + slot_tok % seq_len - first_row
    out_row = (slot_src // t) * n_routed + compact
    pos = jnp.arange(-MOE_BLOCK, cap, dtype=jnp.int32)
    keep = jnp.concatenate([jnp.zeros((MOE_BLOCK,), bool), slot_src >= 0])
    out_row = jnp.concatenate([jnp.zeros((MOE_BLOCK,), jnp.int32), out_row])
    slot_dst = jnp.where(keep, out_row, 2 * n_routed + pos % MOE_DUMP_ROWS).astype(jnp.int32)
    return block_expert, slot_tok, slot_dst


def _combine_kernel(x_ref, y0_ref, y1_ref, w_ref, g_ref, fg_ref, o_ref, *, final):
    w = w_ref[...]
    x = x_ref[0] + g_ref[0] * (y0_ref[...] * w[:, 0:1] + y1_ref[...] * w[:, 1:2])
    if final:
        x = x * lax.rsqrt(jnp.mean(x * x, axis=-1, keepdims=True) + EPS) * fg_ref[...]
    o_ref[0] = x


def moe_combine(xs, yk, wts, mods, gate_col, final_g, final):
    b, l, d = xs.shape
    nt = l // ROW_TILE
    t0 = 1 if final else 0
    n_out = nt - t0
    return pl.pallas_call(
        functools.partial(_combine_kernel, final=final),
        out_shape=jax.ShapeDtypeStruct((b, n_out * ROW_TILE, d), F32),
        grid=(b, n_out),
        in_specs=[pl.BlockSpec((1, ROW_TILE, d), lambda i, t: (i, t + t0, 0)),
                  pl.BlockSpec((ROW_TILE, d), lambda i, t: (i * n_out + t, 0)),
                  pl.BlockSpec((ROW_TILE, d), lambda i, t: (b * n_out + i * n_out + t, 0)),
                  pl.BlockSpec((ROW_TILE, 2), lambda i, t: (i * nt + t + t0, 0)),
                  pl.BlockSpec((1, 1, d), lambda i, t: (_mod_row(i, t + t0), 0, gate_col)),
                  pl.BlockSpec((1, d), lambda i, t: (0, 0))],
        out_specs=pl.BlockSpec((1, ROW_TILE, d), lambda i, t: (i, t, 0)),
        compiler_params=_cparams(("parallel", "parallel"), 48),
        name="moe_combine",
    )(xs, yk, yk, wts, mods, final_g.reshape(1, d))


def _layer(xs, layer, mods, norm1_g, norm2_g, w_in, conv_qkv_w, a_log, dt_bias, dn_norm_g, cf_dw_w,
           cf_ln_g, cf_ln_b, w_out, router_w, router_bias, w_gate, w_up, w_down, final_g, final):
    b, l, d = xs.shape
    in_cols = w_in.shape[-1]

    h = norm_modulate(xs, norm1_g, mods, 0, BF16).reshape(b * l, d)
    p2d = matmul_nt(h, jnp.swapaxes(w_in, 1, 2), layer, 1024, 512)
    p = p2d.reshape(b, l, in_cols)
    aux = delta_gates(p2d, a_log, dt_bias).reshape(b, l, LANES)
    dn = deltanet_heads(p, aux, conv_qkv_w, dn_norm_g)
    cfh = conformer_heads(p, cf_dw_w, cf_ln_g, cf_ln_b, False)
    cfv = conformer_heads(p, cf_dw_w, cf_ln_g, cf_ln_b, True)
    xs = out_proj_residual(dn, cfh, cfv, w_out, layer, xs, mods, 2)

    h2, idx, wts = norm_router(xs, norm2_g, mods, 3, router_w, router_bias)
    first_row = CTX_LEN if final else 0
    n_routed = b * (l - first_row)
    block_expert, slot_tok, slot_dst = route_slots(idx, l, first_row)
    yk = moe_experts(h2.reshape(b * l, d), w_gate, w_up, w_down, layer, n_routed, block_expert, slot_tok,
                     slot_dst)
    return moe_combine(xs, yk, wts.T, mods, 5, final_g, final)


def kernel(x, c, ctx, c_ctx, w_ada, b_ada, norm1_g, norm2_g, w_in, conv_qkv_w, a_log, dt_bias,
           dn_norm_g, cf_dw_w, cf_ln_g, cf_ln_b, w_out, router_w, router_bias, w_gate, w_up,
           w_down, final_g):
    depth = w_ada.shape[0]
    bsz = x.shape[0]
    xs = jnp.concatenate([ctx, x], axis=1)
    c_all = jnp.concatenate([c, c_ctx[None, :], jnp.zeros((8 - bsz - 1, c.shape[1]), F32)], axis=0)
    mods = ada_mods(c_all, w_ada, b_ada)
    for i in range(depth):
        mods_i = mods[i].reshape(8, 1, -1)
        xs = _layer(xs, i, mods_i, norm1_g[i], norm2_g[i], w_in, conv_qkv_w[i], a_log[i], dt_bias[i],
                    dn_norm_g[i], cf_dw_w[i], cf_ln_g[i], cf_ln_b[i], w_out, router_w, router_bias,
                    w_gate, w_up, w_down, final_g, i == depth - 1)
    return xs
```

```python
import functools

import jax
import jax.numpy as jnp
from jax import lax
from jax.experimental import pallas as pl
from jax.experimental.pallas import tpu as pltpu

F32 = jnp.float32
BF16 = jnp.bfloat16

D_MODEL = 4096
CTX_LEN = 256
GRID_W = 64
EPS = 1e-6
DN_HEADS = 16
DN_D = 128
DN_W = DN_HEADS * DN_D
SHORT_CONV = 5
CHUNK = 64
CF_W = 2048
CF_KERNEL = 31
CF_PAD = (CF_KERNEL - 1) // 2
N_EXPERTS = 16
N_GROUPS = 4
EXPERTS_PER_GROUP = 4
D_FF = 1024
MOE_BLOCK = 128
MOE_DUMP_ROWS = 2 * MOE_BLOCK
W_STAGE_ELEMS = 512 * 1024
LANES = 128
ROW_TILE = 256
STEP_GROUPS = (12, 12, 12)
CF_TILE_UNROLL = 8
MIB = 1024 * 1024


def _cparams(sem, vmem_mib):
    return pltpu.CompilerParams(dimension_semantics=sem, vmem_limit_bytes=vmem_mib * MIB)


def _sigmoid(x):
    return 1.0 / (1.0 + jnp.exp(-x))


def _ada_kernel(c_ref, w_ref, b_ref, o_ref):
    c = c_ref[...]
    act = (c * _sigmoid(c)).astype(BF16)
    o_ref[0] = jnp.dot(act, w_ref[0].astype(BF16), preferred_element_type=F32) + b_ref[0]


def ada_mods(c_all, w_ada, b_ada):
    depth, d, n = w_ada.shape
    tn = 512
    return pl.pallas_call(
        _ada_kernel,
        out_shape=jax.ShapeDtypeStruct((depth, 8, n), F32),
        grid=(depth, n // tn),
        in_specs=[pl.BlockSpec((8, d), lambda i, j: (0, 0)),
                  pl.BlockSpec((1, d, tn), lambda i, j: (i, 0, j)),
                  pl.BlockSpec((1, 1, tn), lambda i, j: (i, 0, j))],
        out_specs=pl.BlockSpec((1, 8, tn), lambda i, j: (i, 0, j)),
        compiler_params=_cparams(("parallel", "parallel"), 40),
        name="ada_mods",
    )(c_all, w_ada, b_ada.reshape(depth, 1, n))


def _mod_row(b, t):
    return jnp.where(t == 0, 4, b)


def _norm_mod_kernel(x_ref, g_ref, sh_ref, sc_ref, o_ref):
    x = x_ref[0]
    y = x * lax.rsqrt(jnp.mean(x * x, axis=-1, keepdims=True) + EPS) * g_ref[...]
    o_ref[0] = (y * (1.0 + sc_ref[0]) + sh_ref[0]).astype(o_ref.dtype)


def norm_modulate(xs, gain, mods, shift_col, out_dtype):
    b, l, d = xs.shape
    return pl.pallas_call(
        _norm_mod_kernel,
        out_shape=jax.ShapeDtypeStruct((b, l, d), out_dtype),
        grid=(b, l // ROW_TILE),
        in_specs=[pl.BlockSpec((1, ROW_TILE, d), lambda i, t: (i, t, 0)),
                  pl.BlockSpec((1, d), lambda i, t: (0, 0)),
                  pl.BlockSpec((1, 1, d), lambda i, t: (_mod_row(i, t), 0, shift_col)),
                  pl.BlockSpec((1, 1, d), lambda i, t: (_mod_row(i, t), 0, shift_col + 1))],
        out_specs=pl.BlockSpec((1, ROW_TILE, d), lambda i, t: (i, t, 0)),
        compiler_params=_cparams(("parallel", "parallel"), 40),
        name="norm_modulate",
    )(xs, gain.reshape(1, d), mods, mods)


def _mm_nt_kernel(a_ref, w_ref, o_ref):
    w = w_ref[...].astype(BF16)
    o_ref[...] = lax.dot_general(a_ref[...], w, (((1,), (1,)), ((), ())), preferred_element_type=F32)


def matmul_nt(a, w_t, layer, tm, tn):
    m, k = a.shape
    n = w_t.shape[1]
    return pl.pallas_call(
        _mm_nt_kernel,
        out_shape=jax.ShapeDtypeStruct((m, n), F32),
        grid=(m // tm, pl.cdiv(n, tn)),
        in_specs=[pl.BlockSpec((tm, k), lambda i, j: (i, 0)),
                  pl.BlockSpec((None, tn, k), lambda i, j: (layer, j, 0))],
        out_specs=pl.BlockSpec((tm, tn), lambda i, j: (i, j)),
        compiler_params=_cparams(("parallel", "parallel"), 56),
        name="in_proj",
    )(a, w_t)


def _dprep_kernel(p_ref, alog_ref, dtb_ref, o_ref):
    raw = p_ref[...]
    beta = _sigmoid(raw)
    g = -jnp.exp(alog_ref[...]) * jax.nn.softplus(raw + dtb_ref[...])
    r = lax.broadcasted_iota(jnp.int32, (CHUNK, CHUNK), 0)
    c = lax.broadcasted_iota(jnp.int32, (CHUNK, CHUNK), 1)
    tri_lo = (c <= r).astype(F32)
    tri_up = (c >= r).astype(F32)
    lane = lax.broadcasted_iota(jnp.int32, (CHUNK, LANES), 1)
    for ch in range(ROW_TILE // CHUNK):
        sl = slice(ch * CHUNK, (ch + 1) * CHUNK)
        gch = g[sl]
        cum_f = jnp.dot(tri_lo, gch, preferred_element_type=F32, precision=lax.Precision.HIGHEST)
        cum_b = jnp.dot(tri_up, gch, preferred_element_type=F32, precision=lax.Precision.HIGHEST)
        gc = jnp.where(lane < 3 * DN_HEADS, cum_f, cum_b)
        o_ref[sl, :] = jnp.where(lane < 2 * DN_HEADS, beta[sl], gc)


def delta_gates(p2d, a_log, dt_bias):
    m = p2d.shape[0]
    lanes_vec = lambda v: jnp.zeros((1, LANES), F32).at[0, 2 * DN_HEADS:4 * DN_HEADS].set(v.reshape(-1))
    return pl.pallas_call(
        _dprep_kernel,
        out_shape=jax.ShapeDtypeStruct((m, LANES), F32),
        grid=(m // ROW_TILE,),
        in_specs=[pl.BlockSpec((ROW_TILE, LANES), lambda i: (i, 4 * DN_W // LANES)),
                  pl.BlockSpec((1, LANES), lambda i: (0, 0)),
                  pl.BlockSpec((1, LANES), lambda i: (0, 0))],
        out_specs=pl.BlockSpec((ROW_TILE, LANES), lambda i: (i, 0)),
        compiler_params=_cparams(("parallel",), 32),
        name="delta_gates",
    )(p2d, lanes_vec(a_log), lanes_vec(dt_bias))


def _mm_pairs(xs, ys):
    xb = [x.astype(BF16) for x in xs]
    yb = [_block_diag2(y.astype(BF16)) for y in ys]
    return [jnp.dot(x, y, preferred_element_type=F32) for x, y in zip(xb, yb)]


def _block_diag2(y):
    yy = jnp.concatenate([y, y], axis=0)
    r = lax.broadcasted_iota(jnp.int32, (2 * CHUNK, LANES), 0)
    c = lax.broadcasted_iota(jnp.int32, (2 * CHUNK, LANES), 1)
    return jnp.where((r // CHUNK) == (c // CHUNK), yy, jnp.zeros_like(yy))


def _unit_tri_inverse2(a2s, row, col):
    eye2 = (row == col).astype(F32)
    same = lambda n: (row // n) == (col // n)
    a8 = [jnp.where(same(8), a2, 0.0) for a2 in a2s]
    p = _mm_pairs(a8, a8)
    yield
    t = [eye2 - a for a in a8]
    t = [x + y for x, y in zip(t, _mm_pairs(t, p))]
    yield
    p = _mm_pairs(p, p)
    yield
    t = [x + y for x, y in zip(t, _mm_pairs(t, p))]
    yield
    for n in (16, 32, 64):
        off = same(n) & jnp.logical_not(same(n // 2))
        tc = _mm_pairs(t, [jnp.where(off, a2, 0.0) for a2 in a2s])
        yield
        t = [x - y for x, y in zip(t, _mm_pairs(tc, t))]
        yield
    return t


def _delta_kernel(q_ref, k_ref, v_ref, z_ref, aux_ref, wq_ref, wk_ref, wv_ref, ng_ref, o_ref,
                  qs, ks, vs, bcs, kp_s, n_s, of_s, ob_s, *, seq_len):
    l = seq_len
    n_chunks = l // CHUNK
    ctx_chunks = CTX_LEN // CHUNK
    head = pl.program_id(1)
    halo = 8
    pad = (SHORT_CONV - 1) // 2
    crow = lax.broadcasted_iota(jnp.int32, (CHUNK, 1), 0)

    def conv_silu(x_ref, w_ref, c):
        r0 = c * CHUNK
        seg_lo, seg_hi = (0, CTX_LEN) if r0 < CTX_LEN else (CTX_LEN, l)
        lo, hi = max(r0 - halo, 0), min(r0 + CHUNK + halo, l)
        win = x_ref[0, lo:hi, :]
        zeros = jnp.zeros((halo, LANES), F32)
        if lo > r0 - halo:
            win = jnp.concatenate([zeros, win], axis=0)
        if hi < r0 + CHUNK + halo:
            win = jnp.concatenate([win, zeros], axis=0)
        w = w_ref[...]
        acc = None
        for j in range(SHORT_CONV):
            s = j - pad
            if 0 <= r0 + s and r0 + s + CHUNK <= l:
                piece = x_ref[0, r0 + s:r0 + s + CHUNK, :]
            else:
                piece = win[halo + s:halo + s + CHUNK]
            if r0 + s < seg_lo or r0 + CHUNK - 1 + s >= seg_hi:
                src = crow + (r0 + s)
                piece = jnp.where((src >= seg_lo) & (src < seg_hi), piece, 0.0)
            term = piece * w[j:j + 1]
            acc = term if acc is None else acc + term
        return acc * _sigmoid(acc)

    def l2norm(x):
        return x * lax.rsqrt(jnp.sum(x * x, axis=-1, keepdims=True) + EPS)

    alane = lax.broadcasted_iota(jnp.int32, (CHUNK, LANES), 1)
    prepared = set()

    def ensure_inputs(c):
        if c in prepared:
            return
        prepared.add(c)
        sl = slice(c * CHUNK, (c + 1) * CHUNK)
        qs[sl, :] = l2norm(conv_silu(q_ref, wq_ref, c)) * (DN_D ** -0.5)
        ks[sl, :] = l2norm(conv_silu(k_ref, wk_ref, c))
        vs[sl, :] = conv_silu(v_ref, wv_ref, c)
        aux = aux_ref[0, sl, :]
        for i in range(4):
            colv = jnp.sum(jnp.where(alane == head + i * DN_HEADS, aux, 0.0), axis=-1, keepdims=True)
            bcs[i, sl, :] = jnp.broadcast_to(colv, (CHUNK, LANES))

    def finish_chunk(c):
        sl = slice(c * CHUNK, (c + 1) * CHUNK)
        o = of_s[sl, :] + ob_s[sl, :]
        o = o * lax.rsqrt(jnp.mean(o * o, axis=-1, keepdims=True) + EPS) * ng_ref[...]
        z = z_ref[0, sl, :]
        o_ref[0, sl, :] = (o * (z * _sigmoid(z))).astype(o_ref.dtype)

    row = lax.broadcasted_iota(jnp.int32, (CHUNK, LANES), 0)
    lane = lax.broadcasted_iota(jnp.int32, (CHUNK, LANES), 1)
    fwd = lane < CHUNK
    col = jnp.where(fwd, lane, lane - CHUNK)
    sgn = jnp.where(fwd, 1, -1)
    incl = (row - col) * sgn >= 0
    strict = (row - col) * sgn > 0
    dot = functools.partial(jnp.dot, preferred_element_type=F32)
    nt = (((1,), (1,)), ((), ()))
    tn = (((0,), (0,)), ((), ()))
    acc_refs = (of_s, ob_s)

    def chunk_of(t, d):
        if d == 0:
            return t
        return ctx_chunks - 1 - t if t < ctx_chunks else n_chunks + ctx_chunks - 1 - t

    rows_of = lambda c: slice(c * CHUNK, (c + 1) * CHUNK)

    def g_last_row(c, d):
        r = c * CHUNK + (CHUNK - 1 if d == 0 else 0)
        return bcs[2 + d, r:r + 1, :]

    def prep(steps):
        data, a2s, qks = [], [], []
        for t in steps:
            sf, sb = rows_of(chunk_of(t, 0)), rows_of(chunk_of(t, 1))
            qkv = ((qs[sf, :], ks[sf, :], vs[sf, :]), (qs[sb, :], ks[sb, :], vs[sb, :]))
            beta = (bcs[0, sf, :], bcs[1, sb, :])
            gc = (bcs[2, sf, :], bcs[3, sb, :])
            data.append((qkv, beta, gc))
            b2 = jnp.where(fwd, beta[0], beta[1])
            c2 = jnp.where(fwd, gc[0], gc[1])
            r2 = jnp.sum(jnp.where(row == col, c2, 0.0), axis=0, keepdims=True)
            decay = jnp.where(incl, jnp.exp(jnp.where(incl, c2 - r2, 0.0)), 0.0)
            kst = jnp.concatenate([qkv[0][1], qkv[1][1]], axis=0).astype(BF16)
            qst = jnp.concatenate([qkv[0][0], qkv[1][0]], axis=0).astype(BF16)
            gram = lax.dot_general(kst, kst, nt, preferred_element_type=F32)
            qk = lax.dot_general(qst, kst, nt, preferred_element_type=F32)
            gram = jnp.where(fwd, gram[:CHUNK], gram[CHUNK:])
            qks.append(jnp.where(fwd, qk[:CHUNK], qk[CHUNK:]) * decay)
            a2s.append(jnp.where(strict, b2 * gram * decay, 0.0))
        yield
        t2s = yield from _unit_tri_inverse2(a2s, row, col)
        uws = []
        for (qkv, beta, gc), t2 in zip(data, t2s):
            rhs = jnp.concatenate(
                [jnp.concatenate([qkv[d][2] * beta[d], qkv[d][1] * (beta[d] * jnp.exp(gc[d]))], axis=1)
                 for d in range(2)], axis=0)
            uws.append(dot(_block_diag2(t2.astype(BF16)), rhs.astype(BF16)))
        yield
        for t, (qkv, beta, gc), qk, uw in zip(steps, data, qks, uws):
            for d in range(2):
                c = chunk_of(t, d)
                q, k, _ = qkv[d]
                uw_d = uw[d * CHUNK:(d + 1) * CHUNK].astype(BF16)
                kd = k * jnp.exp(g_last_row(c, d) - gc[d])
                kn = lax.dot_general(kd.astype(BF16), uw_d, tn, preferred_element_type=F32)
                qk_d = jnp.where(fwd == (d == 0), qk, 0.0).astype(BF16)
                po = dot(qk_d, jnp.concatenate([uw_d, uw_d], axis=0))
                p_mat = q * jnp.exp(gc[d]) - po[:, DN_D:]
                kp_s[t, d] = jnp.concatenate([kn[:, DN_D:], p_mat], axis=0).astype(BF16)
                n_s[t, d] = kn[:, :DN_D]
                acc_refs[d][rows_of(c), :] = po[:, :DN_D]
            yield

    state = [jnp.zeros((DN_D, DN_D), F32), jnp.zeros((DN_D, DN_D), F32)]
    visits = {}

    def scan(steps):
        for t in steps:
            for d in range(2):
                c = chunk_of(t, d)
                r = dot(kp_s[t, d], state[d].astype(BF16))
                acc_refs[d][rows_of(c), :] += r[DN_D:]
                state[d] = state[d] * jnp.exp(g_last_row(c, d)) - r[:DN_D] + n_s[t, d]
                visits[c] = visits.get(c, 0) + 1
                if visits[c] == 2:
                    finish_chunk(c)
            yield

    def interleave(gens):
        gens = list(gens)
        while gens:
            for g in list(gens):
                try:
                    next(g)
                except StopIteration:
                    gens.remove(g)

    def inputs(steps):
        for t in steps:
            for d in range(2):
                if chunk_of(t, d) not in prepared:
                    ensure_inputs(chunk_of(t, d))
                    yield

    assert sum(STEP_GROUPS) == n_chunks
    bounds = [sum(STEP_GROUPS[:i]) for i in range(len(STEP_GROUPS) + 1)]
    groups = [list(range(a, b)) for a, b in zip(bounds[:-1], bounds[1:])]
    stages = (inputs, prep, scan)
    for tick in range(len(groups) + len(stages) - 1):
        interleave([stage(groups[tick - k]) for k, stage in enumerate(stages) if 0 <= tick - k < len(groups)])
    assert all(v == 2 for v in visits.values()) and len(visits) == n_chunks


def deltanet_heads(p_main, aux, conv_w, norm_g):
    b, l, _ = p_main.shape
    n_chunks = l // CHUNK
    blk = lambda off: pl.BlockSpec((1, l, DN_D), lambda i, h, off=off: (i, 0, off + h))
    wblk = lambda off: pl.BlockSpec((SHORT_CONV, DN_D), lambda i, h, off=off: (0, off + h))
    seq = pltpu.VMEM((l, DN_D), F32)
    return pl.pallas_call(
        functools.partial(_delta_kernel, seq_len=l),
        out_shape=jax.ShapeDtypeStruct((b, l, DN_W), BF16),
        grid=(b, DN_HEADS),
        in_specs=[blk(0), blk(DN_HEADS), blk(2 * DN_HEADS), blk(3 * DN_HEADS),
                  pl.BlockSpec((1, l, LANES), lambda i, h: (i, 0, 0)),
                  wblk(0), wblk(DN_HEADS), wblk(2 * DN_HEADS),
                  pl.BlockSpec((1, DN_D), lambda i, h: (0, 0))],
        out_specs=pl.BlockSpec((1, l, DN_D), lambda i, h: (i, 0, h)),
        scratch_shapes=[seq, seq, seq,
                        pltpu.VMEM((4, l, LANES), F32),
                        pltpu.VMEM((n_chunks, 2, DN_D + CHUNK, DN_D), BF16),
                        pltpu.VMEM((n_chunks, 2, DN_D, DN_D), F32),
                        seq, seq],
        compiler_params=_cparams(("parallel", "parallel"), 48),
        name="deltanet_heads",
    )(p_main, p_main, p_main, p_main, aux, conv_w, conv_w, conv_w, norm_g.reshape(1, DN_D))


def _cf_kernel(val_lo_ref, val_hi_ref, gate_lo_ref, gate_hi_ref, w_ref, g_ref, b_ref, o_ref, cbuf, lbuf,
               *, seq_len, vertical):
    lat = seq_len - CTX_LEN
    half_lanes = LANES // 2
    low_half = lax.broadcasted_iota(jnp.int32, (seq_len, LANES), 1) < half_lanes

    def straddle(lo_ref, hi_ref):
        return pltpu.roll(jnp.where(low_half, hi_ref[0], lo_ref[0]), half_lanes, 1)

    gate = straddle(gate_lo_ref, gate_hi_ref)
    u = straddle(val_lo_ref, val_hi_ref) * _sigmoid(gate)
    w = w_ref[...]
    gain, bias = g_ref[...], b_ref[...]
    hpad = 16
    vpad = CF_PAD * GRID_W

    def finish(acc, row0):
        xc = acc - jnp.mean(acc, axis=-1, keepdims=True)
        y = xc * lax.rsqrt(jnp.mean(xc * xc, axis=-1, keepdims=True) + EPS) * gain + bias
        o_ref[0, pl.ds(row0, GRID_W), :] = (y * _sigmoid(y)).astype(o_ref.dtype)

    def window_conv(win, masked):
        r = lax.broadcasted_iota(jnp.int32, (GRID_W, 1), 0)
        acc = jnp.zeros((GRID_W, LANES), F32)
        for k in range(CF_KERNEL):
            off = hpad - CF_PAD + k
            piece = win[off:off + GRID_W]
            if masked:
                src = r + (k - CF_PAD)
                piece = jnp.where((src >= 0) & (src < GRID_W), piece, 0.0)
            acc = acc + piece * w[k:k + 1]
        return acc

    zeros_h = jnp.zeros((hpad, LANES), F32)
    cbuf[0:hpad, :] = zeros_h
    cbuf[hpad + CTX_LEN:, :] = zeros_h
    cbuf[hpad:hpad + CTX_LEN, :] = u[:CTX_LEN]

    def ctx_tile(g, carry):
        base = pl.multiple_of(g * GRID_W, GRID_W)
        finish(window_conv(cbuf[pl.ds(base, GRID_W + 2 * hpad), :], False), base)
        return carry

    lax.fori_loop(0, CTX_LEN // GRID_W, ctx_tile, 0, unroll=CF_TILE_UNROLL)

    if vertical:
        zeros_v = jnp.zeros((vpad, LANES), F32)
        lbuf[0:vpad, :] = zeros_v
        lbuf[vpad + lat:, :] = zeros_v
        lbuf[vpad:vpad + lat, :] = u[CTX_LEN:]

        def lat_tile(g, carry):
            acc = jnp.zeros((GRID_W, LANES), F32)
            for k in range(CF_KERNEL):
                base = pl.multiple_of((g + k) * GRID_W, GRID_W)
                acc = acc + lbuf[pl.ds(base, GRID_W), :] * w[k:k + 1]
            finish(acc, pl.multiple_of(CTX_LEN + g * GRID_W, GRID_W))
            return carry
    else:
        lbuf[0:hpad, :] = zeros_h
        lbuf[hpad + lat:, :] = zeros_h
        lbuf[hpad:hpad + lat, :] = u[CTX_LEN:]

        def lat_tile(g, carry):
            base = pl.multiple_of(g * GRID_W, GRID_W)
            finish(window_conv(lbuf[pl.ds(base, GRID_W + 2 * hpad), :], True),
                   pl.multiple_of(CTX_LEN + base, GRID_W))
            return carry

    lax.fori_loop(0, lat // GRID_W, lat_tile, 0, unroll=CF_TILE_UNROLL)


def conformer_heads(p, dw_w, ln_g, ln_b, vertical):
    b, l, _ = p.shape
    half = CF_W // 2
    lat = l - CTX_LEN
    first = half // LANES if vertical else 0
    cf_off = 4 * DN_W + 4 * DN_HEADS
    assert cf_off % LANES == LANES // 2 and CF_W % LANES == 0
    val0 = cf_off // LANES + first
    gate0 = (cf_off + CF_W) // LANES + first
    lrows = lat + (2 * CF_PAD * GRID_W if vertical else 32)
    chan = lambda rows: pl.BlockSpec((rows, LANES), lambda i, j: (0, first + j))
    col = lambda c0: pl.BlockSpec((1, l, LANES), lambda i, j: (i, 0, c0 + j))
    return pl.pallas_call(
        functools.partial(_cf_kernel, seq_len=l, vertical=vertical),
        out_shape=jax.ShapeDtypeStruct((b, l, half), BF16),
        grid=(b, half // LANES),
        in_specs=[col(val0), col(val0 + 1), col(gate0), col(gate0 + 1),
                  chan(CF_KERNEL), chan(1), chan(1)],
        out_specs=pl.BlockSpec((1, l, LANES), lambda i, j: (i, 0, j)),
        scratch_shapes=[pltpu.VMEM((CTX_LEN + 32, LANES), F32), pltpu.VMEM((lrows, LANES), F32)],
        compiler_params=_cparams(("parallel", "parallel"), 32),
        name="conformer_v" if vertical else "conformer_h",
    )(p, p, p, p, dw_w, ln_g.reshape(1, CF_W), ln_b.reshape(1, CF_W))


def _out_proj_kernel(dn_ref, ch_ref, cv_ref, w0_ref, w1_ref, w2_ref, x_ref, gl_ref, gc_ref, o_ref, *, tm):
    dot = functools.partial(jnp.dot, preferred_element_type=F32)
    acc = (dot(dn_ref[0], w0_ref[...].astype(BF16)) + dot(ch_ref[0], w1_ref[...].astype(BF16))
           + dot(cv_ref[0], w2_ref[...].astype(BF16)))
    row = pl.program_id(1) * tm + lax.broadcasted_iota(jnp.int32, (tm, 1), 0)
    gate = jnp.where(row < CTX_LEN, gc_ref[0], gl_ref[0])
    o_ref[0] = x_ref[0] + gate * acc


def out_proj_residual(dn, cfh, cfv, w_out, layer, xs, mods, gate_col):
    b, l, d = xs.shape
    tm, tn = l // 2, 512
    k0, k1 = dn.shape[-1], cfh.shape[-1]
    act = lambda kk: pl.BlockSpec((1, tm, kk), lambda i, t, j: (i, t, 0))
    gate = lambda r: pl.BlockSpec((1, 1, tn), lambda i, t, j, r=r: (i if r is None else r, 0, gate_col * (d // tn) + j))
    return pl.pallas_call(
        functools.partial(_out_proj_kernel, tm=tm),
        out_shape=jax.ShapeDtypeStruct((b, l, d), F32),
        grid=(b, l // tm, d // tn),
        in_specs=[act(k0), act(k1), act(k1),
                  pl.BlockSpec((None, k0, tn), lambda i, t, j: (layer, 0, j)),
                  pl.BlockSpec((None, k1, tn), lambda i, t, j: (layer, k0 // k1, j)),
                  pl.BlockSpec((None, k1, tn), lambda i, t, j: (layer, k0 // k1 + 1, j)),
                  pl.BlockSpec((1, tm, tn), lambda i, t, j: (i, t, j)),
                  gate(None), gate(4)],
        out_specs=pl.BlockSpec((1, tm, tn), lambda i, t, j: (i, t, j)),
        compiler_params=_cparams(("parallel", "parallel", "parallel"), 56),
        name="out_proj",
    )(dn, cfh, cfv, w_out, w_out, w_out, xs, mods, mods)


def _norm_router_kernel(x_ref, g_ref, sh_ref, sc_ref, rw_ref, rb_ref, h_ref, idx_ref, wt_ref):
    x = x_ref[0]
    y = x * lax.rsqrt(jnp.mean(x * x, axis=-1, keepdims=True) + EPS) * g_ref[...]
    h = y * (1.0 + sc_ref[0]) + sh_ref[0]
    h_ref[0] = h
    logits = lax.dot_general(rw_ref[...], h, (((1,), (1,)), ((), ())),
                             preferred_element_type=F32, precision=lax.Precision.HIGHEST)
    scores = _sigmoid(logits)
    biased = scores + rb_ref[...]
    rows = [biased[e:e + 1] for e in range(N_EXPERTS)]
    srows = [scores[e:e + 1] for e in range(N_EXPERTS)]

    def pair_max(v):
        best = v[0] + v[1]
        for a in range(len(v)):
            for bb in range(a + 1, len(v)):
                if (a, bb) != (0, 1):
                    best = jnp.maximum(best, v[a] + v[bb])
        return best

    gscore = [pair_max(rows[g * 4:(g + 1) * 4]) for g in range(N_GROUPS)]
    best_g = jnp.zeros_like(gscore[0], dtype=jnp.int32)
    best_v = gscore[0]
    for g in range(1, N_GROUPS):
        better = gscore[g] > best_v
        best_g = jnp.where(better, g, best_g)
        best_v = jnp.where(better, gscore[g], best_v)

    def pick(vals):
        out = []
        for j in range(EXPERTS_PER_GROUP):
            m = vals[j]
            for g in range(1, N_GROUPS):
                m = jnp.where(best_g == g, vals[g * 4 + j], m)
            out.append(m)
        return out

    m, s = pick(rows), pick(srows)
    i1 = jnp.zeros_like(best_g)
    v1 = m[0]
    for j in range(1, 4):
        better = m[j] > v1
        i1 = jnp.where(better, j, i1)
        v1 = jnp.where(better, m[j], v1)
    i2 = jnp.full_like(best_g, -1)
    v2 = jnp.full_like(v1, -jnp.inf)
    for j in range(4):
        better = (i1 != j) & ((m[j] > v2) | (i2 < 0))
        i2 = jnp.where(better, j, i2)
        v2 = jnp.where(better, m[j], v2)

    def sel(vals, i):
        out = vals[0]
        for j in range(1, 4):
            out = jnp.where(i == j, vals[j], out)
        return out

    w1, w2 = sel(s, i1), sel(s, i2)
    tot = w1 + w2
    idx_ref[0:1, :] = best_g * EXPERTS_PER_GROUP + i1
    idx_ref[1:2, :] = best_g * EXPERTS_PER_GROUP + i2
    wt_ref[0:1, :] = w1 / tot
    wt_ref[1:2, :] = w2 / tot


def norm_router(xs, gain, mods, shift_col, router_w, router_bias):
    b, l, d = xs.shape
    nt = l // ROW_TILE
    tok = lambda i, t: (0, i * nt + t)
    return pl.pallas_call(
        _norm_router_kernel,
        out_shape=(jax.ShapeDtypeStruct((b, l, d), F32),
                   jax.ShapeDtypeStruct((2, b * l), jnp.int32),
                   jax.ShapeDtypeStruct((2, b * l), F32)),
        grid=(b, nt),
        in_specs=[pl.BlockSpec((1, ROW_TILE, d), lambda i, t: (i, t, 0)),
                  pl.BlockSpec((1, d), lambda i, t: (0, 0)),
                  pl.BlockSpec((1, 1, d), lambda i, t: (_mod_row(i, t), 0, shift_col)),
                  pl.BlockSpec((1, 1, d), lambda i, t: (_mod_row(i, t), 0, shift_col + 1)),
                  pl.BlockSpec((N_EXPERTS, d), lambda i, t: (0, 0)),
                  pl.BlockSpec((N_EXPERTS, 1), lambda i, t: (0, 0))],
        out_specs=(pl.BlockSpec((1, ROW_TILE, d), lambda i, t: (i, t, 0)),
                   pl.BlockSpec((2, ROW_TILE), tok),
                   pl.BlockSpec((2, ROW_TILE), tok)),
        compiler_params=_cparams(("parallel", "parallel"), 48),
        name="norm_router",
    )(xs, gain.reshape(1, d), mods, mods, router_w.T, router_bias.reshape(N_EXPERTS, 1))


def _moe_kernel(bexp_ref, stok_ref, sdst_ref, h_hbm, wg_hbm, wu_hbm, wd_hbm, y_hbm,
                xbuf, ybuf, wg_s, wu_s, wd_s, stage_a, stage_b, gsem, ssem, wsem, *, layer):
    i = pl.program_id(0)
    n = pl.num_programs(0)
    slot = lax.rem(i, 2)
    other = 1 - slot
    d, f = wg_s.shape

    def gather_start(block, buf):
        base = block * MOE_BLOCK
        for r in range(MOE_BLOCK):
            pltpu.make_async_copy(h_hbm.at[pl.ds(stok_ref[base + r], 1)], xbuf.at[buf, pl.ds(r, 1)],
                                  gsem.at[buf]).start()

    def gather_wait(buf):
        pltpu.make_async_copy(h_hbm.at[pl.ds(0, MOE_BLOCK)], xbuf.at[buf], gsem.at[buf]).wait()

    def scatter_start(block, buf):
        base = (block + 1) * MOE_BLOCK
        for r in range(MOE_BLOCK):
            pltpu.make_async_copy(ybuf.at[buf, pl.ds(r, 1)], y_hbm.at[pl.ds(sdst_ref[base + r], 1)],
                                  ssem.at[buf]).start()

    def scatter_wait(buf):
        pltpu.make_async_copy(ybuf.at[buf], y_hbm.at[pl.ds(0, MOE_BLOCK)], ssem.at[buf]).wait()

    def load_expert(e):
        rows_a, rows_b = stage_a.shape[1], stage_b.shape[1]
        chunks = []
        for w_hbm, w_s in ((wg_hbm, wg_s), (wu_hbm, wu_s)):
            for c in range(d // rows_a):
                rows = pl.ds(c * rows_a, rows_a)
                chunks.append((w_hbm.at[layer, e, rows], stage_a, w_s.at[rows]))
        for c in range(f // rows_b):
            rows = pl.ds(c * rows_b, rows_b)
            chunks.append((wd_hbm.at[layer, e, rows], stage_b, wd_s.at[rows]))
        copies = [pltpu.make_async_copy(src, stage.at[j % 2], wsem.at[j % 2])
                  for j, (src, stage, _) in enumerate(chunks)]
        copies[0].start()
        copies[1].start()
        for j, (_, stage, dst) in enumerate(chunks):
            copies[j].wait()
            dst[...] = stage[j % 2].astype(BF16)
            if j + 2 < len(chunks):
                copies[j + 2].start()

    @pl.when(i == 0)
    def _():
        gather_start(0, 0)
        ybuf[1] = jnp.zeros(ybuf.shape[1:], F32)

    @pl.when((i == 0) | (bexp_ref[i] != bexp_ref[jnp.maximum(i - 1, 0)]))
    def _():
        load_expert(bexp_ref[i])

    gather_wait(slot)

    @pl.when(i >= 1)
    def _():
        scatter_wait(slot)

    gather_start(jnp.minimum(i + 1, n - 1), other)
    scatter_start(i - 1, other)
    x = xbuf[slot].astype(BF16)
    dot = functools.partial(jnp.dot, preferred_element_type=F32)
    g = dot(x, wg_s[...])
    u = dot(x, wu_s[...])
    act = (g * _sigmoid(g) * u).astype(BF16)
    ybuf[slot] = dot(act, wd_s[...])

    @pl.when(i == n - 1)
    def _():
        scatter_start(i, slot)
        gather_wait(other)
        scatter_wait(other)
        scatter_wait(slot)


def moe_experts(h_tok, w_gate, w_up, w_down, layer, n_routed, block_expert, slot_tok, slot_dst):
    t, d = h_tok.shape
    n_blocks = block_expert.shape[0]
    f = w_gate.shape[-1]
    hbm = pl.BlockSpec(memory_space=pl.ANY)
    grid_spec = pltpu.PrefetchScalarGridSpec(
        num_scalar_prefetch=3,
        grid=(n_blocks,),
        in_specs=[hbm, hbm, hbm, hbm],
        out_specs=hbm,
        scratch_shapes=[pltpu.VMEM((2, MOE_BLOCK, d), F32), pltpu.VMEM((2, MOE_BLOCK, d), F32),
                        pltpu.VMEM((d, f), BF16), pltpu.VMEM((d, f), BF16), pltpu.VMEM((f, d), BF16),
                        pltpu.VMEM((2, W_STAGE_ELEMS // f, f), F32), pltpu.VMEM((2, W_STAGE_ELEMS // d, d), F32),
                        pltpu.SemaphoreType.DMA((2,)), pltpu.SemaphoreType.DMA((2,)),
                        pltpu.SemaphoreType.DMA((2,))],
    )
    return pl.pallas_call(
        functools.partial(_moe_kernel, layer=layer),
        out_shape=jax.ShapeDtypeStruct((2 * n_routed + MOE_DUMP_ROWS, d), F32),
        grid_spec=grid_spec,
        compiler_params=_cparams(("arbitrary",), 56),
        name="moe_experts",
    )(block_expert, slot_tok, slot_dst, h_tok, w_gate, w_up, w_down)


def route_slots(idx, seq_len, first_row):
    t = idx.shape[1]
    n = 2 * t
    kept = seq_len - first_row
    n_routed = t // seq_len * kept
    flat_e = idx.reshape(-1)
    flat_on = jnp.arange(n, dtype=jnp.int32) % seq_len >= first_row
    is_e = flat_e[:, None] == jnp.arange(N_EXPERTS)[None, :]
    onehot = (is_e & flat_on[:, None]).astype(jnp.int32)
    pieces = onehot.reshape(n // MOE_BLOCK, MOE_BLOCK, N_EXPERTS)
    inner = jnp.cumsum(pieces, axis=1)
    totals = inner[:, -1, :]
    csum = (inner + (jnp.cumsum(totals, axis=0) - totals)[:, None, :]).reshape(n, N_EXPERTS)
    rank = jnp.sum(jnp.where(is_e, csum, 0), axis=1) - 1
    counts = csum[-1]
    padded = (counts + MOE_BLOCK - 1) // MOE_BLOCK * MOE_BLOCK
    pad_end = jnp.cumsum(padded)
    n_blocks = 2 * n_routed // MOE_BLOCK + N_EXPERTS
    cap = n_blocks * MOE_BLOCK
    first_slot = jnp.sum(jnp.where(is_e, (pad_end - padded)[None, :], 0), axis=1)
    dest = jnp.where(flat_on, first_slot + rank, cap)
    slot_src = jnp.full((cap,), -1, jnp.int32).at[dest].set(jnp.arange(n, dtype=jnp.int32), mode='drop')
    starts = jnp.arange(n_blocks, dtype=jnp.int32) * MOE_BLOCK
    last_start = jnp.maximum(pad_end[-1] - MOE_BLOCK, 0)
    ended = pad_end[None, :] <= jnp.minimum(starts, last_start)[:, None]
    block_expert = jnp.minimum(jnp.sum(ended, axis=1), N_EXPERTS - 1).astype(jnp.int32)
    slot_tok = jnp.where(slot_src >= 0, slot_src % t, 0).astype(jnp.int32)
    compact = slot_tok // seq_len * kept + slot_tok % seq_len - first_row
    out_row = (slot_src // t) * n_routed + compact
    pos = jnp.arange(-MOE_BLOCK, cap, dtype=jnp.int32)
    keep = jnp.concatenate([jnp.zeros((MOE_BLOCK,), bool), slot_src >= 0])
    out_row = jnp.concatenate([jnp.zeros((MOE_BLOCK,), jnp.int32), out_row])
    slot_dst = jnp.where(keep, out_row, 2 * n_routed + pos % MOE_DUMP_ROWS).astype(jnp.int32)
    return block_expert, slot_tok, slot_dst


def _combine_kernel(x_ref, y0_ref, y1_ref, w_ref, g_ref, fg_ref, o_ref, *, final):
    w = w_ref[...]
    x = x_ref[0] + g_ref[0] * (y0_ref[...] * w[:, 0:1] + y1_ref[...] * w[:, 1:2])
    if final:
        x = x * lax.rsqrt(jnp.mean(x * x, axis=-1, keepdims=True) + EPS) * fg_ref[...]
    o_ref[0] = x


def moe_combine(xs, yk, wts, mods, gate_col, final_g, final):
    b, l, d = xs.shape
    nt = l // ROW_TILE
    t0 = 1 if final else 0
    n_out = nt - t0
    return pl.pallas_call(
        functools.partial(_combine_kernel, final=final),
        out_shape=jax.ShapeDtypeStruct((b, n_out * ROW_TILE, d), F32),
        grid=(b, n_out),
        in_specs=[pl.BlockSpec((1, ROW_TILE, d), lambda i, t: (i, t + t0, 0)),
                  pl.BlockSpec((ROW_TILE, d), lambda i, t: (i * n_out + t, 0)),
                  pl.BlockSpec((ROW_TILE, d), lambda i, t: (b * n_out + i * n_out + t, 0)),
                  pl.BlockSpec((ROW_TILE, 2), lambda i, t: (i * nt + t + t0, 0)),
                  pl.BlockSpec((1, 1, d), lambda i, t: (_mod_row(i, t + t0), 0, gate_col)),
                  pl.BlockSpec((1, d), lambda i, t: (0, 0))],
        out_specs=pl.BlockSpec((1, ROW_TILE, d), lambda i, t: (i, t, 0)),
        compiler_params=_cparams(("parallel", "parallel"), 48),
        name="moe_combine",
    )(xs, yk, yk, wts, mods, final_g.reshape(1, d))


def _layer(xs, layer, mods, norm1_g, norm2_g, w_in, conv_qkv_w, a_log, dt_bias, dn_norm_g, cf_dw_w,
           cf_ln_g, cf_ln_b, w_out, router_w, router_bias, w_gate, w_up, w_down, final_g, final):
    b, l, d = xs.shape
    in_cols = w_in.shape[-1]

    h = norm_modulate(xs, norm1_g, mods, 0, BF16).reshape(b * l, d)
    p2d = matmul_nt(h, jnp.swapaxes(w_in, 1, 2), layer, 1536, 512)
    p = p2d.reshape(b, l, in_cols)
    aux = delta_gates(p2d, a_log, dt_bias).reshape(b, l, LANES)
    dn = deltanet_heads(p, aux, conv_qkv_w, dn_norm_g)
    cfh = conformer_heads(p, cf_dw_w, cf_ln_g, cf_ln_b, False)
    cfv = conformer_heads(p, cf_dw_w, cf_ln_g, cf_ln_b, True)
    xs = out_proj_residual(dn, cfh, cfv, w_out, layer, xs, mods, 2)

    h2, idx, wts = norm_router(xs, norm2_g, mods, 3, router_w, router_bias)
    first_row = CTX_LEN if final else 0
    n_routed = b * (l - first_row)
    block_expert, slot_tok, slot_dst = route_slots(idx, l, first_row)
    yk = moe_experts(h2.reshape(b * l, d), w_gate, w_up, w_down, layer, n_routed, block_expert, slot_tok,
                     slot_dst)
    return moe_combine(xs, yk, wts.T, mods, 5, final_g, final)


def kernel(x, c, ctx, c_ctx, w_ada, b_ada, norm1_g, norm2_g, w_in, conv_qkv_w, a_log, dt_bias,
           dn_norm_g, cf_dw_w, cf_ln_g, cf_ln_b, w_out, router_w, router_bias, w_gate, w_up,
           w_down, final_g):
    depth = w_ada.shape[0]
    bsz = x.shape[0]
    xs = jnp.concatenate([ctx, x], axis=1)
    c_all = jnp.concatenate([c, c_ctx[None, :], jnp.zeros((8 - bsz - 1, c.shape[1]), F32)], axis=0)
    mods = ada_mods(c_all, w_ada, b_ada)
    for i in range(depth):
        mods_i = mods[i].reshape(8, 1, -1)
        xs = _layer(xs, i, mods_i, norm1_g[i], norm2_g[i], w_in, conv_qkv_w[i], a_log[i], dt_bias[i],
                    dn_norm_g[i], cf_dw_w[i], cf_ln_g[i], cf_ln_b[i], w_out, router_w, router_bias,
                    w_gate, w_up, w_down, final_g, i == depth - 1)
    return xs
```

```python
import functools

import jax
import jax.numpy as jnp
from jax import lax
from jax.experimental import pallas as pl
from jax.experimental.pallas import tpu as pltpu

F32 = jnp.float32
BF16 = jnp.bfloat16

D_MODEL = 4096
CTX_LEN = 256
GRID_W = 64
EPS = 1e-6
DN_HEADS = 16
DN_D = 128
DN_W = DN_HEADS * DN_D
SHORT_CONV = 5
CHUNK = 64
CF_W = 2048
CF_KERNEL = 31
CF_PAD = (CF_KERNEL - 1) // 2
N_EXPERTS = 16
N_GROUPS = 4
EXPERTS_PER_GROUP = 4
D_FF = 1024
MOE_BLOCK = 128
MOE_DUMP_ROWS = 2 * MOE_BLOCK
W_STAGE_ELEMS = 512 * 1024
LANES = 128
ROW_TILE = 256
STEP_GROUPS = (12, 12, 12)
CF_TILE_UNROLL = 8
MIB = 1024 * 1024


def _cparams(sem, vmem_mib):
    return pltpu.CompilerParams(dimension_semantics=sem, vmem_limit_bytes=vmem_mib * MIB)


def _sigmoid(x):
    return 1.0 / (1.0 + jnp.exp(-x))


def _ada_kernel(c_ref, w_ref, b_ref, o_ref):
    c = c_ref[...]
    act = (c * _sigmoid(c)).astype(BF16)
    o_ref[0] = jnp.dot(act, w_ref[0].astype(BF16), preferred_element_type=F32) + b_ref[0]


def ada_mods(c_all, w_ada, b_ada):
    depth, d, n = w_ada.shape
    tn = 512
    return pl.pallas_call(
        _ada_kernel,
        out_shape=jax.ShapeDtypeStruct((depth, 8, n), F32),
        grid=(depth, n // tn),
        in_specs=[pl.BlockSpec((8, d), lambda i, j: (0, 0)),
                  pl.BlockSpec((1, d, tn), lambda i, j: (i, 0, j)),
                  pl.BlockSpec((1, 1, tn), lambda i, j: (i, 0, j))],
        out_specs=pl.BlockSpec((1, 8, tn), lambda i, j: (i, 0, j)),
        compiler_params=_cparams(("parallel", "parallel"), 40),
        name="ada_mods",
    )(c_all, w_ada, b_ada.reshape(depth, 1, n))


def _mod_row(b, t):
    return jnp.where(t == 0, 4, b)


def _norm_mod_kernel(x_ref, g_ref, sh_ref, sc_ref, o_ref):
    x = x_ref[0]
    y = x * lax.rsqrt(jnp.mean(x * x, axis=-1, keepdims=True) + EPS) * g_ref[...]
    o_ref[0] = (y * (1.0 + sc_ref[0]) + sh_ref[0]).astype(o_ref.dtype)


def norm_modulate(xs, gain, mods, shift_col, out_dtype):
    b, l, d = xs.shape
    return pl.pallas_call(
        _norm_mod_kernel,
        out_shape=jax.ShapeDtypeStruct((b, l, d), out_dtype),
        grid=(b, l // ROW_TILE),
        in_specs=[pl.BlockSpec((1, ROW_TILE, d), lambda i, t: (i, t, 0)),
                  pl.BlockSpec((1, d), lambda i, t: (0, 0)),
                  pl.BlockSpec((1, 1, d), lambda i, t: (_mod_row(i, t), 0, shift_col)),
                  pl.BlockSpec((1, 1, d), lambda i, t: (_mod_row(i, t), 0, shift_col + 1))],
        out_specs=pl.BlockSpec((1, ROW_TILE, d), lambda i, t: (i, t, 0)),
        compiler_params=_cparams(("parallel", "parallel"), 40),
        name="norm_modulate",
    )(xs, gain.reshape(1, d), mods, mods)


def _mm_nt_kernel(a_ref, w_ref, o_ref):
    w = w_ref[...].astype(BF16)
    o_ref[...] = lax.dot_general(a_ref[...], w, (((1,), (1,)), ((), ())), preferred_element_type=F32)


def matmul_nt(a, w_t, layer, tm, tn):
    m, k = a.shape
    n = w_t.shape[1]
    return pl.pallas_call(
        _mm_nt_kernel,
        out_shape=jax.ShapeDtypeStruct((m, n), F32),
        grid=(m // tm, pl.cdiv(n, tn)),
        in_specs=[pl.BlockSpec((tm, k), lambda i, j: (i, 0)),
                  pl.BlockSpec((None, tn, k), lambda i, j: (layer, j, 0))],
        out_specs=pl.BlockSpec((tm, tn), lambda i, j: (i, j)),
        compiler_params=_cparams(("parallel", "parallel"), 56),
        name="in_proj",
    )(a, w_t)


def _dprep_kernel(p_ref, alog_ref, dtb_ref, o_ref):
    raw = p_ref[...]
    beta = _sigmoid(raw)
    g = -jnp.exp(alog_ref[...]) * jax.nn.softplus(raw + dtb_ref[...])
    r = lax.broadcasted_iota(jnp.int32, (CHUNK, CHUNK), 0)
    c = lax.broadcasted_iota(jnp.int32, (CHUNK, CHUNK), 1)
    tri_lo = (c <= r).astype(F32)
    tri_up = (c >= r).astype(F32)
    lane = lax.broadcasted_iota(jnp.int32, (CHUNK, LANES), 1)
    for ch in range(ROW_TILE // CHUNK):
        sl = slice(ch * CHUNK, (ch + 1) * CHUNK)
        gch = g[sl]
        cum_f = jnp.dot(tri_lo, gch, preferred_element_type=F32, precision=lax.Precision.HIGHEST)
        cum_b = jnp.dot(tri_up, gch, preferred_element_type=F32, precision=lax.Precision.HIGHEST)
        gc = jnp.where(lane < 3 * DN_HEADS, cum_f, cum_b)
        o_ref[sl, :] = jnp.where(lane < 2 * DN_HEADS, beta[sl], gc)


def delta_gates(p2d, a_log, dt_bias):
    m = p2d.shape[0]
    lanes_vec = lambda v: jnp.zeros((1, LANES), F32).at[0, 2 * DN_HEADS:4 * DN_HEADS].set(v.reshape(-1))
    return pl.pallas_call(
        _dprep_kernel,
        out_shape=jax.ShapeDtypeStruct((m, LANES), F32),
        grid=(m // ROW_TILE,),
        in_specs=[pl.BlockSpec((ROW_TILE, LANES), lambda i: (i, 4 * DN_W // LANES)),
                  pl.BlockSpec((1, LANES), lambda i: (0, 0)),
                  pl.BlockSpec((1, LANES), lambda i: (0, 0))],
        out_specs=pl.BlockSpec((ROW_TILE, LANES), lambda i: (i, 0)),
        compiler_params=_cparams(("parallel",), 32),
        name="delta_gates",
    )(p2d, lanes_vec(a_log), lanes_vec(dt_bias))


def _mm_pairs(xs, ys):
    xb = [x.astype(BF16) for x in xs]
    yb = [_block_diag2(y.astype(BF16)) for y in ys]
    return [jnp.dot(x, y, preferred_element_type=F32) for x, y in zip(xb, yb)]


def _block_diag2(y):
    yy = jnp.concatenate([y, y], axis=0)
    r = lax.broadcasted_iota(jnp.int32, (2 * CHUNK, LANES), 0)
    c = lax.broadcasted_iota(jnp.int32, (2 * CHUNK, LANES), 1)
    return jnp.where((r // CHUNK) == (c // CHUNK), yy, jnp.zeros_like(yy))


def _unit_tri_inverse2(a2s, row, col):
    eye2 = (row == col).astype(F32)
    same = lambda n: (row // n) == (col // n)
    a8 = [jnp.where(same(8), a2, 0.0) for a2 in a2s]
    p = _mm_pairs(a8, a8)
    yield
    t = [eye2 - a for a in a8]
    t = [x + y for x, y in zip(t, _mm_pairs(t, p))]
    yield
    p = _mm_pairs(p, p)
    yield
    t = [x + y for x, y in zip(t, _mm_pairs(t, p))]
    yield
    for n in (16, 32, 64):
        off = same(n) & jnp.logical_not(same(n // 2))
        tc = _mm_pairs(t, [jnp.where(off, a2, 0.0) for a2 in a2s])
        yield
        t = [x - y for x, y in zip(t, _mm_pairs(tc, t))]
        yield
    return t


def _delta_kernel(q_ref, k_ref, v_ref, z_ref, aux_ref, wq_ref, wk_ref, wv_ref, ng_ref, o_ref,
                  qs, ks, vs, bcs, kp_s, n_s, of_s, ob_s, *, seq_len):
    l = seq_len
    n_chunks = l // CHUNK
    ctx_chunks = CTX_LEN // CHUNK
    head = pl.program_id(1)
    halo = 8
    pad = (SHORT_CONV - 1) // 2
    crow = lax.broadcasted_iota(jnp.int32, (CHUNK, 1), 0)

    def conv_silu(x_ref, w_ref, c):
        r0 = c * CHUNK
        seg_lo, seg_hi = (0, CTX_LEN) if r0 < CTX_LEN else (CTX_LEN, l)
        lo, hi = max(r0 - halo, 0), min(r0 + CHUNK + halo, l)
        win = x_ref[0, lo:hi, :]
        zeros = jnp.zeros((halo, LANES), F32)
        if lo > r0 - halo:
            win = jnp.concatenate([zeros, win], axis=0)
        if hi < r0 + CHUNK + halo:
            win = jnp.concatenate([win, zeros], axis=0)
        w = w_ref[...]
        acc = None
        for j in range(SHORT_CONV):
            s = j - pad
            if 0 <= r0 + s and r0 + s + CHUNK <= l:
                piece = x_ref[0, r0 + s:r0 + s + CHUNK, :]
            else:
                piece = win[halo + s:halo + s + CHUNK]
            if r0 + s < seg_lo or r0 + CHUNK - 1 + s >= seg_hi:
                src = crow + (r0 + s)
                piece = jnp.where((src >= seg_lo) & (src < seg_hi), piece, 0.0)
            term = piece * w[j:j + 1]
            acc = term if acc is None else acc + term
        return acc * _sigmoid(acc)

    def l2norm(x):
        return x * lax.rsqrt(jnp.sum(x * x, axis=-1, keepdims=True) + EPS)

    alane = lax.broadcasted_iota(jnp.int32, (CHUNK, LANES), 1)
    prepared = set()

    def ensure_inputs(c):
        if c in prepared:
            return
        prepared.add(c)
        sl = slice(c * CHUNK, (c + 1) * CHUNK)
        qs[sl, :] = l2norm(conv_silu(q_ref, wq_ref, c)) * (DN_D ** -0.5)
        ks[sl, :] = l2norm(conv_silu(k_ref, wk_ref, c))
        vs[sl, :] = conv_silu(v_ref, wv_ref, c)
        aux = aux_ref[0, sl, :]
        for i in range(4):
            colv = jnp.sum(jnp.where(alane == head + i * DN_HEADS, aux, 0.0), axis=-1, keepdims=True)
            bcs[i, sl, :] = jnp.broadcast_to(colv, (CHUNK, LANES))

    def finish_chunk(c):
        sl = slice(c * CHUNK, (c + 1) * CHUNK)
        o = of_s[sl, :] + ob_s[sl, :]
        o = o * lax.rsqrt(jnp.mean(o * o, axis=-1, keepdims=True) + EPS) * ng_ref[...]
        z = z_ref[0, sl, :]
        o_ref[0, sl, :] = (o * (z * _sigmoid(z))).astype(o_ref.dtype)

    row = lax.broadcasted_iota(jnp.int32, (CHUNK, LANES), 0)
    lane = lax.broadcasted_iota(jnp.int32, (CHUNK, LANES), 1)
    fwd = lane < CHUNK
    col = jnp.where(fwd, lane, lane - CHUNK)
    sgn = jnp.where(fwd, 1, -1)
    incl = (row - col) * sgn >= 0
    strict = (row - col) * sgn > 0
    dot = functools.partial(jnp.dot, preferred_element_type=F32)
    nt = (((1,), (1,)), ((), ()))
    tn = (((0,), (0,)), ((), ()))
    acc_refs = (of_s, ob_s)

    def chunk_of(t, d):
        if d == 0:
            return t
        return ctx_chunks - 1 - t if t < ctx_chunks else n_chunks + ctx_chunks - 1 - t

    rows_of = lambda c: slice(c * CHUNK, (c + 1) * CHUNK)

    def g_last_row(c, d):
        r = c * CHUNK + (CHUNK - 1 if d == 0 else 0)
        return bcs[2 + d, r:r + 1, :]

    def prep(steps):
        data, a2s, qks = [], [], []
        for t in steps:
            sf, sb = rows_of(chunk_of(t, 0)), rows_of(chunk_of(t, 1))
            qkv = ((qs[sf, :], ks[sf, :], vs[sf, :]), (qs[sb, :], ks[sb, :], vs[sb, :]))
            beta = (bcs[0, sf, :], bcs[1, sb, :])
            gc = (bcs[2, sf, :], bcs[3, sb, :])
            data.append((qkv, beta, gc))
            b2 = jnp.where(fwd, beta[0], beta[1])
            c2 = jnp.where(fwd, gc[0], gc[1])
            r2 = jnp.sum(jnp.where(row == col, c2, 0.0), axis=0, keepdims=True)
            decay = jnp.where(incl, jnp.exp(jnp.where(incl, c2 - r2, 0.0)), 0.0)
            kst = jnp.concatenate([qkv[0][1], qkv[1][1]], axis=0).astype(BF16)
            qst = jnp.concatenate([qkv[0][0], qkv[1][0]], axis=0).astype(BF16)
            gram = lax.dot_general(kst, kst, nt, preferred_element_type=F32)
            qk = lax.dot_general(qst, kst, nt, preferred_element_type=F32)
            gram = jnp.where(fwd, gram[:CHUNK], gram[CHUNK:])
            qks.append(jnp.where(fwd, qk[:CHUNK], qk[CHUNK:]) * decay)
            a2s.append(jnp.where(strict, b2 * gram * decay, 0.0))
        yield
        t2s = yield from _unit_tri_inverse2(a2s, row, col)
        uws = []
        for (qkv, beta, gc), t2 in zip(data, t2s):
            rhs = jnp.concatenate(
                [jnp.concatenate([qkv[d][2] * beta[d], qkv[d][1] * (beta[d] * jnp.exp(gc[d]))], axis=1)
                 for d in range(2)], axis=0)
            uws.append(dot(_block_diag2(t2.astype(BF16)), rhs.astype(BF16)))
        yield
        for t, (qkv, beta, gc), qk, uw in zip(steps, data, qks, uws):
            for d in range(2):
                c = chunk_of(t, d)
                q, k, _ = qkv[d]
                uw_d = uw[d * CHUNK:(d + 1) * CHUNK].astype(BF16)
                kd = k * jnp.exp(g_last_row(c, d) - gc[d])
                kn = lax.dot_general(kd.astype(BF16), uw_d, tn, preferred_element_type=F32)
                qk_d = jnp.where(fwd == (d == 0), qk, 0.0).astype(BF16)
                po = dot(qk_d, jnp.concatenate([uw_d, uw_d], axis=0))
                p_mat = q * jnp.exp(gc[d]) - po[:, DN_D:]
                kp_s[t, d] = jnp.concatenate([kn[:, DN_D:], p_mat], axis=0).astype(BF16)
                n_s[t, d] = kn[:, :DN_D]
                acc_refs[d][rows_of(c), :] = po[:, :DN_D]
            yield

    state = [jnp.zeros((DN_D, DN_D), F32), jnp.zeros((DN_D, DN_D), F32)]
    visits = {}

    def scan(steps):
        for t in steps:
            for d in range(2):
                c = chunk_of(t, d)
                r = dot(kp_s[t, d], state[d].astype(BF16))
                acc_refs[d][rows_of(c), :] += r[DN_D:]
                state[d] = state[d] * jnp.exp(g_last_row(c, d)) - r[:DN_D] + n_s[t, d]
                visits[c] = visits.get(c, 0) + 1
                if visits[c] == 2:
                    finish_chunk(c)
            yield

    def interleave(gens):
        gens = list(gens)
        while gens:
            for g in list(gens):
                try:
                    next(g)
                except StopIteration:
                    gens.remove(g)

    def inputs(steps):
        for t in steps:
            for d in range(2):
                if chunk_of(t, d) not in prepared:
                    ensure_inputs(chunk_of(t, d))
                    yield

    assert sum(STEP_GROUPS) == n_chunks
    bounds = [sum(STEP_GROUPS[:i]) for i in range(len(STEP_GROUPS) + 1)]
    groups = [list(range(a, b)) for a, b in zip(bounds[:-1], bounds[1:])]
    stages = (inputs, prep, scan)
    for tick in range(len(groups) + len(stages) - 1):
        interleave([stage(groups[tick - k]) for k, stage in enumerate(stages) if 0 <= tick - k < len(groups)])
    assert all(v == 2 for v in visits.values()) and len(visits) == n_chunks


def deltanet_heads(p_main, aux, conv_w, norm_g):
    b, l, _ = p_main.shape
    n_chunks = l // CHUNK
    blk = lambda off: pl.BlockSpec((1, l, DN_D), lambda i, h, off=off: (i, 0, off + h))
    wblk = lambda off: pl.BlockSpec((SHORT_CONV, DN_D), lambda i, h, off=off: (0, off + h))
    seq = pltpu.VMEM((l, DN_D), F32)
    return pl.pallas_call(
        functools.partial(_delta_kernel, seq_len=l),
        out_shape=jax.ShapeDtypeStruct((b, l, DN_W), BF16),
        grid=(b, DN_HEADS),
        in_specs=[blk(0), blk(DN_HEADS), blk(2 * DN_HEADS), blk(3 * DN_HEADS),
                  pl.BlockSpec((1, l, LANES), lambda i, h: (i, 0, 0)),
                  wblk(0), wblk(DN_HEADS), wblk(2 * DN_HEADS),
                  pl.BlockSpec((1, DN_D), lambda i, h: (0, 0))],
        out_specs=pl.BlockSpec((1, l, DN_D), lambda i, h: (i, 0, h)),
        scratch_shapes=[seq, seq, seq,
                        pltpu.VMEM((4, l, LANES), F32),
                        pltpu.VMEM((n_chunks, 2, DN_D + CHUNK, DN_D), BF16),
                        pltpu.VMEM((n_chunks, 2, DN_D, DN_D), F32),
                        seq, seq],
        compiler_params=_cparams(("parallel", "parallel"), 48),
        name="deltanet_heads",
    )(p_main, p_main, p_main, p_main, aux, conv_w, conv_w, conv_w, norm_g.reshape(1, DN_D))


def _cf_kernel(val_lo_ref, val_hi_ref, gate_lo_ref, gate_hi_ref, w_ref, g_ref, b_ref, o_ref, cbuf, lbuf,
               *, seq_len, vertical):
    lat = seq_len - CTX_LEN
    half_lanes = LANES // 2
    low_half = lax.broadcasted_iota(jnp.int32, (seq_len, LANES), 1) < half_lanes

    def straddle(lo_ref, hi_ref):
        return pltpu.roll(jnp.where(low_half, hi_ref[0], lo_ref[0]), half_lanes, 1)

    gate = straddle(gate_lo_ref, gate_hi_ref)
    u = straddle(val_lo_ref, val_hi_ref) * _sigmoid(gate)
    w = w_ref[...]
    gain, bias = g_ref[...], b_ref[...]
    hpad = 16
    vpad = CF_PAD * GRID_W

    def finish(acc, row0):
        xc = acc - jnp.mean(acc, axis=-1, keepdims=True)
        y = xc * lax.rsqrt(jnp.mean(xc * xc, axis=-1, keepdims=True) + EPS) * gain + bias
        o_ref[0, pl.ds(row0, GRID_W), :] = (y * _sigmoid(y)).astype(o_ref.dtype)

    def window_conv(win, masked):
        r = lax.broadcasted_iota(jnp.int32, (GRID_W, 1), 0)
        acc = jnp.zeros((GRID_W, LANES), F32)
        for k in range(CF_KERNEL):
            off = hpad - CF_PAD + k
            piece = win[off:off + GRID_W]
            if masked:
                src = r + (k - CF_PAD)
                piece = jnp.where((src >= 0) & (src < GRID_W), piece, 0.0)
            acc = acc + piece * w[k:k + 1]
        return acc

    zeros_h = jnp.zeros((hpad, LANES), F32)
    cbuf[0:hpad, :] = zeros_h
    cbuf[hpad + CTX_LEN:, :] = zeros_h
    cbuf[hpad:hpad + CTX_LEN, :] = u[:CTX_LEN]

    def ctx_tile(g, carry):
        base = pl.multiple_of(g * GRID_W, GRID_W)
        finish(window_conv(cbuf[pl.ds(base, GRID_W + 2 * hpad), :], False), base)
        return carry

    lax.fori_loop(0, CTX_LEN // GRID_W, ctx_tile, 0, unroll=CF_TILE_UNROLL)

    if vertical:
        zeros_v = jnp.zeros((vpad, LANES), F32)
        lbuf[0:vpad, :] = zeros_v
        lbuf[vpad + lat:, :] = zeros_v
        lbuf[vpad:vpad + lat, :] = u[CTX_LEN:]

        def lat_tile(g, carry):
            acc = jnp.zeros((GRID_W, LANES), F32)
            for k in range(CF_KERNEL):
                base = pl.multiple_of((g + k) * GRID_W, GRID_W)
                acc = acc + lbuf[pl.ds(base, GRID_W), :] * w[k:k + 1]
            finish(acc, pl.multiple_of(CTX_LEN + g * GRID_W, GRID_W))
            return carry
    else:
        lbuf[0:hpad, :] = zeros_h
        lbuf[hpad + lat:, :] = zeros_h
        lbuf[hpad:hpad + lat, :] = u[CTX_LEN:]

        def lat_tile(g, carry):
            base = pl.multiple_of(g * GRID_W, GRID_W)
            finish(window_conv(lbuf[pl.ds(base, GRID_W + 2 * hpad), :], True),
                   pl.multiple_of(CTX_LEN + base, GRID_W))
            return carry

    lax.fori_loop(0, lat // GRID_W, lat_tile, 0, unroll=CF_TILE_UNROLL)


def conformer_heads(p, dw_w, ln_g, ln_b, vertical):
    b, l, _ = p.shape
    half = CF_W // 2
    lat = l - CTX_LEN
    first = half // LANES if vertical else 0
    cf_off = 4 * DN_W + 4 * DN_HEADS
    assert cf_off % LANES == LANES // 2 and CF_W % LANES == 0
    val0 = cf_off // LANES + first
    gate0 = (cf_off + CF_W) // LANES + first
    lrows = lat + (2 * CF_PAD * GRID_W if vertical else 32)
    chan = lambda rows: pl.BlockSpec((rows, LANES), lambda i, j: (0, first + j))
    col = lambda c0: pl.BlockSpec((1, l, LANES), lambda i, j: (i, 0, c0 + j))
    return pl.pallas_call(
        functools.partial(_cf_kernel, seq_len=l, vertical=vertical),
        out_shape=jax.ShapeDtypeStruct((b, l, half), BF16),
        grid=(b, half // LANES),
        in_specs=[col(val0), col(val0 + 1), col(gate0), col(gate0 + 1),
                  chan(CF_KERNEL), chan(1), chan(1)],
        out_specs=pl.BlockSpec((1, l, LANES), lambda i, j: (i, 0, j)),
        scratch_shapes=[pltpu.VMEM((CTX_LEN + 32, LANES), F32), pltpu.VMEM((lrows, LANES), F32)],
        compiler_params=_cparams(("parallel", "parallel"), 32),
        name="conformer_v" if vertical else "conformer_h",
    )(p, p, p, p, dw_w, ln_g.reshape(1, CF_W), ln_b.reshape(1, CF_W))


def _out_proj_kernel(dn_ref, ch_ref, cv_ref, w0_ref, w1_ref, w2_ref, x_ref, gl_ref, gc_ref, o_ref, *, tm):
    dot = functools.partial(jnp.dot, preferred_element_type=F32)
    acc = (dot(dn_ref[0], w0_ref[...].astype(BF16)) + dot(ch_ref[0], w1_ref[...].astype(BF16))
           + dot(cv_ref[0], w2_ref[...].astype(BF16)))
    row = pl.program_id(1) * tm + lax.broadcasted_iota(jnp.int32, (tm, 1), 0)
    gate = jnp.where(row < CTX_LEN, gc_ref[0], gl_ref[0])
    o_ref[0] = x_ref[0] + gate * acc


def out_proj_residual(dn, cfh, cfv, w_out, layer, xs, mods, gate_col):
    b, l, d = xs.shape
    tm, tn = l // 2, 512
    k0, k1 = dn.shape[-1], cfh.shape[-1]
    act = lambda kk: pl.BlockSpec((1, tm, kk), lambda i, t, j: (i, t, 0))
    gate = lambda r: pl.BlockSpec((1, 1, tn), lambda i, t, j, r=r: (i if r is None else r, 0, gate_col * (d // tn) + j))
    return pl.pallas_call(
        functools.partial(_out_proj_kernel, tm=tm),
        out_shape=jax.ShapeDtypeStruct((b, l, d), F32),
        grid=(b, l // tm, d // tn),
        in_specs=[act(k0), act(k1), act(k1),
                  pl.BlockSpec((None, k0, tn), lambda i, t, j: (layer, 0, j)),
                  pl.BlockSpec((None, k1, tn), lambda i, t, j: (layer, k0 // k1, j)),
                  pl.BlockSpec((None, k1, tn), lambda i, t, j: (layer, k0 // k1 + 1, j)),
                  pl.BlockSpec((1, tm, tn), lambda i, t, j: (i, t, j)),
                  gate(None), gate(4)],
        out_specs=pl.BlockSpec((1, tm, tn), lambda i, t, j: (i, t, j)),
        compiler_params=_cparams(("parallel", "parallel", "parallel"), 56),
        name="out_proj",
    )(dn, cfh, cfv, w_out, w_out, w_out, xs, mods, mods)


def _norm_router_kernel(x_ref, g_ref, sh_ref, sc_ref, rw_ref, rb_ref, h_ref, idx_ref, wt_ref):
    x = x_ref[0]
    y = x * lax.rsqrt(jnp.mean(x * x, axis=-1, keepdims=True) + EPS) * g_ref[...]
    h = y * (1.0 + sc_ref[0]) + sh_ref[0]
    h_ref[0] = h
    def split(v):
        hi = v.astype(BF16)
        return hi, (v - hi.astype(F32)).astype(BF16)

    nt_dot = lambda a, c: lax.dot_general(a, c, (((1,), (1,)), ((), ())), preferred_element_type=F32)
    (w_hi, w_lo), (h_hi, h_lo) = split(rw_ref[...]), split(h)
    logits = nt_dot(w_hi, h_hi) + (nt_dot(w_hi, h_lo) + nt_dot(w_lo, h_hi))
    scores = _sigmoid(logits)
    biased = scores + rb_ref[...]
    rows = [biased[e:e + 1] for e in range(N_EXPERTS)]
    srows = [scores[e:e + 1] for e in range(N_EXPERTS)]

    def pair_max(v):
        best = v[0] + v[1]
        for a in range(len(v)):
            for bb in range(a + 1, len(v)):
                if (a, bb) != (0, 1):
                    best = jnp.maximum(best, v[a] + v[bb])
        return best

    gscore = [pair_max(rows[g * 4:(g + 1) * 4]) for g in range(N_GROUPS)]
    best_g = jnp.zeros_like(gscore[0], dtype=jnp.int32)
    best_v = gscore[0]
    for g in range(1, N_GROUPS):
        better = gscore[g] > best_v
        best_g = jnp.where(better, g, best_g)
        best_v = jnp.where(better, gscore[g], best_v)

    def pick(vals):
        out = []
        for j in range(EXPERTS_PER_GROUP):
            m = vals[j]
            for g in range(1, N_GROUPS):
                m = jnp.where(best_g == g, vals[g * 4 + j], m)
            out.append(m)
        return out

    m, s = pick(rows), pick(srows)
    i1 = jnp.zeros_like(best_g)
    v1 = m[0]
    for j in range(1, 4):
        better = m[j] > v1
        i1 = jnp.where(better, j, i1)
        v1 = jnp.where(better, m[j], v1)
    i2 = jnp.full_like(best_g, -1)
    v2 = jnp.full_like(v1, -jnp.inf)
    for j in range(4):
        better = (i1 != j) & ((m[j] > v2) | (i2 < 0))
        i2 = jnp.where(better, j, i2)
        v2 = jnp.where(better, m[j], v2)

    def sel(vals, i):
        out = vals[0]
        for j in range(1, 4):
            out = jnp.where(i == j, vals[j], out)
        return out

    w1, w2 = sel(s, i1), sel(s, i2)
    tot = w1 + w2
    idx_ref[0:1, :] = best_g * EXPERTS_PER_GROUP + i1
    idx_ref[1:2, :] = best_g * EXPERTS_PER_GROUP + i2
    wt_ref[0:1, :] = w1 / tot
    wt_ref[1:2, :] = w2 / tot


def norm_router(xs, gain, mods, shift_col, router_w, router_bias):
    b, l, d = xs.shape
    nt = l // ROW_TILE
    tok = lambda i, t: (0, i * nt + t)
    return pl.pallas_call(
        _norm_router_kernel,
        out_shape=(jax.ShapeDtypeStruct((b, l, d), F32),
                   jax.ShapeDtypeStruct((2, b * l), jnp.int32),
                   jax.ShapeDtypeStruct((2, b * l), F32)),
        grid=(b, nt),
        in_specs=[pl.BlockSpec((1, ROW_TILE, d), lambda i, t: (i, t, 0)),
                  pl.BlockSpec((1, d), lambda i, t: (0, 0)),
                  pl.BlockSpec((1, 1, d), lambda i, t: (_mod_row(i, t), 0, shift_col)),
                  pl.BlockSpec((1, 1, d), lambda i, t: (_mod_row(i, t), 0, shift_col + 1)),
                  pl.BlockSpec((N_EXPERTS, d), lambda i, t: (0, 0)),
                  pl.BlockSpec((N_EXPERTS, 1), lambda i, t: (0, 0))],
        out_specs=(pl.BlockSpec((1, ROW_TILE, d), lambda i, t: (i, t, 0)),
                   pl.BlockSpec((2, ROW_TILE), tok),
                   pl.BlockSpec((2, ROW_TILE), tok)),
        compiler_params=_cparams(("parallel", "parallel"), 48),
        name="norm_router",
    )(xs, gain.reshape(1, d), mods, mods, router_w.T, router_bias.reshape(N_EXPERTS, 1))


def _moe_kernel(bexp_ref, stok_ref, sdst_ref, h_hbm, wg_hbm, wu_hbm, wd_hbm, y_hbm,
                xbuf, ybuf, wg_s, wu_s, wd_s, stage_a, stage_b, gsem, ssem, wsem, *, layer):
    i = pl.program_id(0)
    n = pl.num_programs(0)
    slot = lax.rem(i, 2)
    other = 1 - slot
    d, f = wg_s.shape

    def gather_start(block, buf):
        base = block * MOE_BLOCK
        for r in range(MOE_BLOCK):
            pltpu.make_async_copy(h_hbm.at[pl.ds(stok_ref[base + r], 1)], xbuf.at[buf, pl.ds(r, 1)],
                                  gsem.at[buf]).start()

    def gather_wait(buf):
        pltpu.make_async_copy(h_hbm.at[pl.ds(0, MOE_BLOCK)], xbuf.at[buf], gsem.at[buf]).wait()

    def scatter_start(block, buf):
        base = (block + 1) * MOE_BLOCK
        for r in range(MOE_BLOCK):
            pltpu.make_async_copy(ybuf.at[buf, pl.ds(r, 1)], y_hbm.at[pl.ds(sdst_ref[base + r], 1)],
                                  ssem.at[buf]).start()

    def scatter_wait(buf):
        pltpu.make_async_copy(ybuf.at[buf], y_hbm.at[pl.ds(0, MOE_BLOCK)], ssem.at[buf]).wait()

    def load_expert(e):
        rows_a, rows_b = stage_a.shape[1], stage_b.shape[1]
        chunks = []
        for w_hbm, w_s in ((wg_hbm, wg_s), (wu_hbm, wu_s)):
            for c in range(d // rows_a):
                rows = pl.ds(c * rows_a, rows_a)
                chunks.append((w_hbm.at[layer, e, rows], stage_a, w_s.at[rows]))
        for c in range(f // rows_b):
            rows = pl.ds(c * rows_b, rows_b)
            chunks.append((wd_hbm.at[layer, e, rows], stage_b, wd_s.at[rows]))
        copies = [pltpu.make_async_copy(src, stage.at[j % 2], wsem.at[j % 2])
                  for j, (src, stage, _) in enumerate(chunks)]
        copies[0].start()
        copies[1].start()
        for j, (_, stage, dst) in enumerate(chunks):
            copies[j].wait()
            dst[...] = stage[j % 2].astype(BF16)
            if j + 2 < len(chunks):
                copies[j + 2].start()

    @pl.when(i == 0)
    def _():
        gather_start(0, 0)
        ybuf[1] = jnp.zeros(ybuf.shape[1:], F32)

    @pl.when((i == 0) | (bexp_ref[i] != bexp_ref[jnp.maximum(i - 1, 0)]))
    def _():
        load_expert(bexp_ref[i])

    gather_wait(slot)

    @pl.when(i >= 1)
    def _():
        scatter_wait(slot)

    gather_start(jnp.minimum(i + 1, n - 1), other)
    scatter_start(i - 1, other)
    x = xbuf[slot].astype(BF16)
    dot = functools.partial(jnp.dot, preferred_element_type=F32)
    g = dot(x, wg_s[...])
    u = dot(x, wu_s[...])
    act = (g * _sigmoid(g) * u).astype(BF16)
    ybuf[slot] = dot(act, wd_s[...])

    @pl.when(i == n - 1)
    def _():
        scatter_start(i, slot)
        gather_wait(other)
        scatter_wait(other)
        scatter_wait(slot)


def moe_experts(h_tok, w_gate, w_up, w_down, layer, n_routed, block_expert, slot_tok, slot_dst):
    t, d = h_tok.shape
    n_blocks = block_expert.shape[0]
    f = w_gate.shape[-1]
    hbm = pl.BlockSpec(memory_space=pl.ANY)
    grid_spec = pltpu.PrefetchScalarGridSpec(
        num_scalar_prefetch=3,
        grid=(n_blocks,),
        in_specs=[hbm, hbm, hbm, hbm],
        out_specs=hbm,
        scratch_shapes=[pltpu.VMEM((2, MOE_BLOCK, d), F32), pltpu.VMEM((2, MOE_BLOCK, d), F32),
                        pltpu.VMEM((d, f), BF16), pltpu.VMEM((d, f), BF16), pltpu.VMEM((f, d), BF16),
                        pltpu.VMEM((2, W_STAGE_ELEMS // f, f), F32), pltpu.VMEM((2, W_STAGE_ELEMS // d, d), F32),
                        pltpu.SemaphoreType.DMA((2,)), pltpu.SemaphoreType.DMA((2,)),
                        pltpu.SemaphoreType.DMA((2,))],
    )
    return pl.pallas_call(
        functools.partial(_moe_kernel, layer=layer),
        out_shape=jax.ShapeDtypeStruct((2 * n_routed + MOE_DUMP_ROWS, d), F32),
        grid_spec=grid_spec,
        compiler_params=_cparams(("arbitrary",), 56),
        name="moe_experts",
    )(block_expert, slot_tok, slot_dst, h_tok, w_gate, w_up, w_down)


def route_slots(idx, seq_len, first_row):
    t = idx.shape[1]
    n = 2 * t
    kept = seq_len - first_row
    n_routed = t // seq_len * kept
    flat_e = idx.reshape(-1)
    flat_on = jnp.arange(n, dtype=jnp.int32) % seq_len >= first_row
    is_e = flat_e[:, None] == jnp.arange(N_EXPERTS)[None, :]
    onehot = (is_e & flat_on[:, None]).astype(jnp.int32)
    pieces = onehot.reshape(n // MOE_BLOCK, MOE_BLOCK, N_EXPERTS)
    inner = jnp.cumsum(pieces, axis=1)
    totals = inner[:, -1, :]
    csum = (inner + (jnp.cumsum(totals, axis=0) - totals)[:, None, :]).reshape(n, N_EXPERTS)
    rank = jnp.sum(jnp.where(is_e, csum, 0), axis=1) - 1
    counts = csum[-1]
    padded = (counts + MOE_BLOCK - 1) // MOE_BLOCK * MOE_BLOCK
    pad_end = jnp.cumsum(padded)
    n_blocks = 2 * n_routed // MOE_BLOCK + N_EXPERTS
    cap = n_blocks * MOE_BLOCK
    first_slot = jnp.sum(jnp.where(is_e, (pad_end - padded)[None, :], 0), axis=1)
    dest = jnp.where(flat_on, first_slot + rank, cap)
    slot_src = jnp.full((cap,), -1, jnp.int32).at[dest].set(jnp.arange(n, dtype=jnp.int32), mode='drop')
    starts = jnp.arange(n_blocks, dtype=jnp.int32) * MOE_BLOCK
    last_start = jnp.maximum(pad_end[-1] - MOE_BLOCK, 0)
    ended = pad_end[None, :] <= jnp.minimum(starts, last_start)[:, None]
    block_expert = jnp.minimum(jnp.sum(ended, axis=1), N_EXPERTS - 1).astype(jnp.int32)
    slot_tok = jnp.where(slot_src >= 0, slot_src % t, 0).astype(jnp.int32)
    compact = slot_tok // seq_len * kept + slot_tok % seq_len - first_row
    out_row = (slot_src // t) * n_routed + compact
    pos = jnp.arange(-MOE_BLOCK, cap, dtype=jnp.int32)
    keep = jnp.concatenate([jnp.zeros((MOE_BLOCK,), bool), slot_src >= 0])
    out_row = jnp.concatenate([jnp.zeros((MOE_BLOCK,), jnp.int32), out_row])
    slot_dst = jnp.where(keep, out_row, 2 * n_routed + pos % MOE_DUMP_ROWS).astype(jnp.int32)
    return block_expert, slot_tok, slot_dst


def _combine_kernel(x_ref, y0_ref, y1_ref, w_ref, g_ref, fg_ref, o_ref, *, final):
    w = w_ref[...]
    x = x_ref[0] + g_ref[0] * (y0_ref[...] * w[:, 0:1] + y1_ref[...] * w[:, 1:2])
    if final:
        x = x * lax.rsqrt(jnp.mean(x * x, axis=-1, keepdims=True) + EPS) * fg_ref[...]
    o_ref[0] = x


def moe_combine(xs, yk, wts, mods, gate_col, final_g, final):
    b, l, d = xs.shape
    nt = l // ROW_TILE
    t0 = 1 if final else 0
    n_out = nt - t0
    return pl.pallas_call(
        functools.partial(_combine_kernel, final=final),
        out_shape=jax.ShapeDtypeStruct((b, n_out * ROW_TILE, d), F32),
        grid=(b, n_out),
        in_specs=[pl.BlockSpec((1, ROW_TILE, d), lambda i, t: (i, t + t0, 0)),
                  pl.BlockSpec((ROW_TILE, d), lambda i, t: (i * n_out + t, 0)),
                  pl.BlockSpec((ROW_TILE, d), lambda i, t: (b * n_out + i * n_out + t, 0)),
                  pl.BlockSpec((ROW_TILE, 2), lambda i, t: (i * nt + t + t0, 0)),
                  pl.BlockSpec((1, 1, d), lambda i, t: (_mod_row(i, t + t0), 0, gate_col)),
                  pl.BlockSpec((1, d), lambda i, t: (0, 0))],
        out_specs=pl.BlockSpec((1, ROW_TILE, d), lambda i, t: (i, t, 0)),
        compiler_params=_cparams(("parallel", "parallel"), 48),
        name="moe_combine",
    )(xs, yk, yk, wts, mods, final_g.reshape(1, d))


def _layer(xs, layer, mods, norm1_g, norm2_g, w_in, conv_qkv_w, a_log, dt_bias, dn_norm_g, cf_dw_w,
           cf_ln_g, cf_ln_b, w_out, router_w, router_bias, w_gate, w_up, w_down, final_g, final):
    b, l, d = xs.shape
    in_cols = w_in.shape[-1]

    h = norm_modulate(xs, norm1_g, mods, 0, BF16).reshape(b * l, d)
    p2d = matmul_nt(h, jnp.swapaxes(w_in, 1, 2), layer, 1536, 512)
    p = p2d.reshape(b, l, in_cols)
    aux = delta_gates(p2d, a_log, dt_bias).reshape(b, l, LANES)
    dn = deltanet_heads(p, aux, conv_qkv_w, dn_norm_g)
    cfh = conformer_heads(p, cf_dw_w, cf_ln_g, cf_ln_b, False)
    cfv = conformer_heads(p, cf_dw_w, cf_ln_g, cf_ln_b, True)
    xs = out_proj_residual(dn, cfh, cfv, w_out, layer, xs, mods, 2)

    h2, idx, wts = norm_router(xs, norm2_g, mods, 3, router_w, router_bias)
    first_row = CTX_LEN if final else 0
    n_routed = b * (l - first_row)
    block_expert, slot_tok, slot_dst = route_slots(idx, l, first_row)
    yk = moe_experts(h2.reshape(b * l, d), w_gate, w_up, w_down, layer, n_routed, block_expert, slot_tok,
                     slot_dst)
    return moe_combine(xs, yk, wts.T, mods, 5, final_g, final)


def kernel(x, c, ctx, c_ctx, w_ada, b_ada, norm1_g, norm2_g, w_in, conv_qkv_w, a_log, dt_bias,
           dn_norm_g, cf_dw_w, cf_ln_g, cf_ln_b, w_out, router_w, router_bias, w_gate, w_up,
           w_down, final_g):
    depth = w_ada.shape[0]
    bsz = x.shape[0]
    xs = jnp.concatenate([ctx, x], axis=1)
    c_all = jnp.concatenate([c, c_ctx[None, :], jnp.zeros((8 - bsz - 1, c.shape[1]), F32)], axis=0)
    mods = ada_mods(c_all, w_ada, b_ada)
    for i in range(depth):
        mods_i = mods[i].reshape(8, 1, -1)
        xs = _layer(xs, i, mods_i, norm1_g[i], norm2_g[i], w_in, conv_qkv_w[i], a_log[i], dt_bias[i],
                    dn_norm_g[i], cf_dw_w[i], cf_ln_g[i], cf_ln_b[i], w_out, router_w, router_bias,
                    w_gate, w_up, w_down, final_g, i == depth - 1)
    return xs
```

```python
import functools

import jax
import jax.numpy as jnp
from jax import lax
from jax.experimental import pallas as pl
from jax.experimental.pallas import tpu as pltpu

F32 = jnp.float32
BF16 = jnp.bfloat16

D_MODEL = 4096
CTX_LEN = 256
GRID_W = 64
EPS = 1e-6
DN_HEADS = 16
DN_D = 128
DN_W = DN_HEADS * DN_D
SHORT_CONV = 5
CHUNK = 64
CF_W = 2048
CF_KERNEL = 31
CF_PAD = (CF_KERNEL - 1) // 2
N_EXPERTS = 16
N_GROUPS = 4
EXPERTS_PER_GROUP = 4
D_FF = 1024
MOE_BLOCK = 128
MOE_DUMP_ROWS = 2 * MOE_BLOCK
W_STAGE_ELEMS = 512 * 1024
LANES = 128
ROW_TILE = 256
STEP_GROUPS = (12, 12, 12)
CF_TILE_UNROLL = 8
MIB = 1024 * 1024


def _cparams(sem, vmem_mib):
    return pltpu.CompilerParams(dimension_semantics=sem, vmem_limit_bytes=vmem_mib * MIB)


def _sigmoid(x):
    return 1.0 / (1.0 + jnp.exp(-x))


def _ada_kernel(c_ref, w_ref, b_ref, o_ref):
    c = c_ref[...]
    act = (c * _sigmoid(c)).astype(BF16)
    o_ref[0] = jnp.dot(act, w_ref[0].astype(BF16), preferred_element_type=F32) + b_ref[0]


def ada_mods(c_all, w_ada, b_ada):
    depth, d, n = w_ada.shape
    tn = 512
    return pl.pallas_call(
        _ada_kernel,
        out_shape=jax.ShapeDtypeStruct((depth, 8, n), F32),
        grid=(depth, n // tn),
        in_specs=[pl.BlockSpec((8, d), lambda i, j: (0, 0)),
                  pl.BlockSpec((1, d, tn), lambda i, j: (i, 0, j)),
                  pl.BlockSpec((1, 1, tn), lambda i, j: (i, 0, j))],
        out_specs=pl.BlockSpec((1, 8, tn), lambda i, j: (i, 0, j)),
        compiler_params=_cparams(("parallel", "parallel"), 40),
        name="ada_mods",
    )(c_all, w_ada, b_ada.reshape(depth, 1, n))


def _mod_row(b, t):
    return jnp.where(t == 0, 4, b)


def _norm_mod_kernel(x_ref, g_ref, sh_ref, sc_ref, o_ref):
    x = x_ref[0]
    y = x * lax.rsqrt(jnp.mean(x * x, axis=-1, keepdims=True) + EPS) * g_ref[...]
    o_ref[0] = (y * (1.0 + sc_ref[0]) + sh_ref[0]).astype(o_ref.dtype)


def norm_modulate(xs, gain, mods, shift_col, out_dtype):
    b, l, d = xs.shape
    return pl.pallas_call(
        _norm_mod_kernel,
        out_shape=jax.ShapeDtypeStruct((b, l, d), out_dtype),
        grid=(b, l // ROW_TILE),
        in_specs=[pl.BlockSpec((1, ROW_TILE, d), lambda i, t: (i, t, 0)),
                  pl.BlockSpec((1, d), lambda i, t: (0, 0)),
                  pl.BlockSpec((1, 1, d), lambda i, t: (_mod_row(i, t), 0, shift_col)),
                  pl.BlockSpec((1, 1, d), lambda i, t: (_mod_row(i, t), 0, shift_col + 1))],
        out_specs=pl.BlockSpec((1, ROW_TILE, d), lambda i, t: (i, t, 0)),
        compiler_params=_cparams(("parallel", "parallel"), 40),
        name="norm_modulate",
    )(xs, gain.reshape(1, d), mods, mods)


def _mm_nt_kernel(a_ref, w_ref, o_ref):
    w = w_ref[...].astype(BF16)
    o_ref[...] = lax.dot_general(a_ref[...], w, (((1,), (1,)), ((), ())), preferred_element_type=F32)


def matmul_nt(a, w_t, layer, tm, tn):
    m, k = a.shape
    n = w_t.shape[1]
    return pl.pallas_call(
        _mm_nt_kernel,
        out_shape=jax.ShapeDtypeStruct((m, n), F32),
        grid=(m // tm, pl.cdiv(n, tn)),
        in_specs=[pl.BlockSpec((tm, k), lambda i, j: (i, 0)),
                  pl.BlockSpec((None, tn, k), lambda i, j: (layer, j, 0))],
        out_specs=pl.BlockSpec((tm, tn), lambda i, j: (i, j)),
        compiler_params=_cparams(("parallel", "parallel"), 56),
        name="in_proj",
    )(a, w_t)


def _dprep_kernel(p_ref, alog_ref, dtb_ref, o_ref):
    raw = p_ref[...]
    beta = _sigmoid(raw)
    g = -jnp.exp(alog_ref[...]) * jax.nn.softplus(raw + dtb_ref[...])
    r = lax.broadcasted_iota(jnp.int32, (CHUNK, CHUNK), 0)
    c = lax.broadcasted_iota(jnp.int32, (CHUNK, CHUNK), 1)
    tri_lo = (c <= r).astype(F32)
    tri_up = (c >= r).astype(F32)
    lane = lax.broadcasted_iota(jnp.int32, (CHUNK, LANES), 1)
    for ch in range(ROW_TILE // CHUNK):
        sl = slice(ch * CHUNK, (ch + 1) * CHUNK)
        gch = g[sl]
        cum_f = jnp.dot(tri_lo, gch, preferred_element_type=F32, precision=lax.Precision.HIGHEST)
        cum_b = jnp.dot(tri_up, gch, preferred_element_type=F32, precision=lax.Precision.HIGHEST)
        gc = jnp.where(lane < 3 * DN_HEADS, cum_f, cum_b)
        o_ref[sl, :] = jnp.where(lane < 2 * DN_HEADS, beta[sl], gc)


def delta_gates(p2d, a_log, dt_bias):
    m = p2d.shape[0]
    lanes_vec = lambda v: jnp.zeros((1, LANES), F32).at[0, 2 * DN_HEADS:4 * DN_HEADS].set(v.reshape(-1))
    return pl.pallas_call(
        _dprep_kernel,
        out_shape=jax.ShapeDtypeStruct((m, LANES), F32),
        grid=(m // ROW_TILE,),
        in_specs=[pl.BlockSpec((ROW_TILE, LANES), lambda i: (i, 4 * DN_W // LANES)),
                  pl.BlockSpec((1, LANES), lambda i: (0, 0)),
                  pl.BlockSpec((1, LANES), lambda i: (0, 0))],
        out_specs=pl.BlockSpec((ROW_TILE, LANES), lambda i: (i, 0)),
        compiler_params=_cparams(("parallel",), 32),
        name="delta_gates",
    )(p2d, lanes_vec(a_log), lanes_vec(dt_bias))


def _mm_pairs(xs, ys):
    xb = [x.astype(BF16) for x in xs]
    yb = [_block_diag2(y.astype(BF16)) for y in ys]
    return [jnp.dot(x, y, preferred_element_type=F32) for x, y in zip(xb, yb)]


def _block_diag2(y):
    yy = jnp.concatenate([y, y], axis=0)
    r = lax.broadcasted_iota(jnp.int32, (2 * CHUNK, LANES), 0)
    c = lax.broadcasted_iota(jnp.int32, (2 * CHUNK, LANES), 1)
    return jnp.where((r // CHUNK) == (c // CHUNK), yy, jnp.zeros_like(yy))


def _unit_tri_inverse2(a2s, row, col):
    eye2 = (row == col).astype(F32)
    same = lambda n: (row // n) == (col // n)
    a8 = [jnp.where(same(8), a2, 0.0) for a2 in a2s]
    p = _mm_pairs(a8, a8)
    yield
    t = [eye2 - a for a in a8]
    t = [x + y for x, y in zip(t, _mm_pairs(t, p))]
    yield
    p = _mm_pairs(p, p)
    yield
    t = [x + y for x, y in zip(t, _mm_pairs(t, p))]
    yield
    for n in (16, 32, 64):
        off = same(n) & jnp.logical_not(same(n // 2))
        tc = _mm_pairs(t, [jnp.where(off, a2, 0.0) for a2 in a2s])
        yield
        t = [x - y for x, y in zip(t, _mm_pairs(tc, t))]
        yield
    return t


def _delta_kernel(q_ref, k_ref, v_ref, z_ref, aux_ref, wq_ref, wk_ref, wv_ref, ng_ref, o_ref,
                  qs, ks, vs, bcs, kp_s, n_s, of_s, ob_s, *, seq_len):
    l = seq_len
    n_chunks = l // CHUNK
    ctx_chunks = CTX_LEN // CHUNK
    head = pl.program_id(1)
    halo = 8
    pad = (SHORT_CONV - 1) // 2
    crow = lax.broadcasted_iota(jnp.int32, (CHUNK, 1), 0)

    def conv_silu(x_ref, w_ref, c):
        r0 = c * CHUNK
        seg_lo, seg_hi = (0, CTX_LEN) if r0 < CTX_LEN else (CTX_LEN, l)
        lo, hi = max(r0 - halo, 0), min(r0 + CHUNK + halo, l)
        win = x_ref[0, lo:hi, :]
        zeros = jnp.zeros((halo, LANES), F32)
        if lo > r0 - halo:
            win = jnp.concatenate([zeros, win], axis=0)
        if hi < r0 + CHUNK + halo:
            win = jnp.concatenate([win, zeros], axis=0)
        w = w_ref[...]
        acc = None
        for j in range(SHORT_CONV):
            s = j - pad
            if 0 <= r0 + s and r0 + s + CHUNK <= l:
                piece = x_ref[0, r0 + s:r0 + s + CHUNK, :]
            else:
                piece = win[halo + s:halo + s + CHUNK]
            if r0 + s < seg_lo or r0 + CHUNK - 1 + s >= seg_hi:
                src = crow + (r0 + s)
                piece = jnp.where((src >= seg_lo) & (src < seg_hi), piece, 0.0)
            term = piece * w[j:j + 1]
            acc = term if acc is None else acc + term
        return acc * _sigmoid(acc)

    def l2norm(x):
        return x * lax.rsqrt(jnp.sum(x * x, axis=-1, keepdims=True) + EPS)

    alane = lax.broadcasted_iota(jnp.int32, (CHUNK, LANES), 1)
    prepared = set()

    def ensure_inputs(c):
        if c in prepared:
            return
        prepared.add(c)
        sl = slice(c * CHUNK, (c + 1) * CHUNK)
        qs[sl, :] = l2norm(conv_silu(q_ref, wq_ref, c)) * (DN_D ** -0.5)
        ks[sl, :] = l2norm(conv_silu(k_ref, wk_ref, c))
        vs[sl, :] = conv_silu(v_ref, wv_ref, c)
        aux = aux_ref[0, sl, :]
        for i in range(4):
            colv = jnp.sum(jnp.where(alane == head + i * DN_HEADS, aux, 0.0), axis=-1, keepdims=True)
            bcs[i, sl, :] = jnp.broadcast_to(colv, (CHUNK, LANES))

    def finish_chunk(c):
        sl = slice(c * CHUNK, (c + 1) * CHUNK)
        o = of_s[sl, :] + ob_s[sl, :]
        o = o * lax.rsqrt(jnp.mean(o * o, axis=-1, keepdims=True) + EPS) * ng_ref[...]
        z = z_ref[0, sl, :]
        o_ref[0, sl, :] = (o * (z * _sigmoid(z))).astype(o_ref.dtype)

    row = lax.broadcasted_iota(jnp.int32, (CHUNK, LANES), 0)
    lane = lax.broadcasted_iota(jnp.int32, (CHUNK, LANES), 1)
    fwd = lane < CHUNK
    col = jnp.where(fwd, lane, lane - CHUNK)
    sgn = jnp.where(fwd, 1, -1)
    incl = (row - col) * sgn >= 0
    strict = (row - col) * sgn > 0
    dot = functools.partial(jnp.dot, preferred_element_type=F32)
    nt = (((1,), (1,)), ((), ()))
    tn = (((0,), (0,)), ((), ()))
    acc_refs = (of_s, ob_s)

    def chunk_of(t, d):
        if d == 0:
            return t
        return ctx_chunks - 1 - t if t < ctx_chunks else n_chunks + ctx_chunks - 1 - t

    rows_of = lambda c: slice(c * CHUNK, (c + 1) * CHUNK)

    def g_last_row(c, d):
        r = c * CHUNK + (CHUNK - 1 if d == 0 else 0)
        return bcs[2 + d, r:r + 1, :]

    def prep(steps):
        data, a2s, qks = [], [], []
        for t in steps:
            sf, sb = rows_of(chunk_of(t, 0)), rows_of(chunk_of(t, 1))
            qkv = ((qs[sf, :], ks[sf, :], vs[sf, :]), (qs[sb, :], ks[sb, :], vs[sb, :]))
            beta = (bcs[0, sf, :], bcs[1, sb, :])
            gc = (bcs[2, sf, :], bcs[3, sb, :])
            data.append((qkv, beta, gc))
            b2 = jnp.where(fwd, beta[0], beta[1])
            c2 = jnp.where(fwd, gc[0], gc[1])
            r2 = jnp.sum(jnp.where(row == col, c2, 0.0), axis=0, keepdims=True)
            decay = jnp.where(incl, jnp.exp(jnp.where(incl, c2 - r2, 0.0)), 0.0)
            kst = jnp.concatenate([qkv[0][1], qkv[1][1]], axis=0).astype(BF16)
            qst = jnp.concatenate([qkv[0][0], qkv[1][0]], axis=0).astype(BF16)
            gram = lax.dot_general(kst, kst, nt, preferred_element_type=F32)
            qk = lax.dot_general(qst, kst, nt, preferred_element_type=F32)
            gram = jnp.where(fwd, gram[:CHUNK], gram[CHUNK:])
            qks.append(jnp.where(fwd, qk[:CHUNK], qk[CHUNK:]) * decay)
            a2s.append(jnp.where(strict, b2 * gram * decay, 0.0))
        yield
        t2s = yield from _unit_tri_inverse2(a2s, row, col)
        uws = []
        for (qkv, beta, gc), t2 in zip(data, t2s):
            rhs = jnp.concatenate(
                [jnp.concatenate([qkv[d][2] * beta[d], qkv[d][1] * (beta[d] * jnp.exp(gc[d]))], axis=1)
                 for d in range(2)], axis=0)
            uws.append(dot(_block_diag2(t2.astype(BF16)), rhs.astype(BF16)))
        yield
        for t, (qkv, beta, gc), qk, uw in zip(steps, data, qks, uws):
            for d in range(2):
                c = chunk_of(t, d)
                q, k, _ = qkv[d]
                uw_d = uw[d * CHUNK:(d + 1) * CHUNK].astype(BF16)
                kd = k * jnp.exp(g_last_row(c, d) - gc[d])
                kn = lax.dot_general(kd.astype(BF16), uw_d, tn, preferred_element_type=F32)
                qk_d = jnp.where(fwd == (d == 0), qk, 0.0).astype(BF16)
                po = dot(qk_d, jnp.concatenate([uw_d, uw_d], axis=0))
                p_mat = q * jnp.exp(gc[d]) - po[:, DN_D:]
                kp_s[t, d] = jnp.concatenate([kn[:, DN_D:], p_mat], axis=0).astype(BF16)
                n_s[t, d] = kn[:, :DN_D]
                acc_refs[d][rows_of(c), :] = po[:, :DN_D]
            yield

    state = [jnp.zeros((DN_D, DN_D), F32), jnp.zeros((DN_D, DN_D), F32)]
    visits = {}

    def scan(steps):
        for t in steps:
            for d in range(2):
                c = chunk_of(t, d)
                r = dot(kp_s[t, d], state[d].astype(BF16))
                acc_refs[d][rows_of(c), :] += r[DN_D:]
                state[d] = state[d] * jnp.exp(g_last_row(c, d)) - r[:DN_D] + n_s[t, d]
                visits[c] = visits.get(c, 0) + 1
                if visits[c] == 2:
                    finish_chunk(c)
            yield

    def interleave(gens):
        gens = list(gens)
        while gens:
            for g in list(gens):
                try:
                    next(g)
                except StopIteration:
                    gens.remove(g)

    def inputs(steps):
        for t in steps:
            for d in range(2):
                if chunk_of(t, d) not in prepared:
                    ensure_inputs(chunk_of(t, d))
                    yield

    assert sum(STEP_GROUPS) == n_chunks
    bounds = [sum(STEP_GROUPS[:i]) for i in range(len(STEP_GROUPS) + 1)]
    groups = [list(range(a, b)) for a, b in zip(bounds[:-1], bounds[1:])]
    stages = (inputs, prep, scan)
    for tick in range(len(groups) + len(stages) - 1):
        interleave([stage(groups[tick - k]) for k, stage in enumerate(stages) if 0 <= tick - k < len(groups)])
    assert all(v == 2 for v in visits.values()) and len(visits) == n_chunks


def deltanet_heads(p_main, aux, conv_w, norm_g):
    b, l, _ = p_main.shape
    n_chunks = l // CHUNK
    blk = lambda off: pl.BlockSpec((1, l, DN_D), lambda i, h, off=off: (i, 0, off + h))
    wblk = lambda off: pl.BlockSpec((SHORT_CONV, DN_D), lambda i, h, off=off: (0, off + h))
    seq = pltpu.VMEM((l, DN_D), F32)
    return pl.pallas_call(
        functools.partial(_delta_kernel, seq_len=l),
        out_shape=jax.ShapeDtypeStruct((b, l, DN_W), BF16),
        grid=(b, DN_HEADS),
        in_specs=[blk(0), blk(DN_HEADS), blk(2 * DN_HEADS), blk(3 * DN_HEADS),
                  pl.BlockSpec((1, l, LANES), lambda i, h: (i, 0, 0)),
                  wblk(0), wblk(DN_HEADS), wblk(2 * DN_HEADS),
                  pl.BlockSpec((1, DN_D), lambda i, h: (0, 0))],
        out_specs=pl.BlockSpec((1, l, DN_D), lambda i, h: (i, 0, h)),
        scratch_shapes=[seq, seq, seq,
                        pltpu.VMEM((4, l, LANES), F32),
                        pltpu.VMEM((n_chunks, 2, DN_D + CHUNK, DN_D), BF16),
                        pltpu.VMEM((n_chunks, 2, DN_D, DN_D), F32),
                        seq, seq],
        compiler_params=_cparams(("parallel", "parallel"), 48),
        name="deltanet_heads",
    )(p_main, p_main, p_main, p_main, aux, conv_w, conv_w, conv_w, norm_g.reshape(1, DN_D))


def _cf_kernel(val_lo_ref, val_hi_ref, gate_lo_ref, gate_hi_ref, w_ref, g_ref, b_ref, o_ref, cbuf, lbuf,
               *, seq_len, vertical):
    lat = seq_len - CTX_LEN
    half_lanes = LANES // 2
    low_half = lax.broadcasted_iota(jnp.int32, (seq_len, LANES), 1) < half_lanes

    def straddle(lo_ref, hi_ref):
        return pltpu.roll(jnp.where(low_half, hi_ref[0], lo_ref[0]), half_lanes, 1)

    gate = straddle(gate_lo_ref, gate_hi_ref)
    u = straddle(val_lo_ref, val_hi_ref) * _sigmoid(gate)
    w = w_ref[...]
    gain, bias = g_ref[...], b_ref[...]
    hpad = 16
    vpad = CF_PAD * GRID_W

    def finish(acc, row0):
        xc = acc - jnp.mean(acc, axis=-1, keepdims=True)
        y = xc * lax.rsqrt(jnp.mean(xc * xc, axis=-1, keepdims=True) + EPS) * gain + bias
        o_ref[0, pl.ds(row0, GRID_W), :] = (y * _sigmoid(y)).astype(o_ref.dtype)

    def window_conv(win, masked):
        r = lax.broadcasted_iota(jnp.int32, (GRID_W, 1), 0)
        acc = jnp.zeros((GRID_W, LANES), F32)
        for k in range(CF_KERNEL):
            off = hpad - CF_PAD + k
            piece = win[off:off + GRID_W]
            if masked:
                src = r + (k - CF_PAD)
                piece = jnp.where((src >= 0) & (src < GRID_W), piece, 0.0)
            acc = acc + piece * w[k:k + 1]
        return acc

    zeros_h = jnp.zeros((hpad, LANES), F32)
    cbuf[0:hpad, :] = zeros_h
    cbuf[hpad + CTX_LEN:, :] = zeros_h
    cbuf[hpad:hpad + CTX_LEN, :] = u[:CTX_LEN]

    def ctx_tile(g, carry):
        base = pl.multiple_of(g * GRID_W, GRID_W)
        finish(window_conv(cbuf[pl.ds(base, GRID_W + 2 * hpad), :], False), base)
        return carry

    lax.fori_loop(0, CTX_LEN // GRID_W, ctx_tile, 0, unroll=CF_TILE_UNROLL)

    if vertical:
        zeros_v = jnp.zeros((vpad, LANES), F32)
        lbuf[0:vpad, :] = zeros_v
        lbuf[vpad + lat:, :] = zeros_v
        lbuf[vpad:vpad + lat, :] = u[CTX_LEN:]

        def lat_tile(g, carry):
            acc = jnp.zeros((GRID_W, LANES), F32)
            for k in range(CF_KERNEL):
                base = pl.multiple_of((g + k) * GRID_W, GRID_W)
                acc = acc + lbuf[pl.ds(base, GRID_W), :] * w[k:k + 1]
            finish(acc, pl.multiple_of(CTX_LEN + g * GRID_W, GRID_W))
            return carry
    else:
        lbuf[0:hpad, :] = zeros_h
        lbuf[hpad + lat:, :] = zeros_h
        lbuf[hpad:hpad + lat, :] = u[CTX_LEN:]

        def lat_tile(g, carry):
            base = pl.multiple_of(g * GRID_W, GRID_W)
            finish(window_conv(lbuf[pl.ds(base, GRID_W + 2 * hpad), :], True),
                   pl.multiple_of(CTX_LEN + base, GRID_W))
            return carry

    lax.fori_loop(0, lat // GRID_W, lat_tile, 0, unroll=CF_TILE_UNROLL)


def conformer_heads(p, dw_w, ln_g, ln_b, vertical):
    b, l, _ = p.shape
    half = CF_W // 2
    lat = l - CTX_LEN
    first = half // LANES if vertical else 0
    cf_off = 4 * DN_W + 4 * DN_HEADS
    assert cf_off % LANES == LANES // 2 and CF_W % LANES == 0
    val0 = cf_off // LANES + first
    gate0 = (cf_off + CF_W) // LANES + first
    lrows = lat + (2 * CF_PAD * GRID_W if vertical else 32)
    chan = lambda rows: pl.BlockSpec((rows, LANES), lambda i, j: (0, first + j))
    col = lambda c0: pl.BlockSpec((1, l, LANES), lambda i, j: (i, 0, c0 + j))
    return pl.pallas_call(
        functools.partial(_cf_kernel, seq_len=l, vertical=vertical),
        out_shape=jax.ShapeDtypeStruct((b, l, half), BF16),
        grid=(b, half // LANES),
        in_specs=[col(val0), col(val0 + 1), col(gate0), col(gate0 + 1),
                  chan(CF_KERNEL), chan(1), chan(1)],
        out_specs=pl.BlockSpec((1, l, LANES), lambda i, j: (i, 0, j)),
        scratch_shapes=[pltpu.VMEM((CTX_LEN + 32, LANES), F32), pltpu.VMEM((lrows, LANES), F32)],
        compiler_params=_cparams(("parallel", "parallel"), 32),
        name="conformer_v" if vertical else "conformer_h",
    )(p, p, p, p, dw_w, ln_g.reshape(1, CF_W), ln_b.reshape(1, CF_W))


def _out_proj_kernel(dn_ref, ch_ref, cv_ref, w0_ref, w1_ref, w2_ref, x_ref, gl_ref, gc_ref, o_ref, *, tm):
    dot = functools.partial(jnp.dot, preferred_element_type=F32)
    acc = (dot(dn_ref[0], w0_ref[...].astype(BF16)) + dot(ch_ref[0], w1_ref[...].astype(BF16))
           + dot(cv_ref[0], w2_ref[...].astype(BF16)))
    row = pl.program_id(1) * tm + lax.broadcasted_iota(jnp.int32, (tm, 1), 0)
    gate = jnp.where(row < CTX_LEN, gc_ref[0], gl_ref[0])
    o_ref[0] = x_ref[0] + gate * acc


def out_proj_residual(dn, cfh, cfv, w_out, layer, xs, mods, gate_col):
    b, l, d = xs.shape
    tm, tn = l // 2, 512
    k0, k1 = dn.shape[-1], cfh.shape[-1]
    act = lambda kk: pl.BlockSpec((1, tm, kk), lambda i, t, j: (i, t, 0))
    gate = lambda r: pl.BlockSpec((1, 1, tn), lambda i, t, j, r=r: (i if r is None else r, 0, gate_col * (d // tn) + j))
    return pl.pallas_call(
        functools.partial(_out_proj_kernel, tm=tm),
        out_shape=jax.ShapeDtypeStruct((b, l, d), F32),
        grid=(b, l // tm, d // tn),
        in_specs=[act(k0), act(k1), act(k1),
                  pl.BlockSpec((None, k0, tn), lambda i, t, j: (layer, 0, j)),
                  pl.BlockSpec((None, k1, tn), lambda i, t, j: (layer, k0 // k1, j)),
                  pl.BlockSpec((None, k1, tn), lambda i, t, j: (layer, k0 // k1 + 1, j)),
                  pl.BlockSpec((1, tm, tn), lambda i, t, j: (i, t, j)),
                  gate(None), gate(4)],
        out_specs=pl.BlockSpec((1, tm, tn), lambda i, t, j: (i, t, j)),
        compiler_params=_cparams(("parallel", "parallel", "parallel"), 56),
        name="out_proj",
    )(dn, cfh, cfv, w_out, w_out, w_out, xs, mods, mods)


def _norm_router_kernel(x_ref, g_ref, sh_ref, sc_ref, rw_ref, rb_ref, h_ref, idx_ref, wt_ref):
    x = x_ref[0]
    y = x * lax.rsqrt(jnp.mean(x * x, axis=-1, keepdims=True) + EPS) * g_ref[...]
    h = y * (1.0 + sc_ref[0]) + sh_ref[0]
    h_ref[0] = h
    def split(v):
        hi = v.astype(BF16)
        return hi, (v - hi.astype(F32)).astype(BF16)

    nt_dot = lambda a, c: lax.dot_general(a, c, (((1,), (1,)), ((), ())), preferred_element_type=F32)
    (w_hi, w_lo), (h_hi, h_lo) = split(rw_ref[...]), split(h)
    logits = nt_dot(w_hi, h_hi) + (nt_dot(w_hi, h_lo) + nt_dot(w_lo, h_hi))
    scores = _sigmoid(logits)
    biased = scores + rb_ref[...]
    rows = [biased[e:e + 1] for e in range(N_EXPERTS)]
    srows = [scores[e:e + 1] for e in range(N_EXPERTS)]

    def pair_max(v):
        best = v[0] + v[1]
        for a in range(len(v)):
            for bb in range(a + 1, len(v)):
                if (a, bb) != (0, 1):
                    best = jnp.maximum(best, v[a] + v[bb])
        return best

    gscore = [pair_max(rows[g * 4:(g + 1) * 4]) for g in range(N_GROUPS)]
    best_g = jnp.zeros_like(gscore[0], dtype=jnp.int32)
    best_v = gscore[0]
    for g in range(1, N_GROUPS):
        better = gscore[g] > best_v
        best_g = jnp.where(better, g, best_g)
        best_v = jnp.where(better, gscore[g], best_v)

    def pick(vals):
        out = []
        for j in range(EXPERTS_PER_GROUP):
            m = vals[j]
            for g in range(1, N_GROUPS):
                m = jnp.where(best_g == g, vals[g * 4 + j], m)
            out.append(m)
        return out

    m, s = pick(rows), pick(srows)
    i1 = jnp.zeros_like(best_g)
    v1 = m[0]
    for j in range(1, 4):
        better = m[j] > v1
        i1 = jnp.where(better, j, i1)
        v1 = jnp.where(better, m[j], v1)
    i2 = jnp.full_like(best_g, -1)
    v2 = jnp.full_like(v1, -jnp.inf)
    for j in range(4):
        better = (i1 != j) & ((m[j] > v2) | (i2 < 0))
        i2 = jnp.where(better, j, i2)
        v2 = jnp.where(better, m[j], v2)

    def sel(vals, i):
        out = vals[0]
        for j in range(1, 4):
            out = jnp.where(i == j, vals[j], out)
        return out

    w1, w2 = sel(s, i1), sel(s, i2)
    tot = w1 + w2
    idx_ref[0:1, :] = best_g * EXPERTS_PER_GROUP + i1
    idx_ref[1:2, :] = best_g * EXPERTS_PER_GROUP + i2
    wt_ref[0:1, :] = w1 / tot
    wt_ref[1:2, :] = w2 / tot


def norm_router(xs, gain, mods, shift_col, router_w, router_bias):
    b, l, d = xs.shape
    nt = l // ROW_TILE
    tok = lambda i, t: (0, i * nt + t)
    return pl.pallas_call(
        _norm_router_kernel,
        out_shape=(jax.ShapeDtypeStruct((b, l, d), F32),
                   jax.ShapeDtypeStruct((2, b * l), jnp.int32),
                   jax.ShapeDtypeStruct((2, b * l), F32)),
        grid=(b, nt),
        in_specs=[pl.BlockSpec((1, ROW_TILE, d), lambda i, t: (i, t, 0)),
                  pl.BlockSpec((1, d), lambda i, t: (0, 0)),
                  pl.BlockSpec((1, 1, d), lambda i, t: (_mod_row(i, t), 0, shift_col)),
                  pl.BlockSpec((1, 1, d), lambda i, t: (_mod_row(i, t), 0, shift_col + 1)),
                  pl.BlockSpec((N_EXPERTS, d), lambda i, t: (0, 0)),
                  pl.BlockSpec((N_EXPERTS, 1), lambda i, t: (0, 0))],
        out_specs=(pl.BlockSpec((1, ROW_TILE, d), lambda i, t: (i, t, 0)),
                   pl.BlockSpec((2, ROW_TILE), tok),
                   pl.BlockSpec((2, ROW_TILE), tok)),
        compiler_params=_cparams(("parallel", "parallel"), 48),
        name="norm_router",
    )(xs, gain.reshape(1, d), mods, mods, router_w.T, router_bias.reshape(N_EXPERTS, 1))


def _moe_kernel(bexp_ref, stok_ref, sdst_ref, h_hbm, wg_hbm, wu_hbm, wd_hbm, y_hbm,
                xbuf, ybuf, wg_s, wu_s, wd_s, stage_a, stage_b, gsem, ssem, wsem, *, layer):
    i = pl.program_id(0)
    n = pl.num_programs(0)
    slot = lax.rem(i, 2)
    other = 1 - slot
    d, f = wg_s.shape

    def gather_start(block, buf):
        base = block * MOE_BLOCK
        for r in range(MOE_BLOCK):
            pltpu.make_async_copy(h_hbm.at[pl.ds(stok_ref[base + r], 1)], xbuf.at[buf, pl.ds(r, 1)],
                                  gsem.at[buf]).start()

    def gather_wait(buf):
        pltpu.make_async_copy(h_hbm.at[pl.ds(0, MOE_BLOCK)], xbuf.at[buf], gsem.at[buf]).wait()

    def scatter_start(block, buf):
        base = (block + 1) * MOE_BLOCK
        for r in range(MOE_BLOCK):
            pltpu.make_async_copy(ybuf.at[buf, pl.ds(r, 1)], y_hbm.at[pl.ds(sdst_ref[base + r], 1)],
                                  ssem.at[buf]).start()

    def scatter_wait(buf):
        pltpu.make_async_copy(ybuf.at[buf], y_hbm.at[pl.ds(0, MOE_BLOCK)], ssem.at[buf]).wait()

    def load_expert(e):
        rows_a, rows_b = stage_a.shape[1], stage_b.shape[1]
        chunks = []
        for w_hbm, w_s in ((wg_hbm, wg_s), (wu_hbm, wu_s)):
            for c in range(d // rows_a):
                rows = pl.ds(c * rows_a, rows_a)
                chunks.append((w_hbm.at[layer, e, rows], stage_a, w_s.at[rows]))
        for c in range(f // rows_b):
            rows = pl.ds(c * rows_b, rows_b)
            chunks.append((wd_hbm.at[layer, e, rows], stage_b, wd_s.at[rows]))
        copies = [pltpu.make_async_copy(src, stage.at[j % 2], wsem.at[j % 2])
                  for j, (src, stage, _) in enumerate(chunks)]
        copies[0].start()
        copies[1].start()
        for j, (_, stage, dst) in enumerate(chunks):
            copies[j].wait()
            dst[...] = stage[j % 2].astype(BF16)
            if j + 2 < len(chunks):
                copies[j + 2].start()

    @pl.when(i == 0)
    def _():
        gather_start(0, 0)
        ybuf[1] = jnp.zeros(ybuf.shape[1:], F32)

    @pl.when((i == 0) | (bexp_ref[i] != bexp_ref[jnp.maximum(i - 1, 0)]))
    def _():
        load_expert(bexp_ref[i])

    gather_wait(slot)

    @pl.when(i >= 1)
    def _():
        scatter_wait(slot)

    gather_start(jnp.minimum(i + 1, n - 1), other)
    scatter_start(i - 1, other)
    x = xbuf[slot].astype(BF16)
    dot = functools.partial(jnp.dot, preferred_element_type=F32)
    g = dot(x, wg_s[...])
    u = dot(x, wu_s[...])
    act = (g * _sigmoid(g) * u).astype(BF16)
    ybuf[slot] = dot(act, wd_s[...])

    @pl.when(i == n - 1)
    def _():
        scatter_start(i, slot)
        gather_wait(other)
        scatter_wait(other)
        scatter_wait(slot)


def moe_experts(h_tok, w_gate, w_up, w_down, layer, n_routed, block_expert, slot_tok, slot_dst):
    t, d = h_tok.shape
    n_blocks = block_expert.shape[0]
    f = w_gate.shape[-1]
    hbm = pl.BlockSpec(memory_space=pl.ANY)
    grid_spec = pltpu.PrefetchScalarGridSpec(
        num_scalar_prefetch=3,
        grid=(n_blocks,),
        in_specs=[hbm, hbm, hbm, hbm],
        out_specs=hbm,
        scratch_shapes=[pltpu.VMEM((2, MOE_BLOCK, d), F32), pltpu.VMEM((2, MOE_BLOCK, d), F32),
                        pltpu.VMEM((d, f), BF16), pltpu.VMEM((d, f), BF16), pltpu.VMEM((f, d), BF16),
                        pltpu.VMEM((2, W_STAGE_ELEMS // f, f), F32), pltpu.VMEM((2, W_STAGE_ELEMS // d, d), F32),
                        pltpu.SemaphoreType.DMA((2,)), pltpu.SemaphoreType.DMA((2,)),
                        pltpu.SemaphoreType.DMA((2,))],
    )
    return pl.pallas_call(
        functools.partial(_moe_kernel, layer=layer),
        out_shape=jax.ShapeDtypeStruct((2 * n_routed + MOE_DUMP_ROWS, d), F32),
        grid_spec=grid_spec,
        compiler_params=_cparams(("arbitrary",), 56),
        name="moe_experts",
    )(block_expert, slot_tok, slot_dst, h_tok, w_gate, w_up, w_down)


def route_slots(idx, seq_len, first_row):
    t = idx.shape[1]
    n = 2 * t
    kept = seq_len - first_row
    n_routed = t // seq_len * kept
    flat_e = idx.reshape(-1)
    flat_on = jnp.arange(n, dtype=jnp.int32) % seq_len >= first_row
    is_e = flat_e[:, None] == jnp.arange(N_EXPERTS)[None, :]
    onehot = (is_e & flat_on[:, None]).astype(jnp.int32)
    pieces = onehot.reshape(n // MOE_BLOCK, MOE_BLOCK, N_EXPERTS)
    inner = jnp.cumsum(pieces, axis=1)
    totals = inner[:, -1, :]
    csum = (inner + (jnp.cumsum(totals, axis=0) - totals)[:, None, :]).reshape(n, N_EXPERTS)
    rank = jnp.sum(jnp.where(is_e, csum, 0), axis=1) - 1
    counts = csum[-1]
    padded = (counts + MOE_BLOCK - 1) // MOE_BLOCK * MOE_BLOCK
    pad_end = jnp.cumsum(padded)
    n_blocks = 2 * n_routed // MOE_BLOCK + N_EXPERTS
    cap = n_blocks * MOE_BLOCK
    first_slot = jnp.sum(jnp.where(is_e, (pad_end - padded)[None, :], 0), axis=1)
    dest = jnp.where(flat_on, first_slot + rank, cap)
    slot_src = jnp.full((cap,), -1, jnp.int32).at[dest].set(jnp.arange(n, dtype=jnp.int32), mode='drop')
    starts = jnp.arange(n_blocks, dtype=jnp.int32) * MOE_BLOCK
    last_start = jnp.maximum(pad_end[-1] - MOE_BLOCK, 0)
    ended = pad_end[None, :] <= jnp.minimum(starts, last_start)[:, None]
    block_expert = jnp.minimum(jnp.sum(ended, axis=1), N_EXPERTS - 1).astype(jnp.int32)
    slot_tok = jnp.where(slot_src >= 0, slot_src % t, 0).astype(jnp.int32)
    compact = slot_tok // seq_len * kept + slot_tok % seq_len - first_row
    out_row = (slot_src // t) * n_routed + compact
    pos = jnp.arange(-MOE_BLOCK, cap, dtype=jnp.int32)
    keep = jnp.concatenate([jnp.zeros((MOE_BLOCK,), bool), slot_src >= 0])
    out_row = jnp.concatenate([jnp.zeros((MOE_BLOCK,), jnp.int32), out_row])
    slot_dst = jnp.where(keep, out_row, 2 * n_routed + pos % MOE_DUMP_ROWS).astype(jnp.int32)
    return block_expert, slot_tok, slot_dst


def _combine_kernel(x_ref, y0_ref, y1_ref, w_ref, g_ref, ng_ref, sh_ref, sc_ref, o_ref, *h_ref, final):
    w = w_ref[...]
    x = x_ref[0] + g_ref[0] * (y0_ref[...] * w[:, 0:1] + y1_ref[...] * w[:, 1:2])
    normed = x * lax.rsqrt(jnp.mean(x * x, axis=-1, keepdims=True) + EPS) * ng_ref[...]
    if final:
        o_ref[0] = normed
    else:
        o_ref[0] = x
        h_ref[0][0] = (normed * (1.0 + sc_ref[0]) + sh_ref[0]).astype(BF16)


def moe_combine(xs, yk, wts, mods, gate_col, norm_g, next_mods, final):
    b, l, d = xs.shape
    nt = l // ROW_TILE
    t0 = 1 if final else 0
    n_out = nt - t0
    tile = pl.BlockSpec((1, ROW_TILE, d), lambda i, t: (i, t, 0))
    out_shape = jax.ShapeDtypeStruct((b, n_out * ROW_TILE, d), F32)
    return pl.pallas_call(
        functools.partial(_combine_kernel, final=final),
        out_shape=out_shape if final else (out_shape, jax.ShapeDtypeStruct((b, l, d), BF16)),
        grid=(b, n_out),
        in_specs=[pl.BlockSpec((1, ROW_TILE, d), lambda i, t: (i, t + t0, 0)),
                  pl.BlockSpec((ROW_TILE, d), lambda i, t: (i * n_out + t, 0)),
                  pl.BlockSpec((ROW_TILE, d), lambda i, t: (b * n_out + i * n_out + t, 0)),
                  pl.BlockSpec((ROW_TILE, 2), lambda i, t: (i * nt + t + t0, 0)),
                  pl.BlockSpec((1, 1, d), lambda i, t: (_mod_row(i, t + t0), 0, gate_col)),
                  pl.BlockSpec((1, d), lambda i, t: (0, 0)),
                  pl.BlockSpec((1, 1, d), lambda i, t: (_mod_row(i, t + t0), 0, 0)),
                  pl.BlockSpec((1, 1, d), lambda i, t: (_mod_row(i, t + t0), 0, 1))],
        out_specs=tile if final else (tile, tile),
        compiler_params=_cparams(("parallel", "parallel"), 48),
        name="moe_combine",
    )(xs, yk, yk, wts, mods, norm_g.reshape(1, d), next_mods, next_mods)


def _layer(xs, h, layer, mods, norm2_g, w_in, conv_qkv_w, a_log, dt_bias, dn_norm_g, cf_dw_w,
           cf_ln_g, cf_ln_b, w_out, router_w, router_bias, w_gate, w_up, w_down, out_norm_g, next_mods, final):
    b, l, d = xs.shape
    in_cols = w_in.shape[-1]
    h = h.reshape(b * l, d)
    p2d = matmul_nt(h, jnp.swapaxes(w_in, 1, 2), layer, 1536, 512)
    p = p2d.reshape(b, l, in_cols)
    aux = delta_gates(p2d, a_log, dt_bias).reshape(b, l, LANES)
    dn = deltanet_heads(p, aux, conv_qkv_w, dn_norm_g)
    cfh = conformer_heads(p, cf_dw_w, cf_ln_g, cf_ln_b, False)
    cfv = conformer_heads(p, cf_dw_w, cf_ln_g, cf_ln_b, True)
    xs = out_proj_residual(dn, cfh, cfv, w_out, layer, xs, mods, 2)

    h2, idx, wts = norm_router(xs, norm2_g, mods, 3, router_w, router_bias)
    first_row = CTX_LEN if final else 0
    n_routed = b * (l - first_row)
    block_expert, slot_tok, slot_dst = route_slots(idx, l, first_row)
    yk = moe_experts(h2.reshape(b * l, d), w_gate, w_up, w_down, layer, n_routed, block_expert, slot_tok,
                     slot_dst)
    return moe_combine(xs, yk, wts.T, mods, 5, out_norm_g, next_mods, final)


def kernel(x, c, ctx, c_ctx, w_ada, b_ada, norm1_g, norm2_g, w_in, conv_qkv_w, a_log, dt_bias,
           dn_norm_g, cf_dw_w, cf_ln_g, cf_ln_b, w_out, router_w, router_bias, w_gate, w_up,
           w_down, final_g):
    depth = w_ada.shape[0]
    bsz = x.shape[0]
    xs = jnp.concatenate([ctx, x], axis=1)
    c_all = jnp.concatenate([c, c_ctx[None, :], jnp.zeros((8 - bsz - 1, c.shape[1]), F32)], axis=0)
    mods = ada_mods(c_all, w_ada, b_ada)
    table = lambda i: mods[i].reshape(8, 1, -1)
    h = norm_modulate(xs, norm1_g[0], table(0), 0, BF16)
    for i in range(depth):
        final = i == depth - 1
        out = _layer(xs, h, i, table(i), norm2_g[i], w_in, conv_qkv_w[i], a_log[i], dt_bias[i],
                     dn_norm_g[i], cf_dw_w[i], cf_ln_g[i], cf_ln_b[i], w_out, router_w, router_bias,
                     w_gate, w_up, w_down, final_g if final else norm1_g[i + 1],
                     table(i) if final else table(i + 1), final)
        if final:
            return out
        xs, h = out
```
